```python
import math
import jax, jax.numpy as jnp
from jax import lax
import numpy as np

D_MODEL = 2048
BATCH = 4
SEQ = 2048
DEPTH = 2
DEC_BATCH = 128
DEC_SEQ = 4
PAST_LEN = 16384
PAGE_SIZE = 128

N_MIXERS = 2
N_SSM_LAYERS = (DEPTH + 1) // 2
N_GMLP_LAYERS = DEPTH // 2
SSM_GROUP = 16
SSM_GROUPS = D_MODEL // SSM_GROUP
SSM_STATE = 64
DT_MIN = 0.001
DT_MAX = 0.1
CHUNK = 128
GMLP_WIDTH = D_MODEL
GMLP_HEADS = 16
GMLP_HEAD_DIM = GMLP_WIDTH // GMLP_HEADS
FFN_HIDDEN = ((8 * D_MODEL + 3 * 256 - 1) // (3 * 256)) * 256
EPS = 1e-6

kernel_name = "hybrid_s5_gmlp_decoder_step"


def rms_norm(x, g):
    xf = x.astype(jnp.float32)
    y = xf * lax.rsqrt(jnp.mean(xf * xf, axis=-1, keepdims=True) + EPS)
    return (y * g.astype(jnp.float32)).astype(x.dtype)


def layer_norm(x, g):
    xf = x.astype(jnp.float32)
    xc = xf - jnp.mean(xf, axis=-1, keepdims=True)
    y = xc * lax.rsqrt(jnp.mean(xc * xc, axis=-1, keepdims=True) + EPS)
    return (y * g.astype(jnp.float32)).astype(x.dtype)


def ssm_discretize(lambda_re, lambda_im, log_dt, b_re, b_im):
    dt = jnp.exp(log_dt.astype(jnp.float32))[:, None]
    lr = lambda_re.astype(jnp.float32)
    li = lambda_im.astype(jnp.float32)
    mag = jnp.exp(lr * dt)
    a_re = mag * jnp.cos(li * dt)
    a_im = mag * jnp.sin(li * dt)
    den = lr * lr + li * li
    nr = a_re - 1.0
    ni = a_im
    q_re = (nr * lr + ni * li) / den
    q_im = (ni * lr - nr * li) / den
    br = b_re.astype(jnp.float32)
    bi = b_im.astype(jnp.float32)
    bb_re = q_re[..., None] * br - q_im[..., None] * bi
    bb_im = q_re[..., None] * bi + q_im[..., None] * br
    return a_re, a_im, bb_re, bb_im


def ssm_combine(e1, e2):
    a1r, a1i, b1r, b1i = e1
    a2r, a2i, b2r, b2i = e2
    ar = a2r * a1r - a2i * a1i
    ai = a2r * a1i + a2i * a1r
    br = a2r * b1r - a2i * b1i + b2r
    bi = a2r * b1i + a2i * b1r + b2i
    return (ar, ai, br, bi)


def ssm_mixer(h, h0_re, h0_im, w_in, lambda_re, lambda_im, log_dt, b_re, b_im, c_re, c_im, d_skip, w_out):
    bsz, t, _ = h.shape
    u = (h @ w_in).astype(jnp.float32).reshape(bsz, t, SSM_GROUPS, SSM_GROUP)
    a_re, a_im, bb_re, bb_im = ssm_discretize(lambda_re, lambda_im, log_dt, b_re, b_im)
    bu_re = jnp.einsum('btgc,gpc->btgp', u, bb_re)
    bu_im = jnp.einsum('btgc,gpc->btgp', u, bb_im)
    s_re = h0_re.astype(jnp.float32)
    s_im = h0_im.astype(jnp.float32)
    bu_re = bu_re.at[:, 0].add(a_re * s_re - a_im * s_im)
    bu_im = bu_im.at[:, 0].add(a_re * s_im + a_im * s_re)
    ar = jnp.broadcast_to(a_re, bu_re.shape)
    ai = jnp.broadcast_to(a_im, bu_im.shape)
    _, _, st_re, st_im = lax.associative_scan(ssm_combine, (ar, ai, bu_re, bu_im), axis=1)
    y = (jnp.einsum('btgp,gcp->btgc', st_re, c_re.astype(jnp.float32))
         - jnp.einsum('btgp,gcp->btgc', st_im, c_im.astype(jnp.float32)))
    y = y.reshape(bsz, t, D_MODEL) + d_skip.astype(jnp.float32) * u.reshape(bsz, t, D_MODEL)
    z = jax.nn.gelu(y).astype(h.dtype)
    zo = z @ w_out
    out = zo[..., :D_MODEL] * jax.nn.sigmoid(zo[..., D_MODEL:])
    return out, st_re[:, -1], st_im[:, -1]


def gmlp_mixer(h, w_in, v_gain, w_spatial, b_spatial, w_out):
    bsz, t, _ = h.shape
    z = jax.nn.gelu(h @ w_in)
    u = z[..., :GMLP_WIDTH]
    v = layer_norm(z[..., GMLP_WIDTH:], v_gain)
    n_chunks = -(-t // CHUNK)
    pad = n_chunks * CHUNK - t
    vp = jnp.pad(v, ((0, 0), (0, pad), (0, 0))).reshape(bsz, n_chunks, CHUNK, GMLP_HEADS, GMLP_HEAD_DIM)
    mask = jnp.tril(jnp.ones((CHUNK, CHUNK), dtype=bool))
    ws = jnp.where(mask[None], w_spatial, jnp.zeros((), w_spatial.dtype))
    s = jnp.einsum('hqk,bnkhd->bnqhd', ws, vp) + jnp.transpose(b_spatial)[None, None, :, :, None]
    s = s.reshape(bsz, n_chunks * CHUNK, GMLP_WIDTH)[:, :t]
    out = (u * s) @ w_out
    return out, v


def swiglu(h, w_gate_up, w_down):
    gu = h @ w_gate_up
    return (jax.nn.silu(gu[..., :FFN_HIDDEN]) * gu[..., FFN_HIDDEN:]) @ w_down


def trunk(x, h0_re, h0_im, p):
    new_re, new_im, new_v = [], [], []
    for i in range(DEPTH):
        j = i // N_MIXERS
        hn = rms_norm(x, p['norm_mix'][i])
        if i % N_MIXERS == 0:
            mix, sr, si = ssm_mixer(hn, h0_re[j], h0_im[j], p['ssm_w_in'][j], p['ssm_lambda_re'][j],
                                    p['ssm_lambda_im'][j], p['ssm_log_dt'][j], p['ssm_b_re'][j],
                                    p['ssm_b_im'][j], p['ssm_c_re'][j], p['ssm_c_im'][j],
                                    p['ssm_d'][j], p['ssm_w_out'][j])
            new_re.append(sr)
            new_im.append(si)
        else:
            mix, v = gmlp_mixer(hn, p['gmlp_w_in'][j], p['gmlp_v_gain'][j], p['gmlp_w_spatial'][j],
                                p['gmlp_b_spatial'][j], p['gmlp_w_out'][j])
            new_v.append(v)
        x = x + mix
        x = x + swiglu(rms_norm(x, p['norm_ffn'][i]), p['ffn_w_gate_up'][i], p['ffn_w_down'][i])
    return rms_norm(x, p['norm_final']), jnp.stack(new_re), jnp.stack(new_im), jnp.stack(new_v)


def setup_inputs(seed: int = 0) -> dict:
    key = jax.random.key(seed)
    ks = jax.random.split(key, 24)
    f32 = jnp.float32
    nA, nB, G, P, C = N_SSM_LAYERS, N_GMLP_LAYERS, SSM_GROUPS, SSM_STATE, SSM_GROUP
    nrm = lambda k, shape, s: jax.random.normal(k, shape, f32) * s
    lambda_re = -0.5 + nrm(ks[4], (nA, G, P), 0.01)
    lambda_im = math.pi * jnp.arange(P, dtype=f32)[None, None, :] + nrm(ks[5], (nA, G, P), 0.01)
    log_dt = jax.random.uniform(ks[6], (nA, G), f32, math.log(DT_MIN), math.log(DT_MAX))
    return {
        'x_prompt': nrm(ks[0], (BATCH, SEQ, D_MODEL), 1.0),
        'x_sample': nrm(ks[1], (DEC_BATCH, DEC_SEQ, D_MODEL), 1.0),
        'state_ssm_re': nrm(ks[2], (nA, DEC_BATCH, G, P), 0.5),
        'state_ssm_im': nrm(ks[3], (nA, DEC_BATCH, G, P), 0.5),
        'norm_mix': 1.0 + nrm(ks[7], (DEPTH, D_MODEL), 0.02),
        'norm_ffn': 1.0 + nrm(ks[8], (DEPTH, D_MODEL), 0.02),
        'norm_final': 1.0 + nrm(ks[9], (D_MODEL,), 0.02),
        'ssm_w_in': nrm(ks[10], (nA, D_MODEL, D_MODEL), D_MODEL ** -0.5),
        'ssm_lambda_re': lambda_re,
        'ssm_lambda_im': lambda_im,
        'ssm_log_dt': log_dt,
        'ssm_b_re': nrm(ks[11], (nA, G, P, C), (2.0 * C) ** -0.5),
        'ssm_b_im': nrm(ks[12], (nA, G, P, C), (2.0 * C) ** -0.5),
        'ssm_c_re': nrm(ks[13], (nA, G, C, P), (2.0 * P) ** -0.5),
        'ssm_c_im': nrm(ks[14], (nA, G, C, P), (2.0 * P) ** -0.5),
        'ssm_d': nrm(ks[15], (nA, D_MODEL), 1.0),
        'ssm_w_out': nrm(ks[16], (nA, D_MODEL, 2 * D_MODEL), D_MODEL ** -0.5),
        'gmlp_w_in': nrm(ks[17], (nB, D_MODEL, 2 * GMLP_WIDTH), D_MODEL ** -0.5),
        'gmlp_v_gain': 1.0 + nrm(ks[18], (nB, GMLP_WIDTH), 0.02),
        'gmlp_w_spatial': nrm(ks[19], (nB, GMLP_HEADS, CHUNK, CHUNK), CHUNK ** -0.5),
        'gmlp_b_spatial': 1.0 + nrm(ks[20], (nB, GMLP_HEADS, CHUNK), 0.02),
        'gmlp_w_out': nrm(ks[21], (nB, GMLP_WIDTH, D_MODEL), GMLP_WIDTH ** -0.5),
        'ffn_w_gate_up': nrm(ks[22], (DEPTH, D_MODEL, 2 * FFN_HIDDEN), D_MODEL ** -0.5),
        'ffn_w_down': nrm(ks[23], (DEPTH, FFN_HIDDEN, D_MODEL), FFN_HIDDEN ** -0.5),
    }


def reference(x_prompt, x_sample, state_ssm_re, state_ssm_im, norm_mix, norm_ffn, norm_final,
              ssm_w_in, ssm_lambda_re, ssm_lambda_im, ssm_log_dt, ssm_b_re, ssm_b_im, ssm_c_re, ssm_c_im,
              ssm_d, ssm_w_out, gmlp_w_in, gmlp_v_gain, gmlp_w_spatial, gmlp_b_spatial, gmlp_w_out,
              ffn_w_gate_up, ffn_w_down):
    p = dict(norm_mix=norm_mix, norm_ffn=norm_ffn, norm_final=norm_final,
             ssm_w_in=ssm_w_in, ssm_lambda_re=ssm_lambda_re, ssm_lambda_im=ssm_lambda_im,
             ssm_log_dt=ssm_log_dt, ssm_b_re=ssm_b_re, ssm_b_im=ssm_b_im, ssm_c_re=ssm_c_re,
             ssm_c_im=ssm_c_im, ssm_d=ssm_d, ssm_w_out=ssm_w_out,
             gmlp_w_in=gmlp_w_in, gmlp_v_gain=gmlp_v_gain, gmlp_w_spatial=gmlp_w_spatial,
             gmlp_b_spatial=gmlp_b_spatial, gmlp_w_out=gmlp_w_out,
             ffn_w_gate_up=ffn_w_gate_up, ffn_w_down=ffn_w_down)
    zeros = jnp.zeros((N_SSM_LAYERS, x_prompt.shape[0], SSM_GROUPS, SSM_STATE), jnp.float32)
    y_prompt, new_ssm_re_prompt, new_ssm_im_prompt, _ = trunk(x_prompt, zeros, zeros, p)
    y_sample, new_ssm_re_sample, new_ssm_im_sample, new_gmlp_v_sample = trunk(x_sample, state_ssm_re, state_ssm_im, p)
    return (y_prompt, y_sample, new_ssm_re_prompt, new_ssm_im_prompt,
            new_ssm_re_sample, new_ssm_im_sample, new_gmlp_v_sample)
```

```python
import functools

import jax
import jax.numpy as jnp
from jax import lax
from jax.experimental import pallas as pl
from jax.experimental.pallas import tpu as pltpu

D_MODEL = 2048
SSM_GROUPS = 128
SSM_GROUP = 16
SSM_STATE = 64
SSM_LANES = SSM_GROUPS * SSM_STATE
GMLP_WIDTH = D_MODEL
GMLP_HEADS = 16
GMLP_HEAD_DIM = GMLP_WIDTH // GMLP_HEADS
CHUNK = 128
FFN_HIDDEN = 5632
EPS = 1e-6

LANES = 128
SUBLANES = 8
MXU_DIM = 256
VMEM_LIMIT = 56 * 1024 * 1024

PAIRS = SSM_GROUPS // 2
QUAD_K = 4 * 2 * SSM_GROUP
OCTS = D_MODEL // MXU_DIM
OCT_K = SSM_LANES // OCTS

BF16 = jnp.bfloat16
F32 = jnp.float32


def _resident(shape):
    zeros = (0,) * len(shape)
    return pl.BlockSpec(shape, lambda *_: zeros, pipeline_mode=pl.Buffered(1))


def _params(semantics):
    return pltpu.CompilerParams(dimension_semantics=semantics, vmem_limit_bytes=VMEM_LIMIT)


def _rms(x, g):
    ms = jnp.mean(x * x, axis=-1, keepdims=True)
    return (x * lax.rsqrt(ms + EPS)) * g


def _dot(a, b):
    return jnp.dot(a, b, preferred_element_type=F32)


def _discretize_kernel(lr_ref, li_ref, ldt_ref, br_ref, bi_ref, are_ref, aim_ref, bbr_ref, bbi_ref):
    lr = lr_ref[...]
    li = li_ref[...]
    dt = jnp.exp(ldt_ref[...])
    mag = jnp.exp(lr * dt)
    a_re = mag * jnp.cos(li * dt)
    a_im = mag * jnp.sin(li * dt)
    den = lr * lr + li * li
    nr = a_re - 1.0
    ni = a_im
    q_re = (nr * lr + ni * li) / den
    q_im = (ni * lr - nr * li) / den
    br = br_ref[...]
    bi = bi_ref[...]
    are_ref[...] = a_re
    aim_ref[...] = a_im
    bbr_ref[...] = q_re * br - q_im * bi
    bbi_ref[...] = q_re * bi + q_im * br


def _discretize(lambda_re, lambda_im, log_dt, b_re, b_im):
    g, p, c = b_re.shape
    full = (g, c, p)
    flat = (g * c * p // LANES, LANES)
    args = [
        jnp.broadcast_to(lambda_re[:, None, :], full).reshape(flat),
        jnp.broadcast_to(lambda_im[:, None, :], full).reshape(flat),
        jnp.broadcast_to(log_dt[:, None, None], full).reshape(flat),
        jnp.transpose(b_re, (0, 2, 1)).reshape(flat),
        jnp.transpose(b_im, (0, 2, 1)).reshape(flat),
    ]
    out = jax.ShapeDtypeStruct(flat, F32)
    a_re, a_im, bb_re, bb_im = pl.pallas_call(
        _discretize_kernel, out_shape=(out, out, out, out), name="ssm_discretize")(*args)
    a_re = a_re.reshape(full)[:, 0, :]
    a_im = a_im.reshape(full)[:, 0, :]
    return a_re, a_im, bb_re.reshape(full), bb_im.reshape(full)


def _ssm_tables(a_re, a_im, bb_re, bb_im, c_re, c_im):
    c, p = SSM_GROUP, SSM_STATE
    eye2 = jnp.eye(2, dtype=F32)

    def in_blocks(bb):
        return jnp.einsum('jgcp,gh->jgchp', bb.reshape(PAIRS, 2, c, p), eye2).reshape(PAIRS, 2 * c, 2 * p)

    blk = jnp.concatenate([in_blocks(bb_re), in_blocks(bb_im)], axis=-1)
    blk = blk.reshape(PAIRS // 4, 4, 2 * c, MXU_DIM)
    w_b = jnp.einsum('qjrn,jk->qjkrn', blk, jnp.eye(4, dtype=F32)).reshape(PAIRS, QUAD_K, MXU_DIM)

    eye16 = jnp.eye(16, dtype=F32)

    def out_blocks(cm):
        return jnp.einsum('ogcp,gh->ogphc', cm.reshape(OCTS, 16, c, p), eye16).reshape(OCTS, OCT_K, MXU_DIM)

    a_re_t = jnp.broadcast_to(a_re.reshape(1, SSM_LANES), (SUBLANES, SSM_LANES))
    a_im_t = jnp.broadcast_to(a_im.reshape(1, SSM_LANES), (SUBLANES, SSM_LANES))
    return (w_b.astype(BF16), a_re_t, a_im_t, out_blocks(c_re).astype(BF16), out_blocks(-c_im).astype(BF16))


def _norm_matmul_kernel(x_ref, g_ref, w_ref, o_ref):
    hn = _rms(x_ref[...], g_ref[...]).astype(BF16)
    o_ref[...] = _dot(hn, w_ref[...])


def _norm_matmul(x, g, w, tm):
    m, d = x.shape
    n = w.shape[1]
    return pl.pallas_call(
        _norm_matmul_kernel,
        grid=(m // tm,),
        in_specs=[pl.BlockSpec((tm, d), lambda i: (i, 0)), _resident((1, d)), _resident((d, n))],
        out_specs=pl.BlockSpec((tm, n), lambda i: (i, 0)),
        out_shape=jax.ShapeDtypeStruct((m, n), F32),
        compiler_params=_params(("parallel",)),
        name="ssm_norm_in_proj",
    )(x, g.reshape(1, d), w)


SCAN_COLS = 1024


def _ssm_scan_kernel(u_ref, wb_ref, are_ref, aim_ref, wcr_ref, wci_ref, dsk_ref, h0r_ref, h0i_ref,
                     z_ref, sr_ref, si_ref, bur_ref, bui_ref, *, nb, tt):
    @pl.when(pl.program_id(1) == 0)
    def _():
        sr_ref[...] = h0r_ref[...]
        si_ref[...] = h0i_ref[...]

    u = u_ref[...]
    ub = u.astype(BF16)
    for j in range(PAIRS):
        q = j // 4
        res = _dot(ub[:, QUAD_K * q:QUAD_K * (q + 1)], wb_ref[j])
        bur_ref[:, LANES * j:LANES * (j + 1)] = res[:, :LANES]
        bui_ref[:, LANES * j:LANES * (j + 1)] = res[:, LANES:]

    for cb in range(SSM_LANES // SCAN_COLS):
        cols = slice(cb * SCAN_COLS, (cb + 1) * SCAN_COLS)
        ar = are_ref[:, cols]
        ai = aim_ref[:, cols]
        for sg in range(nb // SUBLANES):
            seqs = slice(sg * SUBLANES, (sg + 1) * SUBLANES)
            sr = sr_ref[seqs, cols]
            si = si_ref[seqs, cols]
            for t in range(tt):
                rows = slice(t * nb + sg * SUBLANES, t * nb + (sg + 1) * SUBLANES)
                nr = ar * sr - ai * si + bur_ref[rows, cols]
                ni = ar * si + ai * sr + bui_ref[rows, cols]
                sr, si = nr, ni
                bur_ref[rows, cols] = sr
                bui_ref[rows, cols] = si
            sr_ref[seqs, cols] = sr
            si_ref[seqs, cols] = si

    for o in range(OCTS):
        kk = slice(o * OCT_K, (o + 1) * OCT_K)
        nn = slice(o * MXU_DIM, (o + 1) * MXU_DIM)
        y = _dot(bur_ref[:, kk].astype(BF16), wcr_ref[o]) + _dot(bui_ref[:, kk].astype(BF16), wci_ref[o])
        y = y + dsk_ref[:, nn] * u[:, nn]
        z_ref[:, nn] = jax.nn.gelu(y).astype(BF16)


def _ssm_scan(u, tables, d_skip, h0_re, h0_im, *, nb, tt, n_bt, n_tt):
    w_b, a_re_t, a_im_t, w_cr, w_ci = tables
    m, d = u.shape
    r = nb * tt
    assert m == r * n_bt * n_tt and nb % SUBLANES == 0
    rows = pl.BlockSpec((r, d), lambda b, t: (b * n_tt + t, 0))
    state = pl.BlockSpec((nb, SSM_LANES), lambda b, t: (b, 0))
    st_shape = jax.ShapeDtypeStruct((nb * n_bt, SSM_LANES), F32)
    return pl.pallas_call(
        functools.partial(_ssm_scan_kernel, nb=nb, tt=tt),
        grid=(n_bt, n_tt),
        in_specs=[rows, _resident(w_b.shape), _resident(a_re_t.shape), _resident(a_im_t.shape),
                  _resident(w_cr.shape), _resident(w_ci.shape), _resident((1, d)), state, state],
        out_specs=(rows, state, state),
        out_shape=(jax.ShapeDtypeStruct((m, d), BF16), st_shape, st_shape),
        scratch_shapes=[pltpu.VMEM((r, SSM_LANES), F32), pltpu.VMEM((r, SSM_LANES), F32)],
        compiler_params=_params(("parallel", "arbitrary")),
        name="ssm_scan",
    )(u, w_b, a_re_t, a_im_t, w_cr, w_ci, d_skip.reshape(1, d), h0_re, h0_im)


def _ssm_out_kernel(z_ref, x_ref, w_ref, o_ref):
    zo = _dot(z_ref[...], w_ref[...])
    o_ref[...] = x_ref[...] + zo[:, :D_MODEL] * jax.nn.sigmoid(zo[:, D_MODEL:])


def _ssm_out(z, x, w, tm):
    m, d = x.shape
    rows = pl.BlockSpec((tm, d), lambda i: (i, 0))
    return pl.pallas_call(
        _ssm_out_kernel,
        grid=(m // tm,),
        in_specs=[rows, rows, _resident(w.shape)],
        out_specs=rows,
        out_shape=jax.ShapeDtypeStruct((m, d), F32),
        compiler_params=_params(("parallel",)),
        name="ssm_out_glu",
    )(z, x, w)


def _ffn_kernel(x_ref, g_ref, wg_ref, wu_ref, wd_ref, gf_ref, o_ref, hn_ref, *, final_norm):
    h = pl.program_id(1)

    @pl.when(h == 0)
    def _():
        x = x_ref[...]
        hn_ref[...] = _rms(x, g_ref[...]).astype(BF16)
        o_ref[...] = x

    hn = hn_ref[...]
    act = (jax.nn.silu(_dot(hn, wg_ref[...])) * _dot(hn, wu_ref[...])).astype(BF16)
    o_ref[...] += _dot(act, wd_ref[...])

    if final_norm:
        @pl.when(h == pl.num_programs(1) - 1)
        def _():
            o_ref[...] = _rms(o_ref[...], gf_ref[...])


def _ffn(x, g, w_gate_up, w_down, g_final, *, tm, th, final_norm):
    m, d = x.shape
    n_h = FFN_HIDDEN // th
    rows = pl.BlockSpec((tm, d), lambda i, h: (i, 0))
    return pl.pallas_call(
        functools.partial(_ffn_kernel, final_norm=final_norm),
        grid=(m // tm, n_h),
        in_specs=[pl.BlockSpec((tm, d), lambda i, h: (i, 0), pipeline_mode=pl.Buffered(1)),
                  _resident((1, d)),
                  pl.BlockSpec((d, th), lambda i, h: (0, h)),
                  pl.BlockSpec((d, th), lambda i, h: (0, n_h + h)),
                  pl.BlockSpec((th, d), lambda i, h: (h, 0)),
                  _resident((1, d))],
        out_specs=rows,
        out_shape=jax.ShapeDtypeStruct((m, d), F32),
        scratch_shapes=[pltpu.VMEM((tm, d), BF16)],
        compiler_params=_params(("parallel", "arbitrary")),
        name="ffn_swiglu",
    )(x, g.reshape(1, d), w_gate_up, w_gate_up, w_down, g_final.reshape(1, d))


def _gmlp_gate_inputs(x, g, w_in, v_gain):
    hn = _rms(x, g).astype(BF16)
    z = jax.nn.gelu(_dot(hn, w_in))
    u = z[:, :GMLP_WIDTH]
    v = z[:, GMLP_WIDTH:]
    vc = v - jnp.mean(v, axis=-1, keepdims=True)
    vn = (vc * lax.rsqrt(jnp.mean(vc * vc, axis=-1, keepdims=True) + EPS)) * v_gain
    return u, vn


def _gmlp_prompt_kernel(x_ref, g_ref, win_ref, vg_ref, ws_ref, bs_ref, wout_ref, o_ref, gate_ref):
    x = x_ref[...]
    u, vn = _gmlp_gate_inputs(x, g_ref[...], win_ref[...], vg_ref[...])
    vb = vn.astype(BF16)
    q_idx = lax.broadcasted_iota(jnp.int32, (CHUNK, CHUNK), 0)
    k_idx = lax.broadcasted_iota(jnp.int32, (CHUNK, CHUNK), 1)
    causal = k_idx <= q_idx
    for h in range(GMLP_HEADS):
        cols = slice(h * GMLP_HEAD_DIM, (h + 1) * GMLP_HEAD_DIM)
        ws = jnp.where(causal, ws_ref[h], 0.0).astype(BF16)
        bias = bs_ref[:, h:h + 1]
        for c in range(x.shape[0] // CHUNK):
            rows = slice(c * CHUNK, (c + 1) * CHUNK)
            s = _dot(ws, vb[rows, cols]) + bias
            gate_ref[rows, cols] = (u[rows, cols] * s).astype(BF16)
    o_ref[...] = x + _dot(gate_ref[...], wout_ref[...])


def _gmlp_sample_kernel(x_ref, g_ref, win_ref, vg_ref, wq_ref, bq_ref, wout_ref, o_ref, v_ref, *, steps):
    x = x_ref[...]
    u, vn = _gmlp_gate_inputs(x, g_ref[...], win_ref[...], vg_ref[...])
    v_ref[...] = vn
    nseq = x.shape[0] // steps
    gates = []
    for q in range(steps):
        s = bq_ref[q:q + 1, :]
        for k in range(q + 1):
            s = s + wq_ref[q * steps + k:q * steps + k + 1, :] * vn[k * nseq:(k + 1) * nseq, :]
        gates.append(u[q * nseq:(q + 1) * nseq, :] * s)
    gate = jnp.concatenate(gates, axis=0).astype(BF16)
    o_ref[...] = x + _dot(gate, wout_ref[...])


def _gmlp_prompt(x, g, w_in, v_gain, w_spatial, b_spatial_t, w_out, tm):
    m, d = x.shape
    rows = pl.BlockSpec((tm, d), lambda i: (i, 0))
    return pl.pallas_call(
        _gmlp_prompt_kernel,
        grid=(m // tm,),
        in_specs=[rows, _resident((1, d)), _resident(w_in.shape), _resident((1, GMLP_WIDTH)),
                  _resident(w_spatial.shape), _resident(b_spatial_t.shape), _resident(w_out.shape)],
        out_specs=rows,
        out_shape=jax.ShapeDtypeStruct((m, d), F32),
        scratch_shapes=[pltpu.VMEM((tm, GMLP_WIDTH), BF16)],
        compiler_params=_params(("parallel",)),
        name="gmlp_prompt",
    )(x, g.reshape(1, d), w_in, v_gain.reshape(1, GMLP_WIDTH), w_spatial, b_spatial_t, w_out)


def _gmlp_sample(x, g, w_in, v_gain, w_q, b_q, w_out, tm, steps):
    m, d = x.shape
    rows = pl.BlockSpec((tm, d), lambda i: (i, 0))
    return pl.pallas_call(
        functools.partial(_gmlp_sample_kernel, steps=steps),
        grid=(m // tm,),
        in_specs=[rows, _resident((1, d)), _resident(w_in.shape), _resident((1, GMLP_WIDTH)),
                  _resident(w_q.shape), _resident(b_q.shape), _resident(w_out.shape)],
        out_specs=(rows, pl.BlockSpec((tm, GMLP_WIDTH), lambda i: (i, 0))),
        out_shape=(jax.ShapeDtypeStruct((m, d), F32), jax.ShapeDtypeStruct((m, GMLP_WIDTH), F32)),
        compiler_params=_params(("parallel",)),
        name="gmlp_sample",
    )(x, g.reshape(1, d), w_in, v_gain.reshape(1, GMLP_WIDTH), w_q, b_q, w_out)


PROMPT_SCAN_SEQS = SUBLANES
PROMPT_SCAN_STEPS = 16
SAMPLE_TILE_SEQS = 32


def kernel(x_prompt, x_sample, state_ssm_re, state_ssm_im, norm_mix, norm_ffn, norm_final, ssm_w_in, ssm_lambda_re, ssm_lambda_im, ssm_log_dt, ssm_b_re, ssm_b_im, ssm_c_re, ssm_c_im, ssm_d, ssm_w_out, gmlp_w_in, gmlp_v_gain, gmlp_w_spatial, gmlp_b_spatial, gmlp_w_out, ffn_w_gate_up, ffn_w_down):
    bsz, seq, d = x_prompt.shape
    dbsz, dseq, _ = x_sample.shape

    w_in0 = ssm_w_in[0].astype(BF16)
    w_out0 = ssm_w_out[0].astype(BF16)
    g_w_in = gmlp_w_in[0].astype(BF16)
    g_w_out = gmlp_w_out[0].astype(BF16)
    w_gu = ffn_w_gate_up.astype(BF16)
    w_dn = ffn_w_down.astype(BF16)

    a_re, a_im, bb_re, bb_im = _discretize(ssm_lambda_re[0], ssm_lambda_im[0], ssm_log_dt[0],
                                           ssm_b_re[0], ssm_b_im[0])
    tables = _ssm_tables(a_re, a_im, bb_re, bb_im, ssm_c_re[0], ssm_c_im[0])

    def layer0(x_rows, scan_fn, tm, ffn_tm):
        u = _norm_matmul(x_rows, norm_mix[0], w_in0, tm)
        z, s_re, s_im = scan_fn(u)
        x1 = _ssm_out(z, x_rows, w_out0, tm)
        x2 = _ffn(x1, norm_ffn[0], w_gu[0], w_dn[0], norm_final, tm=ffn_tm, th=512, final_norm=False)
        return x2, s_re, s_im

    def ffn1(x_rows, ffn_tm):
        return _ffn(x_rows, norm_ffn[1], w_gu[1], w_dn[1], norm_final, tm=ffn_tm, th=512, final_norm=True)

    xp = jnp.transpose(x_prompt, (1, 0, 2)).reshape(seq * bsz, d)
    pad_seqs = PROMPT_SCAN_SEQS - bsz
    zeros_p = jnp.zeros((PROMPT_SCAN_SEQS, SSM_LANES), F32)

    def prompt_scan(u):
        u_pad = jnp.pad(u.reshape(seq, bsz, d), ((0, 0), (0, pad_seqs), (0, 0))).reshape(seq * PROMPT_SCAN_SEQS, d)
        z_pad, s_re, s_im = _ssm_scan(u_pad, tables, ssm_d[0], zeros_p, zeros_p, nb=PROMPT_SCAN_SEQS,
                                      tt=PROMPT_SCAN_STEPS, n_bt=1, n_tt=seq // PROMPT_SCAN_STEPS)
        z = z_pad.reshape(seq, PROMPT_SCAN_SEQS, d)[:, :bsz].reshape(seq * bsz, d)
        return z, s_re[:bsz], s_im[:bsz]

    xp2, p_re, p_im = layer0(xp, prompt_scan, 512, 1024)
    xp2 = jnp.transpose(xp2.reshape(seq, bsz, d), (1, 0, 2)).reshape(bsz * seq, d)
    xp3 = _gmlp_prompt(xp2, norm_mix[1], g_w_in, gmlp_v_gain[0], gmlp_w_spatial[0],
                       jnp.transpose(gmlp_b_spatial[0]), g_w_out, 256)
    y_prompt = ffn1(xp3, 1024).reshape(bsz, seq, d)

    n_bt = dbsz // SAMPLE_TILE_SEQS
    tile_rows = SAMPLE_TILE_SEQS * dseq

    def to_rows(a):
        return jnp.transpose(a.reshape(n_bt, SAMPLE_TILE_SEQS, dseq, d), (0, 2, 1, 3)).reshape(dbsz * dseq, d)

    def from_rows(a):
        return jnp.transpose(a.reshape(n_bt, dseq, SAMPLE_TILE_SEQS, d), (0, 2, 1, 3)).reshape(dbsz, dseq, d)

    xs = to_rows(x_sample)
    h0_re = state_ssm_re[0].reshape(dbsz, SSM_LANES)
    h0_im = state_ssm_im[0].reshape(dbsz, SSM_LANES)
    sample_scan = functools.partial(_ssm_scan, tables=tables, d_skip=ssm_d[0], h0_re=h0_re, h0_im=h0_im,
                                    nb=SAMPLE_TILE_SEQS, tt=dseq, n_bt=n_bt, n_tt=1)
    xs2, s_re, s_im = layer0(xs, sample_scan, tile_rows, dbsz * dseq)
    w_q = jnp.repeat(gmlp_w_spatial[0][:, :dseq, :dseq].reshape(GMLP_HEADS, dseq * dseq).T, GMLP_HEAD_DIM, axis=1)
    b_q = jnp.repeat(gmlp_b_spatial[0][:, :dseq].T, GMLP_HEAD_DIM, axis=1)
    xs3, v_rows = _gmlp_sample(xs2, norm_mix[1], g_w_in, gmlp_v_gain[0], w_q, b_q, g_w_out, tile_rows, dseq)
    y_sample = from_rows(ffn1(xs3, dbsz * dseq))

    state_shape = (1, -1, SSM_GROUPS, SSM_STATE)
    return (y_prompt, y_sample,
            p_re.reshape(state_shape), p_im.reshape(state_shape),
            s_re.reshape(state_shape), s_im.reshape(state_shape),
            from_rows(v_rows)[None])
```

```python
import functools

import jax
import jax.numpy as jnp
from jax import lax
from jax.experimental import pallas as pl
from jax.experimental.pallas import tpu as pltpu

D_MODEL = 2048
SSM_GROUPS = 128
SSM_GROUP = 16
SSM_STATE = 64
SSM_LANES = SSM_GROUPS * SSM_STATE
GMLP_WIDTH = D_MODEL
GMLP_HEADS = 16
GMLP_HEAD_DIM = GMLP_WIDTH // GMLP_HEADS
CHUNK = 128
FFN_HIDDEN = 5632
EPS = 1e-6

LANES = 128
SUBLANES = 8
MXU_DIM = 256
VMEM_LIMIT = 56 * 1024 * 1024

PAIRS = SSM_GROUPS // 2
QUAD_K = 4 * 2 * SSM_GROUP
OCTS = D_MODEL // MXU_DIM
OCT_K = SSM_LANES // OCTS

BF16 = jnp.bfloat16
F32 = jnp.float32


def _resident(shape):
    zeros = (0,) * len(shape)
    return pl.BlockSpec(shape, lambda *_: zeros, pipeline_mode=pl.Buffered(1))


def _params(semantics):
    return pltpu.CompilerParams(dimension_semantics=semantics, vmem_limit_bytes=VMEM_LIMIT)


def _rms(x, g):
    ms = jnp.mean(x * x, axis=-1, keepdims=True)
    return (x * lax.rsqrt(ms + EPS)) * g


def _dot(a, b):
    return jnp.dot(a, b, preferred_element_type=F32)


def _discretize_kernel(lr_ref, li_ref, ldt_ref, br_ref, bi_ref, are_ref, aim_ref, bbr_ref, bbi_ref):
    lr = lr_ref[...]
    li = li_ref[...]
    dt = jnp.exp(ldt_ref[...])
    mag = jnp.exp(lr * dt)
    a_re = mag * jnp.cos(li * dt)
    a_im = mag * jnp.sin(li * dt)
    den = lr * lr + li * li
    nr = a_re - 1.0
    ni = a_im
    q_re = (nr * lr + ni * li) / den
    q_im = (ni * lr - nr * li) / den
    br = br_ref[...]
    bi = bi_ref[...]
    are_ref[...] = a_re
    aim_ref[...] = a_im
    bbr_ref[...] = q_re * br - q_im * bi
    bbi_ref[...] = q_re * bi + q_im * br


def _discretize(lambda_re, lambda_im, log_dt, b_re, b_im):
    g, p, c = b_re.shape
    full = (g, c, p)
    flat = (g * c * p // LANES, LANES)
    args = [
        jnp.broadcast_to(lambda_re[:, None, :], full).reshape(flat),
        jnp.broadcast_to(lambda_im[:, None, :], full).reshape(flat),
        jnp.broadcast_to(log_dt[:, None, None], full).reshape(flat),
        jnp.transpose(b_re, (0, 2, 1)).reshape(flat),
        jnp.transpose(b_im, (0, 2, 1)).reshape(flat),
    ]
    out = jax.ShapeDtypeStruct(flat, F32)
    a_re, a_im, bb_re, bb_im = pl.pallas_call(
        _discretize_kernel, out_shape=(out, out, out, out), name="ssm_discretize")(*args)
    a_re = a_re.reshape(full)[:, 0, :]
    a_im = a_im.reshape(full)[:, 0, :]
    return a_re, a_im, bb_re.reshape(full), bb_im.reshape(full)


def _ssm_tables(a_re, a_im, bb_re, bb_im, c_re, c_im):
    c, p = SSM_GROUP, SSM_STATE
    eye2 = jnp.eye(2, dtype=F32)

    def in_blocks(bb):
        return jnp.einsum('jgcp,gh->jgchp', bb.reshape(PAIRS, 2, c, p), eye2).reshape(PAIRS, 2 * c, 2 * p)

    blk = jnp.concatenate([in_blocks(bb_re), in_blocks(bb_im)], axis=-1)
    blk = blk.reshape(PAIRS // 4, 4, 2 * c, MXU_DIM)
    w_b = jnp.einsum('qjrn,jk->qjkrn', blk, jnp.eye(4, dtype=F32)).reshape(PAIRS, QUAD_K, MXU_DIM)

    eye16 = jnp.eye(16, dtype=F32)

    def out_blocks(cm):
        return jnp.einsum('ogcp,gh->ogphc', cm.reshape(OCTS, 16, c, p), eye16).reshape(OCTS, OCT_K, MXU_DIM)

    a_re_t = jnp.broadcast_to(a_re.reshape(1, SSM_LANES), (SUBLANES, SSM_LANES))
    a_im_t = jnp.broadcast_to(a_im.reshape(1, SSM_LANES), (SUBLANES, SSM_LANES))
    return (w_b.astype(BF16), a_re_t, a_im_t, out_blocks(c_re).astype(BF16), out_blocks(-c_im).astype(BF16))


def _norm_matmul_kernel(x_ref, g_ref, w_ref, o_ref):
    hn = _rms(x_ref[...], g_ref[...]).astype(BF16)
    o_ref[...] = _dot(hn, w_ref[...])


def _norm_matmul(x, g, w, tm):
    m, d = x.shape
    n = w.shape[1]
    return pl.pallas_call(
        _norm_matmul_kernel,
        grid=(m // tm,),
        in_specs=[pl.BlockSpec((tm, d), lambda i: (i, 0)), _resident((1, d)), _resident((d, n))],
        out_specs=pl.BlockSpec((tm, n), lambda i: (i, 0)),
        out_shape=jax.ShapeDtypeStruct((m, n), F32),
        compiler_params=_params(("parallel",)),
        name="ssm_norm_in_proj",
    )(x, g.reshape(1, d), w)


SCAN_COLS = 1024


def _ssm_scan_kernel(u_ref, wb_ref, are_ref, aim_ref, wcr_ref, wci_ref, dsk_ref, h0r_ref, h0i_ref,
                     z_ref, sr_ref, si_ref, bur_ref, bui_ref, *, nb, tt):
    @pl.when(pl.program_id(1) == 0)
    def _():
        sr_ref[...] = h0r_ref[...]
        si_ref[...] = h0i_ref[...]

    u = u_ref[...]
    ub = u.astype(BF16)
    for j in range(PAIRS):
        q = j // 4
        res = _dot(ub[:, QUAD_K * q:QUAD_K * (q + 1)], wb_ref[j])
        bur_ref[:, LANES * j:LANES * (j + 1)] = res[:, :LANES]
        bui_ref[:, LANES * j:LANES * (j + 1)] = res[:, LANES:]

    for cb in range(SSM_LANES // SCAN_COLS):
        cols = slice(cb * SCAN_COLS, (cb + 1) * SCAN_COLS)
        ar = are_ref[:, cols]
        ai = aim_ref[:, cols]
        for sg in range(nb // SUBLANES):
            seqs = slice(sg * SUBLANES, (sg + 1) * SUBLANES)
            sr = sr_ref[seqs, cols]
            si = si_ref[seqs, cols]
            for t in range(tt):
                rows = slice(t * nb + sg * SUBLANES, t * nb + (sg + 1) * SUBLANES)
                nr = ar * sr - ai * si + bur_ref[rows, cols]
                ni = ar * si + ai * sr + bui_ref[rows, cols]
                sr, si = nr, ni
                bur_ref[rows, cols] = sr
                bui_ref[rows, cols] = si
            sr_ref[seqs, cols] = sr
            si_ref[seqs, cols] = si

    for o in range(OCTS):
        kk = slice(o * OCT_K, (o + 1) * OCT_K)
        nn = slice(o * MXU_DIM, (o + 1) * MXU_DIM)
        y = _dot(bur_ref[:, kk].astype(BF16), wcr_ref[o]) + _dot(bui_ref[:, kk].astype(BF16), wci_ref[o])
        y = y + dsk_ref[:, nn] * u[:, nn]
        z_ref[:, nn] = jax.nn.gelu(y).astype(BF16)


def _ssm_scan(u, tables, d_skip, h0_re, h0_im, *, nb, tt, n_bt, n_tt):
    w_b, a_re_t, a_im_t, w_cr, w_ci = tables
    m, d = u.shape
    r = nb * tt
    assert m == r * n_bt * n_tt and nb % SUBLANES == 0
    rows = pl.BlockSpec((r, d), lambda b, t: (b * n_tt + t, 0))
    state = pl.BlockSpec((nb, SSM_LANES), lambda b, t: (b, 0))
    st_shape = jax.ShapeDtypeStruct((nb * n_bt, SSM_LANES), F32)
    return pl.pallas_call(
        functools.partial(_ssm_scan_kernel, nb=nb, tt=tt),
        grid=(n_bt, n_tt),
        in_specs=[rows, _resident(w_b.shape), _resident(a_re_t.shape), _resident(a_im_t.shape),
                  _resident(w_cr.shape), _resident(w_ci.shape), _resident((1, d)), state, state],
        out_specs=(rows, state, state),
        out_shape=(jax.ShapeDtypeStruct((m, d), BF16), st_shape, st_shape),
        scratch_shapes=[pltpu.VMEM((r, SSM_LANES), F32), pltpu.VMEM((r, SSM_LANES), F32)],
        compiler_params=_params(("parallel", "arbitrary")),
        name="ssm_scan",
    )(u, w_b, a_re_t, a_im_t, w_cr, w_ci, d_skip.reshape(1, d), h0_re, h0_im)


QUADS = PAIRS // 4
HALF_SLABS = PAIRS // 2
SCAN_SLABS = 8


def _ssm_scan_split_kernel(u_ref, wb_ref, are_ref, aim_ref, wcr_ref, wci_ref, dsk_ref,
                           z_ref, sr_ref, si_ref, lhs_ref, bur_ref, bui_ref, y_ref, *, nb, tt):
    r = nb * tt
    i = pl.program_id(0)

    @pl.when(i == 0)
    def _():
        lhs_ref[...] = jnp.zeros_like(lhs_ref)
        sr_ref[...] = jnp.zeros_like(sr_ref)
        si_ref[...] = jnp.zeros_like(si_ref)

    for q in range(QUADS):
        for b in range(nb):
            blk = u_ref[b, :, LANES * q:LANES * (q + 1)]
            for h in range(2):
                lhs_ref[q, h, pl.ds(2 * b + h, tt, stride=2 * nb), :] = blk

    for q in range(QUADS):
        lhs = jnp.concatenate([lhs_ref[q, 0], lhs_ref[q, 1]], axis=1).astype(BF16)
        res = _dot(lhs, wb_ref[q])
        for s in range(2):
            bur_ref[2 * q + s] = res[:, MXU_DIM * s:MXU_DIM * s + LANES]
            bui_ref[2 * q + s] = res[:, MXU_DIM * s + LANES:MXU_DIM * (s + 1)]

    for k0 in range(0, HALF_SLABS, SCAN_SLABS):
        slabs = range(k0, k0 + SCAN_SLABS)
        ar = [are_ref[:, LANES * k:LANES * (k + 1)] for k in slabs]
        ai = [aim_ref[:, LANES * k:LANES * (k + 1)] for k in slabs]
        sr = [sr_ref[:, LANES * k:LANES * (k + 1)] for k in slabs]
        si = [si_ref[:, LANES * k:LANES * (k + 1)] for k in slabs]
        for t in range(tt):
            rows = slice(SUBLANES * t, SUBLANES * (t + 1))
            for n, k in enumerate(slabs):
                nr = ar[n] * sr[n] - ai[n] * si[n] + bur_ref[k, rows, :]
                ni = ar[n] * si[n] + ai[n] * sr[n] + bui_ref[k, rows, :]
                sr[n], si[n] = nr, ni
                bur_ref[k, rows, :] = nr
                bui_ref[k, rows, :] = ni
        for n, k in enumerate(slabs):
            sr_ref[:, LANES * k:LANES * (k + 1)] = sr[n]
            si_ref[:, LANES * k:LANES * (k + 1)] = si[n]

    for o in range(OCTS):
        order = [(2 * (2 * o + ql) + s, h) for ql in range(2) for h in range(2) for s in range(2)]
        lre = jnp.concatenate([bur_ref[k, pl.ds(h, r, stride=2), :] for k, h in order], axis=1).astype(BF16)
        lim = jnp.concatenate([bui_ref[k, pl.ds(h, r, stride=2), :] for k, h in order], axis=1).astype(BF16)
        y = _dot(lre, wcr_ref[o]) + _dot(lim, wci_ref[o])
        for half in range(2):
            y_ref[half] = y[:, LANES * half:LANES * (half + 1)]
        for b in range(nb):
            for half in range(2):
                cols = slice(MXU_DIM * o + LANES * half, MXU_DIM * o + LANES * (half + 1))
                yb = y_ref[half, pl.ds(b, tt, stride=nb), :] + dsk_ref[:, cols] * u_ref[b, :, cols]
                z_ref[b, :, cols] = jax.nn.gelu(yb).astype(BF16)


def _ssm_scan_split(u, tables, d_skip, tt):
    w_b, a_re_t, a_im_t, w_cr, w_ci = tables
    nb, seq, d = u.shape
    assert 2 * nb == SUBLANES and seq % tt == 0
    w_b2 = jnp.transpose(w_b.reshape(QUADS, 2, 2, QUAD_K, MXU_DIM), (0, 1, 3, 2, 4)).reshape(QUADS, 2 * QUAD_K, 2 * MXU_DIM)

    def split_lanes(a):
        halves = jnp.transpose(a[0].reshape(QUADS, 2, 2 * LANES), (1, 0, 2)).reshape(2, HALF_SLABS * LANES)
        return jnp.tile(halves, (nb, 1))

    rows = pl.BlockSpec((nb, tt, d), lambda i: (0, i, 0))
    st_shape = jax.ShapeDtypeStruct((SUBLANES, HALF_SLABS * LANES), F32)
    r = nb * tt
    z, s_re, s_im = pl.pallas_call(
        functools.partial(_ssm_scan_split_kernel, nb=nb, tt=tt),
        grid=(seq // tt,),
        in_specs=[rows, _resident(w_b2.shape), _resident(st_shape.shape), _resident(st_shape.shape),
                  _resident(w_cr.shape), _resident(w_ci.shape), _resident((1, d))],
        out_specs=(rows, pl.BlockSpec(st_shape.shape, lambda i: (0, 0)), pl.BlockSpec(st_shape.shape, lambda i: (0, 0))),
        out_shape=(jax.ShapeDtypeStruct((nb, seq, d), BF16), st_shape, st_shape),
        scratch_shapes=[pltpu.VMEM((QUADS, 2, 2 * r, LANES), F32),
                        pltpu.VMEM((HALF_SLABS, 2 * r, LANES), F32),
                        pltpu.VMEM((HALF_SLABS, 2 * r, LANES), F32),
                        pltpu.VMEM((2, r, LANES), F32)],
        compiler_params=_params(("arbitrary",)),
        name="ssm_scan_split",
    )(u, w_b2, split_lanes(a_re_t), split_lanes(a_im_t), w_cr, w_ci, d_skip.reshape(1, d))

    def join(s):
        return jnp.transpose(s.reshape(nb, 2, QUADS, 2 * LANES), (0, 2, 1, 3)).reshape(nb, SSM_LANES)

    return z, join(s_re), join(s_im)


def _ssm_out_kernel(z_ref, x_ref, w_ref, o_ref):
    zo = _dot(z_ref[...], w_ref[...])
    o_ref[...] = x_ref[...] + zo[:, :D_MODEL] * jax.nn.sigmoid(zo[:, D_MODEL:])


def _ssm_out(z, x, w, tm):
    m, d = x.shape
    rows = pl.BlockSpec((tm, d), lambda i: (i, 0))
    return pl.pallas_call(
        _ssm_out_kernel,
        grid=(m // tm,),
        in_specs=[rows, rows, _resident(w.shape)],
        out_specs=rows,
        out_shape=jax.ShapeDtypeStruct((m, d), F32),
        compiler_params=_params(("parallel",)),
        name="ssm_out_glu",
    )(z, x, w)


def _ffn_kernel(x_ref, g_ref, wg_ref, wu_ref, wd_ref, gf_ref, o_ref, hn_ref, *, final_norm):
    h = pl.program_id(1)

    @pl.when(h == 0)
    def _():
        x = x_ref[...]
        hn_ref[...] = _rms(x, g_ref[...]).astype(BF16)
        o_ref[...] = x

    hn = hn_ref[...]
    act = (jax.nn.silu(_dot(hn, wg_ref[...])) * _dot(hn, wu_ref[...])).astype(BF16)
    o_ref[...] += _dot(act, wd_ref[...])

    if final_norm:
        @pl.when(h == pl.num_programs(1) - 1)
        def _():
            o_ref[...] = _rms(o_ref[...], gf_ref[...])


def _ffn(x, g, w_gate_up, w_down, g_final, *, tm, th, final_norm):
    m, d = x.shape
    n_h = FFN_HIDDEN // th
    rows = pl.BlockSpec((tm, d), lambda i, h: (i, 0))
    return pl.pallas_call(
        functools.partial(_ffn_kernel, final_norm=final_norm),
        grid=(m // tm, n_h),
        in_specs=[pl.BlockSpec((tm, d), lambda i, h: (i, 0), pipeline_mode=pl.Buffered(1)),
                  _resident((1, d)),
                  pl.BlockSpec((d, th), lambda i, h: (0, h)),
                  pl.BlockSpec((d, th), lambda i, h: (0, n_h + h)),
                  pl.BlockSpec((th, d), lambda i, h: (h, 0)),
                  _resident((1, d))],
        out_specs=rows,
        out_shape=jax.ShapeDtypeStruct((m, d), F32),
        scratch_shapes=[pltpu.VMEM((tm, d), BF16)],
        compiler_params=_params(("parallel", "arbitrary")),
        name="ffn_swiglu",
    )(x, g.reshape(1, d), w_gate_up, w_gate_up, w_down, g_final.reshape(1, d))


def _gmlp_gate_inputs(x, g, w_in, v_gain):
    hn = _rms(x, g).astype(BF16)
    z = jax.nn.gelu(_dot(hn, w_in))
    u = z[:, :GMLP_WIDTH]
    v = z[:, GMLP_WIDTH:]
    vc = v - jnp.mean(v, axis=-1, keepdims=True)
    vn = (vc * lax.rsqrt(jnp.mean(vc * vc, axis=-1, keepdims=True) + EPS)) * v_gain
    return u, vn


def _gmlp_prompt_kernel(x_ref, g_ref, win_ref, vg_ref, ws_ref, bs_ref, wout_ref, o_ref, gate_ref):
    x = x_ref[...]
    u, vn = _gmlp_gate_inputs(x, g_ref[...], win_ref[...], vg_ref[...])
    vb = vn.astype(BF16)
    q_idx = lax.broadcasted_iota(jnp.int32, (CHUNK, CHUNK), 0)
    k_idx = lax.broadcasted_iota(jnp.int32, (CHUNK, CHUNK), 1)
    causal = k_idx <= q_idx
    for h in range(GMLP_HEADS):
        cols = slice(h * GMLP_HEAD_DIM, (h + 1) * GMLP_HEAD_DIM)
        ws = jnp.where(causal, ws_ref[h], 0.0).astype(BF16)
        bias = bs_ref[:, h:h + 1]
        for c in range(x.shape[0] // CHUNK):
            rows = slice(c * CHUNK, (c + 1) * CHUNK)
            s = _dot(ws, vb[rows, cols]) + bias
            gate_ref[rows, cols] = (u[rows, cols] * s).astype(BF16)
    o_ref[...] = x + _dot(gate_ref[...], wout_ref[...])


def _gmlp_sample_kernel(x_ref, g_ref, win_ref, vg_ref, wq_ref, bq_ref, wout_ref, o_ref, v_ref, *, steps):
    x = x_ref[...]
    u, vn = _gmlp_gate_inputs(x, g_ref[...], win_ref[...], vg_ref[...])
    v_ref[...] = vn
    nseq = x.shape[0] // steps
    gates = []
    for q in range(steps):
        s = bq_ref[q:q + 1, :]
        for k in range(q + 1):
            s = s + wq_ref[q * steps + k:q * steps + k + 1, :] * vn[k * nseq:(k + 1) * nseq, :]
        gates.append(u[q * nseq:(q + 1) * nseq, :] * s)
    gate = jnp.concatenate(gates, axis=0).astype(BF16)
    o_ref[...] = x + _dot(gate, wout_ref[...])


def _gmlp_prompt(x, g, w_in, v_gain, w_spatial, b_spatial_t, w_out, tm):
    m, d = x.shape
    rows = pl.BlockSpec((tm, d), lambda i: (i, 0))
    return pl.pallas_call(
        _gmlp_prompt_kernel,
        grid=(m // tm,),
        in_specs=[rows, _resident((1, d)), _resident(w_in.shape), _resident((1, GMLP_WIDTH)),
                  _resident(w_spatial.shape), _resident(b_spatial_t.shape), _resident(w_out.shape)],
        out_specs=rows,
        out_shape=jax.ShapeDtypeStruct((m, d), F32),
        scratch_shapes=[pltpu.VMEM((tm, GMLP_WIDTH), BF16)],
        compiler_params=_params(("parallel",)),
        name="gmlp_prompt",
    )(x, g.reshape(1, d), w_in, v_gain.reshape(1, GMLP_WIDTH), w_spatial, b_spatial_t, w_out)


def _gmlp_sample(x, g, w_in, v_gain, w_q, b_q, w_out, tm, steps):
    m, d = x.shape
    rows = pl.BlockSpec((tm, d), lambda i: (i, 0))
    return pl.pallas_call(
        functools.partial(_gmlp_sample_kernel, steps=steps),
        grid=(m // tm,),
        in_specs=[rows, _resident((1, d)), _resident(w_in.shape), _resident((1, GMLP_WIDTH)),
                  _resident(w_q.shape), _resident(b_q.shape), _resident(w_out.shape)],
        out_specs=(rows, pl.BlockSpec((tm, GMLP_WIDTH), lambda i: (i, 0))),
        out_shape=(jax.ShapeDtypeStruct((m, d), F32), jax.ShapeDtypeStruct((m, GMLP_WIDTH), F32)),
        compiler_params=_params(("parallel",)),
        name="gmlp_sample",
    )(x, g.reshape(1, d), w_in, v_gain.reshape(1, GMLP_WIDTH), w_q, b_q, w_out)


PROMPT_SCAN_STEPS = 32
SAMPLE_TILE_SEQS = 32


def kernel(x_prompt, x_sample, state_ssm_re, state_ssm_im, norm_mix, norm_ffn, norm_final, ssm_w_in, ssm_lambda_re, ssm_lambda_im, ssm_log_dt, ssm_b_re, ssm_b_im, ssm_c_re, ssm_c_im, ssm_d, ssm_w_out, gmlp_w_in, gmlp_v_gain, gmlp_w_spatial, gmlp_b_spatial, gmlp_w_out, ffn_w_gate_up, ffn_w_down):
    bsz, seq, d = x_prompt.shape
    dbsz, dseq, _ = x_sample.shape

    w_in0 = ssm_w_in[0].astype(BF16)
    w_out0 = ssm_w_out[0].astype(BF16)
    g_w_in = gmlp_w_in[0].astype(BF16)
    g_w_out = gmlp_w_out[0].astype(BF16)
    w_gu = ffn_w_gate_up.astype(BF16)
    w_dn = ffn_w_down.astype(BF16)

    a_re, a_im, bb_re, bb_im = _discretize(ssm_lambda_re[0], ssm_lambda_im[0], ssm_log_dt[0],
                                           ssm_b_re[0], ssm_b_im[0])
    tables = _ssm_tables(a_re, a_im, bb_re, bb_im, ssm_c_re[0], ssm_c_im[0])

    def layer0(x_rows, scan_fn, tm, ffn_tm):
        u = _norm_matmul(x_rows, norm_mix[0], w_in0, tm)
        z, s_re, s_im = scan_fn(u)
        x1 = _ssm_out(z, x_rows, w_out0, tm)
        x2 = _ffn(x1, norm_ffn[0], w_gu[0], w_dn[0], norm_final, tm=ffn_tm, th=512, final_norm=False)
        return x2, s_re, s_im

    def ffn1(x_rows, ffn_tm):
        return _ffn(x_rows, norm_ffn[1], w_gu[1], w_dn[1], norm_final, tm=ffn_tm, th=512, final_norm=True)

    xp = x_prompt.reshape(bsz * seq, d)

    def prompt_scan(u):
        z, s_re, s_im = _ssm_scan_split(u.reshape(bsz, seq, d), tables, ssm_d[0], PROMPT_SCAN_STEPS)
        return z.reshape(bsz * seq, d), s_re, s_im

    xp2, p_re, p_im = layer0(xp, prompt_scan, 512, 1024)
    xp3 = _gmlp_prompt(xp2, norm_mix[1], g_w_in, gmlp_v_gain[0], gmlp_w_spatial[0],
                       jnp.transpose(gmlp_b_spatial[0]), g_w_out, 256)
    y_prompt = ffn1(xp3, 1024).reshape(bsz, seq, d)

    n_bt = dbsz // SAMPLE_TILE_SEQS
    tile_rows = SAMPLE_TILE_SEQS * dseq

    def to_rows(a):
        return jnp.transpose(a.reshape(n_bt, SAMPLE_TILE_SEQS, dseq, d), (0, 2, 1, 3)).reshape(dbsz * dseq, d)

    def from_rows(a):
        return jnp.transpose(a.reshape(n_bt, dseq, SAMPLE_TILE_SEQS, d), (0, 2, 1, 3)).reshape(dbsz, dseq, d)

    xs = to_rows(x_sample)
    h0_re = state_ssm_re[0].reshape(dbsz, SSM_LANES)
    h0_im = state_ssm_im[0].reshape(dbsz, SSM_LANES)
    sample_scan = functools.partial(_ssm_scan, tables=tables, d_skip=ssm_d[0], h0_re=h0_re, h0_im=h0_im,
                                    nb=SAMPLE_TILE_SEQS, tt=dseq, n_bt=n_bt, n_tt=1)
    xs2, s_re, s_im = layer0(xs, sample_scan, tile_rows, dbsz * dseq)
    w_q = jnp.repeat(gmlp_w_spatial[0][:, :dseq, :dseq].reshape(GMLP_HEADS, dseq * dseq).T, GMLP_HEAD_DIM, axis=1)
    b_q = jnp.repeat(gmlp_b_spatial[0][:, :dseq].T, GMLP_HEAD_DIM, axis=1)
    xs3, v_rows = _gmlp_sample(xs2, norm_mix[1], g_w_in, gmlp_v_gain[0], w_q, b_q, g_w_out, tile_rows, dseq)
    y_sample = from_rows(ffn1(xs3, dbsz * dseq))

    state_shape = (1, -1, SSM_GROUPS, SSM_STATE)
    return (y_prompt, y_sample,
            p_re.reshape(state_shape), p_im.reshape(state_shape),
            s_re.reshape(state_shape), s_im.reshape(state_shape),
            from_rows(v_rows)[None])
```

```python
import functools

import jax
import jax.numpy as jnp
from jax import lax
from jax.experimental import pallas as pl
from jax.experimental.pallas import tpu as pltpu

D_MODEL = 2048
SSM_GROUPS = 128
SSM_GROUP = 16
SSM_STATE = 64
SSM_LANES = SSM_GROUPS * SSM_STATE
GMLP_WIDTH = D_MODEL
GMLP_HEADS = 16
GMLP_HEAD_DIM = GMLP_WIDTH // GMLP_HEADS
CHUNK = 128
FFN_HIDDEN = 5632
EPS = 1e-6

LANES = 128
SUBLANES = 8
MXU_DIM = 256
VMEM_LIMIT = 56 * 1024 * 1024

PAIRS = SSM_GROUPS // 2
QUAD_K = 4 * 2 * SSM_GROUP
QUADS = PAIRS // 4
HALF_SLABS = PAIRS // 2
OCTS = D_MODEL // MXU_DIM
OCT_K = SSM_LANES // OCTS

BF16 = jnp.bfloat16
F32 = jnp.float32


def _resident(shape):
    zeros = (0,) * len(shape)
    return pl.BlockSpec(shape, lambda *_: zeros, pipeline_mode=pl.Buffered(1))


def _params(semantics):
    return pltpu.CompilerParams(dimension_semantics=semantics, vmem_limit_bytes=VMEM_LIMIT)


def _rms(x, g):
    ms = jnp.mean(x * x, axis=-1, keepdims=True)
    return (x * lax.rsqrt(ms + EPS)) * g


def _dot(a, b):
    return jnp.dot(a, b, preferred_element_type=F32)


def _discretize_kernel(lr_ref, li_ref, ldt_ref, br_ref, bi_ref, are_ref, aim_ref, bbr_ref, bbi_ref):
    lr = lr_ref[...]
    li = li_ref[...]
    dt = jnp.exp(ldt_ref[...])
    mag = jnp.exp(lr * dt)
    a_re = mag * jnp.cos(li * dt)
    a_im = mag * jnp.sin(li * dt)
    den = lr * lr + li * li
    nr = a_re - 1.0
    ni = a_im
    q_re = (nr * lr + ni * li) / den
    q_im = (ni * lr - nr * li) / den
    br = br_ref[...]
    bi = bi_ref[...]
    are_ref[...] = a_re
    aim_ref[...] = a_im
    bbr_ref[...] = q_re * br - q_im * bi
    bbi_ref[...] = q_re * bi + q_im * br


def _discretize(lambda_re, lambda_im, log_dt, b_re, b_im):
    g, p, c = b_re.shape
    full = (g, c, p)
    flat = (g * c * p // LANES, LANES)
    args = [
        jnp.broadcast_to(lambda_re[:, None, :], full).reshape(flat),
        jnp.broadcast_to(lambda_im[:, None, :], full).reshape(flat),
        jnp.broadcast_to(log_dt[:, None, None], full).reshape(flat),
        jnp.transpose(b_re, (0, 2, 1)).reshape(flat),
        jnp.transpose(b_im, (0, 2, 1)).reshape(flat),
    ]
    out = jax.ShapeDtypeStruct(flat, F32)
    a_re, a_im, bb_re, bb_im = pl.pallas_call(
        _discretize_kernel, out_shape=(out, out, out, out), name="ssm_discretize")(*args)
    a_re = a_re.reshape(full)[:, 0, :]
    a_im = a_im.reshape(full)[:, 0, :]
    return a_re, a_im, bb_re.reshape(full), bb_im.reshape(full)


def _ssm_tables(a_re, a_im, bb_re, bb_im, c_re, c_im):
    c, p = SSM_GROUP, SSM_STATE
    idx = jnp.arange
    hit = idx(8)[None, :, None, None] == (4 * idx(2)[:, None, None, None] + 2 * idx(2)[None, None, :, None]
                                          + idx(2)[None, None, None, :])
    bb = jnp.stack([bb_re, bb_im], axis=0).reshape(2, QUADS, 8, c, p)
    bb = jnp.transpose(bb, (1, 2, 3, 0, 4))
    w_b = jnp.where(hit[None, :, :, None, :, None, :, None], bb[:, None, :, :, None, :, None, :], 0.0)
    w_b = w_b.astype(BF16).reshape(QUADS, 2 * QUAD_K, 2 * MXU_DIM)

    groups = MXU_DIM // c
    eye = (idx(groups)[:, None] == idx(groups)[None, :])[None, :, None, :, None]

    def out_blocks(cm):
        cm = jnp.transpose(cm.reshape(OCTS, groups, c, p), (0, 1, 3, 2))
        return jnp.where(eye, cm[:, :, :, None, :], 0.0).astype(BF16).reshape(OCTS, OCT_K, MXU_DIM)

    a_re_t = jnp.broadcast_to(a_re.reshape(1, SSM_LANES), (SUBLANES, SSM_LANES))
    a_im_t = jnp.broadcast_to(a_im.reshape(1, SSM_LANES), (SUBLANES, SSM_LANES))
    return (w_b, a_re_t, a_im_t, out_blocks(c_re), out_blocks(-c_im))


def _norm_matmul_kernel(x_ref, g_ref, w_ref, o_ref):
    hn = _rms(x_ref[...], g_ref[...]).astype(BF16)
    o_ref[...] = _dot(hn, w_ref[...])


def _norm_matmul(x, g, w, tm):
    m, d = x.shape
    n = w.shape[1]
    return pl.pallas_call(
        _norm_matmul_kernel,
        grid=(m // tm,),
        in_specs=[pl.BlockSpec((tm, d), lambda i: (i, 0)), _resident((1, d)), _resident((d, n))],
        out_specs=pl.BlockSpec((tm, n), lambda i: (i, 0)),
        out_shape=jax.ShapeDtypeStruct((m, n), F32),
        compiler_params=_params(("parallel",)),
        name="ssm_norm_in_proj",
    )(x, g.reshape(1, d), w)


SCAN_COLS = 1024


def _ssm_scan_kernel(u_ref, wb_ref, are_ref, aim_ref, wcr_ref, wci_ref, dsk_ref, h0r_ref, h0i_ref,
                     z_ref, sr_ref, si_ref, bur_ref, bui_ref, *, nb, tt):
    @pl.when(pl.program_id(1) == 0)
    def _():
        sr_ref[...] = h0r_ref[...]
        si_ref[...] = h0i_ref[...]

    u = u_ref[...]
    ub = u.astype(BF16)
    for q in range(QUADS):
        lhs = ub[:, QUAD_K * q:QUAD_K * (q + 1)]
        for h in range(2):
            res = _dot(lhs, wb_ref[q, QUAD_K * h:QUAD_K * (h + 1), :])
            for s in range(2):
                j = 4 * q + 2 * h + s
                bur_ref[:, LANES * j:LANES * (j + 1)] = res[:, MXU_DIM * s:MXU_DIM * s + LANES]
                bui_ref[:, LANES * j:LANES * (j + 1)] = res[:, MXU_DIM * s + LANES:MXU_DIM * (s + 1)]

    for cb in range(SSM_LANES // SCAN_COLS):
        cols = slice(cb * SCAN_COLS, (cb + 1) * SCAN_COLS)
        ar = are_ref[:, cols]
        ai = aim_ref[:, cols]
        for sg in range(nb // SUBLANES):
            seqs = slice(sg * SUBLANES, (sg + 1) * SUBLANES)
            sr = sr_ref[seqs, cols]
            si = si_ref[seqs, cols]
            for t in range(tt):
                rows = slice(t * nb + sg * SUBLANES, t * nb + (sg + 1) * SUBLANES)
                nr = ar * sr - ai * si + bur_ref[rows, cols]
                ni = ar * si + ai * sr + bui_ref[rows, cols]
                sr, si = nr, ni
                bur_ref[rows, cols] = sr
                bui_ref[rows, cols] = si
            sr_ref[seqs, cols] = sr
            si_ref[seqs, cols] = si

    for o in range(OCTS):
        kk = slice(o * OCT_K, (o + 1) * OCT_K)
        nn = slice(o * MXU_DIM, (o + 1) * MXU_DIM)
        y = _dot(bur_ref[:, kk].astype(BF16), wcr_ref[o]) + _dot(bui_ref[:, kk].astype(BF16), wci_ref[o])
        y = y + dsk_ref[:, nn] * u[:, nn]
        z_ref[:, nn] = jax.nn.gelu(y).astype(BF16)


def _ssm_scan(u, tables, d_skip, h0_re, h0_im, *, nb, tt, n_bt, n_tt):
    w_b, a_re_t, a_im_t, w_cr, w_ci = tables
    m, d = u.shape
    r = nb * tt
    assert m == r * n_bt * n_tt and nb % SUBLANES == 0
    rows = pl.BlockSpec((r, d), lambda b, t: (b * n_tt + t, 0))
    state = pl.BlockSpec((nb, SSM_LANES), lambda b, t: (b, 0))
    st_shape = jax.ShapeDtypeStruct((nb * n_bt, SSM_LANES), F32)
    return pl.pallas_call(
        functools.partial(_ssm_scan_kernel, nb=nb, tt=tt),
        grid=(n_bt, n_tt),
        in_specs=[rows, _resident(w_b.shape), _resident(a_re_t.shape), _resident(a_im_t.shape),
                  _resident(w_cr.shape), _resident(w_ci.shape), _resident((1, d)), state, state],
        out_specs=(rows, state, state),
        out_shape=(jax.ShapeDtypeStruct((m, d), BF16), st_shape, st_shape),
        scratch_shapes=[pltpu.VMEM((r, SSM_LANES), F32), pltpu.VMEM((r, SSM_LANES), F32)],
        compiler_params=_params(("parallel", "arbitrary")),
        name="ssm_scan",
    )(u, w_b, a_re_t, a_im_t, w_cr, w_ci, d_skip.reshape(1, d), h0_re, h0_im)


SCAN_SLABS = 8


def _ssm_scan_split_kernel(u_ref, wb_ref, are_ref, aim_ref, wcr_ref, wci_ref, dsk_ref,
                           z_ref, sr_ref, si_ref, lhs_ref, bur_ref, bui_ref, y_ref, *, nb, tt):
    r = nb * tt
    i = pl.program_id(0)

    @pl.when(i == 0)
    def _():
        lhs_ref[...] = jnp.zeros_like(lhs_ref)
        sr_ref[...] = jnp.zeros_like(sr_ref)
        si_ref[...] = jnp.zeros_like(si_ref)

    for q in range(QUADS):
        for b in range(nb):
            blk = u_ref[b, :, LANES * q:LANES * (q + 1)]
            for h in range(2):
                lhs_ref[q, h, pl.ds(2 * b + h, tt, stride=2 * nb), :] = blk

    for q in range(QUADS):
        lhs = jnp.concatenate([lhs_ref[q, 0], lhs_ref[q, 1]], axis=1).astype(BF16)
        res = _dot(lhs, wb_ref[q])
        for s in range(2):
            bur_ref[2 * q + s] = res[:, MXU_DIM * s:MXU_DIM * s + LANES]
            bui_ref[2 * q + s] = res[:, MXU_DIM * s + LANES:MXU_DIM * (s + 1)]

    for k0 in range(0, HALF_SLABS, SCAN_SLABS):
        slabs = range(k0, k0 + SCAN_SLABS)
        ar = [are_ref[:, LANES * k:LANES * (k + 1)] for k in slabs]
        ai = [aim_ref[:, LANES * k:LANES * (k + 1)] for k in slabs]
        sr = [sr_ref[:, LANES * k:LANES * (k + 1)] for k in slabs]
        si = [si_ref[:, LANES * k:LANES * (k + 1)] for k in slabs]
        for t in range(tt):
            rows = slice(SUBLANES * t, SUBLANES * (t + 1))
            for n, k in enumerate(slabs):
                nr = ar[n] * sr[n] - ai[n] * si[n] + bur_ref[k, rows, :]
                ni = ar[n] * si[n] + ai[n] * sr[n] + bui_ref[k, rows, :]
                sr[n], si[n] = nr, ni
                bur_ref[k, rows, :] = nr
                bui_ref[k, rows, :] = ni
        for n, k in enumerate(slabs):
            sr_ref[:, LANES * k:LANES * (k + 1)] = sr[n]
            si_ref[:, LANES * k:LANES * (k + 1)] = si[n]

    for o in range(OCTS):
        order = [(2 * (2 * o + ql) + s, h) for ql in range(2) for h in range(2) for s in range(2)]
        lre = jnp.concatenate([bur_ref[k, pl.ds(h, r, stride=2), :] for k, h in order], axis=1).astype(BF16)
        lim = jnp.concatenate([bui_ref[k, pl.ds(h, r, stride=2), :] for k, h in order], axis=1).astype(BF16)
        y = _dot(lre, wcr_ref[o]) + _dot(lim, wci_ref[o])
        for half in range(2):
            y_ref[half] = y[:, LANES * half:LANES * (half + 1)]
        for b in range(nb):
            for half in range(2):
                cols = slice(MXU_DIM * o + LANES * half, MXU_DIM * o + LANES * (half + 1))
                yb = y_ref[half, pl.ds(b, tt, stride=nb), :] + dsk_ref[:, cols] * u_ref[b, :, cols]
                z_ref[b, :, cols] = jax.nn.gelu(yb).astype(BF16)


def _ssm_scan_split(u, tables, d_skip, tt):
    w_b, a_re_t, a_im_t, w_cr, w_ci = tables
    nb, seq, d = u.shape
    assert 2 * nb == SUBLANES and seq % tt == 0

    def split_lanes(a):
        halves = jnp.transpose(a[0].reshape(QUADS, 2, 2 * LANES), (1, 0, 2)).reshape(2, HALF_SLABS * LANES)
        return jnp.tile(halves, (nb, 1))

    rows = pl.BlockSpec((nb, tt, d), lambda i: (0, i, 0))
    st_shape = jax.ShapeDtypeStruct((SUBLANES, HALF_SLABS * LANES), F32)
    r = nb * tt
    z, s_re, s_im = pl.pallas_call(
        functools.partial(_ssm_scan_split_kernel, nb=nb, tt=tt),
        grid=(seq // tt,),
        in_specs=[rows, _resident(w_b.shape), _resident(st_shape.shape), _resident(st_shape.shape),
                  _resident(w_cr.shape), _resident(w_ci.shape), _resident((1, d))],
        out_specs=(rows, pl.BlockSpec(st_shape.shape, lambda i: (0, 0)), pl.BlockSpec(st_shape.shape, lambda i: (0, 0))),
        out_shape=(jax.ShapeDtypeStruct((nb, seq, d), BF16), st_shape, st_shape),
        scratch_shapes=[pltpu.VMEM((QUADS, 2, 2 * r, LANES), F32),
                        pltpu.VMEM((HALF_SLABS, 2 * r, LANES), F32),
                        pltpu.VMEM((HALF_SLABS, 2 * r, LANES), F32),
                        pltpu.VMEM((2, r, LANES), F32)],
        compiler_params=_params(("arbitrary",)),
        name="ssm_scan_split",
    )(u, w_b, split_lanes(a_re_t), split_lanes(a_im_t), w_cr, w_ci, d_skip.reshape(1, d))

    def join(s):
        return jnp.transpose(s.reshape(nb, 2, QUADS, 2 * LANES), (0, 2, 1, 3)).reshape(nb, SSM_LANES)

    return z, join(s_re), join(s_im)


SSM_OUT_COLS = 512


def _ssm_out_kernel(z_ref, x_ref, w_ref, o_ref):
    z = z_ref[...]
    for c in range(D_MODEL // SSM_OUT_COLS):
        cols = slice(c * SSM_OUT_COLS, (c + 1) * SSM_OUT_COLS)
        gate_cols = slice(D_MODEL + c * SSM_OUT_COLS, D_MODEL + (c + 1) * SSM_OUT_COLS)
        val = _dot(z, w_ref[:, cols])
        gate = _dot(z, w_ref[:, gate_cols])
        o_ref[:, cols] = x_ref[:, cols] + val * jax.nn.sigmoid(gate)


def _ssm_out(z, x, w, tm):
    m, d = x.shape
    rows = pl.BlockSpec((tm, d), lambda i: (i, 0))
    return pl.pallas_call(
        _ssm_out_kernel,
        grid=(m // tm,),
        in_specs=[rows, rows, _resident(w.shape)],
        out_specs=rows,
        out_shape=jax.ShapeDtypeStruct((m, d), F32),
        compiler_params=_params(("parallel",)),
        name="ssm_out_glu",
    )(z, x, w)


def _ffn_kernel(x_ref, g_ref, wg_ref, wu_ref, wd_ref, gf_ref, o_ref, hn_ref, *, final_norm):
    h = pl.program_id(1)

    @pl.when(h == 0)
    def _():
        x = x_ref[...]
        hn_ref[...] = _rms(x, g_ref[...]).astype(BF16)
        o_ref[...] = x

    hn = hn_ref[...]
    act = (jax.nn.silu(_dot(hn, wg_ref[...])) * _dot(hn, wu_ref[...])).astype(BF16)
    o_ref[...] += _dot(act, wd_ref[...])

    if final_norm:
        @pl.when(h == pl.num_programs(1) - 1)
        def _():
            o_ref[...] = _rms(o_ref[...], gf_ref[...])


def _ffn(x, g, w_gate_up, w_down, g_final, *, layer, tm, th, final_norm):
    m, d = x.shape
    n_h = FFN_HIDDEN // th
    rows = pl.BlockSpec((tm, d), lambda i, h: (i, 0))
    return pl.pallas_call(
        functools.partial(_ffn_kernel, final_norm=final_norm),
        grid=(m // tm, n_h),
        in_specs=[rows,
                  _resident((1, d)),
                  pl.BlockSpec((None, d, th), lambda i, h: (layer, 0, h)),
                  pl.BlockSpec((None, d, th), lambda i, h: (layer, 0, n_h + h)),
                  pl.BlockSpec((None, th, d), lambda i, h: (layer, h, 0)),
                  _resident((1, d))],
        out_specs=rows,
        out_shape=jax.ShapeDtypeStruct((m, d), F32),
        scratch_shapes=[pltpu.VMEM((tm, d), BF16)],
        compiler_params=_params(("parallel", "arbitrary")),
        name="ffn_swiglu",
    )(x, g.reshape(1, d), w_gate_up, w_gate_up, w_down, g_final.reshape(1, d))


def _gmlp_gate_inputs(x, g, win_ref, v_gain):
    hn = _rms(x, g).astype(BF16)
    v = jax.nn.gelu(_dot(hn, win_ref[:, GMLP_WIDTH:]))
    vc = v - jnp.mean(v, axis=-1, keepdims=True)
    vn = (vc * lax.rsqrt(jnp.mean(vc * vc, axis=-1, keepdims=True) + EPS)) * v_gain
    u = jax.nn.gelu(_dot(hn, win_ref[:, :GMLP_WIDTH]))
    return u, vn


def _gmlp_prompt_kernel(x_ref, g_ref, win_ref, vg_ref, ws_ref, bs_ref, wout_ref, o_ref, gate_ref):
    x = x_ref[...]
    u, vn = _gmlp_gate_inputs(x, g_ref[...], win_ref, vg_ref[...])
    vb = vn.astype(BF16)
    q_idx = lax.broadcasted_iota(jnp.int32, (CHUNK, CHUNK), 0)
    k_idx = lax.broadcasted_iota(jnp.int32, (CHUNK, CHUNK), 1)
    causal = k_idx <= q_idx
    for h in range(GMLP_HEADS):
        cols = slice(h * GMLP_HEAD_DIM, (h + 1) * GMLP_HEAD_DIM)
        ws = jnp.where(causal, ws_ref[h], 0.0).astype(BF16)
        bias = bs_ref[:, h:h + 1]
        for c in range(x.shape[0] // CHUNK):
            rows = slice(c * CHUNK, (c + 1) * CHUNK)
            s = _dot(ws, vb[rows, cols]) + bias
            gate_ref[rows, cols] = (u[rows, cols] * s).astype(BF16)
    o_ref[...] = x + _dot(gate_ref[...], wout_ref[...])


def _gmlp_sample_kernel(x_ref, g_ref, win_ref, vg_ref, wq_ref, bq_ref, wout_ref, o_ref, v_ref, *, steps):
    x = x_ref[...]
    u, vn = _gmlp_gate_inputs(x, g_ref[...], win_ref, vg_ref[...])
    v_ref[...] = vn
    nseq = x.shape[0] // steps
    gates = []
    for q in range(steps):
        s = bq_ref[q:q + 1, :]
        for k in range(q + 1):
            s = s + wq_ref[q * steps + k:q * steps + k + 1, :] * vn[k * nseq:(k + 1) * nseq, :]
        gates.append(u[q * nseq:(q + 1) * nseq, :] * s)
    gate = jnp.concatenate(gates, axis=0).astype(BF16)
    o_ref[...] = x + _dot(gate, wout_ref[...])


def _gmlp_prompt(x, g, w_in, v_gain, w_spatial, b_spatial_t, w_out, tm):
    m, d = x.shape
    rows = pl.BlockSpec((tm, d), lambda i: (i, 0))
    return pl.pallas_call(
        _gmlp_prompt_kernel,
        grid=(m // tm,),
        in_specs=[rows, _resident((1, d)), _resident(w_in.shape), _resident((1, GMLP_WIDTH)),
                  _resident(w_spatial.shape), _resident(b_spatial_t.shape), _resident(w_out.shape)],
        out_specs=rows,
        out_shape=jax.ShapeDtypeStruct((m, d), F32),
        scratch_shapes=[pltpu.VMEM((tm, GMLP_WIDTH), BF16)],
        compiler_params=_params(("parallel",)),
        name="gmlp_prompt",
    )(x, g.reshape(1, d), w_in, v_gain.reshape(1, GMLP_WIDTH), w_spatial, b_spatial_t, w_out)


def _gmlp_sample(x, g, w_in, v_gain, w_q, b_q, w_out, tm, steps):
    m, d = x.shape
    rows = pl.BlockSpec((tm, d), lambda i: (i, 0))
    return pl.pallas_call(
        functools.partial(_gmlp_sample_kernel, steps=steps),
        grid=(m // tm,),
        in_specs=[rows, _resident((1, d)), _resident(w_in.shape), _resident((1, GMLP_WIDTH)),
                  _resident(w_q.shape), _resident(b_q.shape), _resident(w_out.shape)],
        out_specs=(rows, pl.BlockSpec((tm, GMLP_WIDTH), lambda i: (i, 0))),
        out_shape=(jax.ShapeDtypeStruct((m, d), F32), jax.ShapeDtypeStruct((m, GMLP_WIDTH), F32)),
        compiler_params=_params(("parallel",)),
        name="gmlp_sample",
    )(x, g.reshape(1, d), w_in, v_gain.reshape(1, GMLP_WIDTH), w_q, b_q, w_out)


PROMPT_SCAN_STEPS = 64
SAMPLE_TILE_SEQS = 32


def kernel(x_prompt, x_sample, state_ssm_re, state_ssm_im, norm_mix, norm_ffn, norm_final, ssm_w_in, ssm_lambda_re, ssm_lambda_im, ssm_log_dt, ssm_b_re, ssm_b_im, ssm_c_re, ssm_c_im, ssm_d, ssm_w_out, gmlp_w_in, gmlp_v_gain, gmlp_w_spatial, gmlp_b_spatial, gmlp_w_out, ffn_w_gate_up, ffn_w_down):
    bsz, seq, d = x_prompt.shape
    dbsz, dseq, _ = x_sample.shape

    w_in0 = ssm_w_in[0].astype(BF16)
    w_out0 = ssm_w_out[0].astype(BF16)
    g_w_in = gmlp_w_in[0].astype(BF16)
    g_w_out = gmlp_w_out[0].astype(BF16)
    w_gu = ffn_w_gate_up.astype(BF16)
    w_dn = ffn_w_down.astype(BF16)

    a_re, a_im, bb_re, bb_im = _discretize(ssm_lambda_re[0], ssm_lambda_im[0], ssm_log_dt[0],
                                           ssm_b_re[0], ssm_b_im[0])
    tables = _ssm_tables(a_re, a_im, bb_re, bb_im, ssm_c_re[0], ssm_c_im[0])

    def layer0(x_rows, scan_fn, tm, ffn_tm):
        u = _norm_matmul(x_rows, norm_mix[0], w_in0, tm)
        z, s_re, s_im = scan_fn(u)
        x1 = _ssm_out(z, x_rows, w_out0, tm)
        x2 = _ffn(x1, norm_ffn[0], w_gu, w_dn, norm_final, layer=0, tm=ffn_tm, th=512, final_norm=False)
        return x2, s_re, s_im

    def ffn1(x_rows, ffn_tm):
        return _ffn(x_rows, norm_ffn[1], w_gu, w_dn, norm_final, layer=1, tm=ffn_tm, th=512, final_norm=True)

    xp = x_prompt.reshape(bsz * seq, d)

    def prompt_scan(u):
        z, s_re, s_im = _ssm_scan_split(u.reshape(bsz, seq, d), tables, ssm_d[0], PROMPT_SCAN_STEPS)
        return z.reshape(bsz * seq, d), s_re, s_im

    xp2, p_re, p_im = layer0(xp, prompt_scan, 512, 1024)
    xp3 = _gmlp_prompt(xp2, norm_mix[1], g_w_in, gmlp_v_gain[0], gmlp_w_spatial[0],
                       jnp.transpose(gmlp_b_spatial[0]), g_w_out, 256)
    y_prompt = ffn1(xp3, 1024).reshape(bsz, seq, d)

    n_bt = dbsz // SAMPLE_TILE_SEQS
    tile_rows = SAMPLE_TILE_SEQS * dseq

    def to_rows(a):
        return jnp.transpose(a.reshape(n_bt, SAMPLE_TILE_SEQS, dseq, d), (0, 2, 1, 3)).reshape(dbsz * dseq, d)

    def from_rows(a):
        return jnp.transpose(a.reshape(n_bt, dseq, SAMPLE_TILE_SEQS, d), (0, 2, 1, 3)).reshape(dbsz, dseq, d)

    xs = to_rows(x_sample)
    h0_re = state_ssm_re[0].reshape(dbsz, SSM_LANES)
    h0_im = state_ssm_im[0].reshape(dbsz, SSM_LANES)
    sample_scan = functools.partial(_ssm_scan, tables=tables, d_skip=ssm_d[0], h0_re=h0_re, h0_im=h0_im,
                                    nb=SAMPLE_TILE_SEQS, tt=dseq, n_bt=n_bt, n_tt=1)
    xs2, s_re, s_im = layer0(xs, sample_scan, tile_rows, dbsz * dseq)
    w_q = jnp.repeat(gmlp_w_spatial[0][:, :dseq, :dseq].reshape(GMLP_HEADS, dseq * dseq).T, GMLP_HEAD_DIM, axis=1)
    b_q = jnp.repeat(gmlp_b_spatial[0][:, :dseq].T, GMLP_HEAD_DIM, axis=1)
    xs3, v_rows = _gmlp_sample(xs2, norm_mix[1], g_w_in, gmlp_v_gain[0], w_q, b_q, g_w_out, tile_rows, dseq)
    y_sample = from_rows(ffn1(xs3, dbsz * dseq))

    state_shape = (1, -1, SSM_GROUPS, SSM_STATE)
    return (y_prompt, y_sample,
            p_re.reshape(state_shape), p_im.reshape(state_shape),
            s_re.reshape(state_shape), s_im.reshape(state_shape),
            from_rows(v_rows)[None])
```

```python
import functools

import jax
import jax.numpy as jnp
from jax import lax
from jax.experimental import pallas as pl
from jax.experimental.pallas import tpu as pltpu

D_MODEL = 2048
SSM_GROUPS = 128
SSM_GROUP = 16
SSM_STATE = 64
SSM_LANES = SSM_GROUPS * SSM_STATE
GMLP_WIDTH = D_MODEL
GMLP_HEADS = 16
GMLP_HEAD_DIM = GMLP_WIDTH // GMLP_HEADS
CHUNK = 128
FFN_HIDDEN = 5632
EPS = 1e-6

LANES = 128
SUBLANES = 8
MXU_DIM = 256
VMEM_LIMIT = 56 * 1024 * 1024

PAIRS = SSM_GROUPS // 2
QUAD_K = 4 * 2 * SSM_GROUP
QUADS = PAIRS // 4
HALF_SLABS = PAIRS // 2
OCTS = D_MODEL // MXU_DIM
OCT_K = SSM_LANES // OCTS

BF16 = jnp.bfloat16
F32 = jnp.float32


def _resident(shape):
    zeros = (0,) * len(shape)
    return pl.BlockSpec(shape, lambda *_: zeros, pipeline_mode=pl.Buffered(1))


def _params(semantics):
    return pltpu.CompilerParams(dimension_semantics=semantics, vmem_limit_bytes=VMEM_LIMIT)


def _rms(x, g):
    ms = jnp.mean(x * x, axis=-1, keepdims=True)
    return (x * lax.rsqrt(ms + EPS)) * g


def _dot(a, b):
    return jnp.dot(a, b, preferred_element_type=F32)


BF16_ROWS = 16


class _SideCast:
    def __init__(self, w, layer, steps):
        _, r, c = w.shape
        assert r % (steps * BF16_ROWS) == 0
        self.operand = w
        self.in_spec = pl.BlockSpec((None, r // steps, c), lambda i: (layer, i, 0))
        self.out_spec = pl.BlockSpec((r // steps, c), lambda i: (i, 0))
        self.out_shape = jax.ShapeDtypeStruct((r, c), BF16)


def _cast_blocks(src_refs, dst_refs):
    for src, dst in zip(src_refs, dst_refs):
        dst[...] = src[...].astype(BF16)


def _discretize_kernel(lr_ref, li_ref, ldt_ref, br_ref, bi_ref, are_ref, aim_ref, bbr_ref, bbi_ref):
    lr = lr_ref[...]
    li = li_ref[...]
    dt = jnp.exp(ldt_ref[...])
    mag = jnp.exp(lr * dt)
    a_re = mag * jnp.cos(li * dt)
    a_im = mag * jnp.sin(li * dt)
    den = lr * lr + li * li
    nr = a_re - 1.0
    ni = a_im
    q_re = (nr * lr + ni * li) / den
    q_im = (ni * lr - nr * li) / den
    br = br_ref[...]
    bi = bi_ref[...]
    are_ref[...] = a_re
    aim_ref[...] = a_im
    bbr_ref[...] = q_re * br - q_im * bi
    bbi_ref[...] = q_re * bi + q_im * br


def _discretize(lambda_re, lambda_im, log_dt, b_re, b_im):
    g, p, c = b_re.shape
    full = (g, c, p)
    flat = (g * c * p // LANES, LANES)
    args = [
        jnp.broadcast_to(lambda_re[:, None, :], full).reshape(flat),
        jnp.broadcast_to(lambda_im[:, None, :], full).reshape(flat),
        jnp.broadcast_to(log_dt[:, None, None], full).reshape(flat),
        jnp.transpose(b_re, (0, 2, 1)).reshape(flat),
        jnp.transpose(b_im, (0, 2, 1)).reshape(flat),
    ]
    out = jax.ShapeDtypeStruct(flat, F32)
    a_re, a_im, bb_re, bb_im = pl.pallas_call(
        _discretize_kernel, out_shape=(out, out, out, out), name="ssm_discretize")(*args)
    a_re = a_re.reshape(full)[:, 0, :]
    a_im = a_im.reshape(full)[:, 0, :]
    return a_re, a_im, bb_re.reshape(full), bb_im.reshape(full)


def _ssm_tables(a_re, a_im, bb_re, bb_im, c_re, c_im):
    c, p = SSM_GROUP, SSM_STATE
    row = lax.broadcasted_iota(jnp.int32, (2 * QUAD_K, 2 * MXU_DIM), 0)
    col = lax.broadcasted_iota(jnp.int32, (2 * QUAD_K, 2 * MXU_DIM), 1)
    row_h, row_g8 = row // QUAD_K, (row % QUAD_K) // c
    col_s, col_g2 = col // MXU_DIM, (col % LANES) // p
    hit = row_g8 == 4 * row_h + 2 * col_s + col_g2
    re = bb_re.reshape(QUADS, QUAD_K, p)
    im = bb_im.reshape(QUADS, QUAD_K, p)
    tiled = jnp.concatenate([re, re, im, im] * 2, axis=-1)
    w_b = jnp.where(hit[None], jnp.concatenate([tiled, tiled], axis=1), 0.0).astype(BF16)

    groups = MXU_DIM // c
    row_g = lax.broadcasted_iota(jnp.int32, (OCT_K, MXU_DIM), 0) // p
    col_g = lax.broadcasted_iota(jnp.int32, (OCT_K, MXU_DIM), 1) // c

    def out_blocks(cm):
        cm = jnp.transpose(cm.reshape(OCTS, groups, c, p), (0, 1, 3, 2)).reshape(OCTS, OCT_K, c)
        return jnp.where((row_g == col_g)[None], jnp.tile(cm, (1, 1, groups)), 0.0).astype(BF16)

    a_re_t = jnp.broadcast_to(a_re.reshape(1, SSM_LANES), (SUBLANES, SSM_LANES))
    a_im_t = jnp.broadcast_to(a_im.reshape(1, SSM_LANES), (SUBLANES, SSM_LANES))
    return (w_b, a_re_t, a_im_t, out_blocks(c_re), out_blocks(-c_im))


def _norm_matmul_kernel(x_ref, g_ref, w_ref, *rest):
    n_cast = (len(rest) - 2) // 2
    o_ref, wb_ref = rest[n_cast], rest[-1]
    _cast_blocks(rest[:n_cast], rest[n_cast + 1:-1])

    @pl.when(pl.program_id(0) == 0)
    def _():
        wb_ref[...] = w_ref[...].astype(BF16)

    hn = _rms(x_ref[...], g_ref[...]).astype(BF16)
    o_ref[...] = _dot(hn, wb_ref[...])


def _norm_matmul(x, g, w, layer, tm, casts=()):
    m, d = x.shape
    n = w.shape[2]
    casts = [_SideCast(cw, cl, m // tm) for cw, cl in casts]
    return pl.pallas_call(
        _norm_matmul_kernel,
        grid=(m // tm,),
        in_specs=[pl.BlockSpec((tm, d), lambda i: (i, 0)), _resident((1, d)),
                  pl.BlockSpec((None, d, n), lambda i: (layer, 0, 0), pipeline_mode=pl.Buffered(1))]
                 + [c.in_spec for c in casts],
        out_specs=[pl.BlockSpec((tm, n), lambda i: (i, 0))] + [c.out_spec for c in casts],
        out_shape=[jax.ShapeDtypeStruct((m, n), F32)] + [c.out_shape for c in casts],
        scratch_shapes=[pltpu.VMEM((d, n), BF16)],
        compiler_params=_params(("arbitrary",)),
        name="ssm_norm_in_proj",
    )(x, g.reshape(1, d), w, *[c.operand for c in casts])


SCAN_COLS = 1024


def _ssm_scan_kernel(u_ref, wb_ref, are_ref, aim_ref, wcr_ref, wci_ref, dsk_ref, h0r_ref, h0i_ref,
                     z_ref, sr_ref, si_ref, bur_ref, bui_ref, *, nb, tt):
    @pl.when(pl.program_id(1) == 0)
    def _():
        sr_ref[...] = h0r_ref[...]
        si_ref[...] = h0i_ref[...]

    u = u_ref[...]
    ub = u.astype(BF16)
    for q in range(QUADS):
        lhs = ub[:, QUAD_K * q:QUAD_K * (q + 1)]
        for h in range(2):
            res = _dot(lhs, wb_ref[q, QUAD_K * h:QUAD_K * (h + 1), :])
            for s in range(2):
                j = 4 * q + 2 * h + s
                bur_ref[:, LANES * j:LANES * (j + 1)] = res[:, MXU_DIM * s:MXU_DIM * s + LANES]
                bui_ref[:, LANES * j:LANES * (j + 1)] = res[:, MXU_DIM * s + LANES:MXU_DIM * (s + 1)]

    for cb in range(SSM_LANES // SCAN_COLS):
        cols = slice(cb * SCAN_COLS, (cb + 1) * SCAN_COLS)
        ar = are_ref[:, cols]
        ai = aim_ref[:, cols]
        for sg in range(nb // SUBLANES):
            seqs = slice(sg * SUBLANES, (sg + 1) * SUBLANES)
            sr = sr_ref[seqs, cols]
            si = si_ref[seqs, cols]
            for t in range(tt):
                rows = slice(t * nb + sg * SUBLANES, t * nb + (sg + 1) * SUBLANES)
                nr = ar * sr - ai * si + bur_ref[rows, cols]
                ni = ar * si + ai * sr + bui_ref[rows, cols]
                sr, si = nr, ni
                bur_ref[rows, cols] = sr
                bui_ref[rows, cols] = si
            sr_ref[seqs, cols] = sr
            si_ref[seqs, cols] = si

    for o in range(OCTS):
        kk = slice(o * OCT_K, (o + 1) * OCT_K)
        nn = slice(o * MXU_DIM, (o + 1) * MXU_DIM)
        y = _dot(bur_ref[:, kk].astype(BF16), wcr_ref[o]) + _dot(bui_ref[:, kk].astype(BF16), wci_ref[o])
        y = y + dsk_ref[:, nn] * u[:, nn]
        z_ref[:, nn] = jax.nn.gelu(y).astype(BF16)


def _ssm_scan(u, tables, d_skip, h0_re, h0_im, *, nb, tt, n_bt, n_tt):
    w_b, a_re_t, a_im_t, w_cr, w_ci = tables
    m, d = u.shape
    r = nb * tt
    assert m == r * n_bt * n_tt and nb % SUBLANES == 0
    rows = pl.BlockSpec((r, d), lambda b, t: (b * n_tt + t, 0))
    state = pl.BlockSpec((nb, SSM_LANES), lambda b, t: (b, 0))
    st_shape = jax.ShapeDtypeStruct((nb * n_bt, SSM_LANES), F32)
    return pl.pallas_call(
        functools.partial(_ssm_scan_kernel, nb=nb, tt=tt),
        grid=(n_bt, n_tt),
        in_specs=[rows, _resident(w_b.shape), _resident(a_re_t.shape), _resident(a_im_t.shape),
                  _resident(w_cr.shape), _resident(w_ci.shape), _resident((1, d)), state, state],
        out_specs=(rows, state, state),
        out_shape=(jax.ShapeDtypeStruct((m, d), BF16), st_shape, st_shape),
        scratch_shapes=[pltpu.VMEM((r, SSM_LANES), F32), pltpu.VMEM((r, SSM_LANES), F32)],
        compiler_params=_params(("parallel", "arbitrary")),
        name="ssm_scan",
    )(u, w_b, a_re_t, a_im_t, w_cr, w_ci, d_skip.reshape(1, d), h0_re, h0_im)


SCAN_SLABS = 8


def _ssm_scan_split_kernel(u_ref, wb_ref, are_ref, aim_ref, wcr_ref, wci_ref, dsk_ref, cast_src_ref,
                           z_ref, sr_ref, si_ref, cast_dst_ref, lhs_ref, bur_ref, bui_ref, y_ref, *, nb, tt):
    r = nb * tt
    i = pl.program_id(0)
    _cast_blocks([cast_src_ref], [cast_dst_ref])

    @pl.when(i == 0)
    def _():
        lhs_ref[...] = jnp.zeros_like(lhs_ref)
        sr_ref[...] = jnp.zeros_like(sr_ref)
        si_ref[...] = jnp.zeros_like(si_ref)

    for q in range(QUADS):
        for b in range(nb):
            blk = u_ref[b, :, LANES * q:LANES * (q + 1)]
            for h in range(2):
                lhs_ref[q, h, pl.ds(2 * b + h, tt, stride=2 * nb), :] = blk

    for q in range(QUADS):
        lhs = jnp.concatenate([lhs_ref[q, 0], lhs_ref[q, 1]], axis=1).astype(BF16)
        res = _dot(lhs, wb_ref[q])
        for s in range(2):
            bur_ref[2 * q + s] = res[:, MXU_DIM * s:MXU_DIM * s + LANES]
            bui_ref[2 * q + s] = res[:, MXU_DIM * s + LANES:MXU_DIM * (s + 1)]

    for k0 in range(0, HALF_SLABS, SCAN_SLABS):
        slabs = range(k0, k0 + SCAN_SLABS)
        ar = [are_ref[:, LANES * k:LANES * (k + 1)] for k in slabs]
        ai = [aim_ref[:, LANES * k:LANES * (k + 1)] for k in slabs]
        sr = [sr_ref[:, LANES * k:LANES * (k + 1)] for k in slabs]
        si = [si_ref[:, LANES * k:LANES * (k + 1)] for k in slabs]
        for t in range(tt):
            rows = slice(SUBLANES * t, SUBLANES * (t + 1))
            for n, k in enumerate(slabs):
                nr = ar[n] * sr[n] - ai[n] * si[n] + bur_ref[k, rows, :]
                ni = ar[n] * si[n] + ai[n] * sr[n] + bui_ref[k, rows, :]
                sr[n], si[n] = nr, ni
                bur_ref[k, rows, :] = nr
                bui_ref[k, rows, :] = ni
        for n, k in enumerate(slabs):
            sr_ref[:, LANES * k:LANES * (k + 1)] = sr[n]
            si_ref[:, LANES * k:LANES * (k + 1)] = si[n]

    for o in range(OCTS):
        order = [(2 * (2 * o + ql) + s, h) for ql in range(2) for h in range(2) for s in range(2)]
        lre = jnp.concatenate([bur_ref[k, pl.ds(h, r, stride=2), :] for k, h in order], axis=1).astype(BF16)
        lim = jnp.concatenate([bui_ref[k, pl.ds(h, r, stride=2), :] for k, h in order], axis=1).astype(BF16)
        y = _dot(lre, wcr_ref[o]) + _dot(lim, wci_ref[o])
        for half in range(2):
            y_ref[half] = y[:, LANES * half:LANES * (half + 1)]
        for b in range(nb):
            for half in range(2):
                cols = slice(MXU_DIM * o + LANES * half, MXU_DIM * o + LANES * (half + 1))
                yb = y_ref[half, pl.ds(b, tt, stride=nb), :] + dsk_ref[:, cols] * u_ref[b, :, cols]
                z_ref[b, :, cols] = jax.nn.gelu(yb).astype(BF16)


def _ssm_scan_split(u, tables, d_skip, tt, cast_w, cast_layer):
    w_b, a_re_t, a_im_t, w_cr, w_ci = tables
    nb, seq, d = u.shape
    assert 2 * nb == SUBLANES and seq % tt == 0
    cast = _SideCast(cast_w, cast_layer, seq // tt)

    def split_lanes(a):
        halves = jnp.transpose(a[0].reshape(QUADS, 2, 2 * LANES), (1, 0, 2)).reshape(2, HALF_SLABS * LANES)
        return jnp.tile(halves, (nb, 1))

    rows = pl.BlockSpec((nb, tt, d), lambda i: (0, i, 0))
    st_shape = jax.ShapeDtypeStruct((SUBLANES, HALF_SLABS * LANES), F32)
    r = nb * tt
    z, s_re, s_im, w_cast = pl.pallas_call(
        functools.partial(_ssm_scan_split_kernel, nb=nb, tt=tt),
        grid=(seq // tt,),
        in_specs=[rows, _resident(w_b.shape), _resident(st_shape.shape), _resident(st_shape.shape),
                  _resident(w_cr.shape), _resident(w_ci.shape), _resident((1, d)), cast.in_spec],
        out_specs=(rows, pl.BlockSpec(st_shape.shape, lambda i: (0, 0)), pl.BlockSpec(st_shape.shape, lambda i: (0, 0)),
                   cast.out_spec),
        out_shape=(jax.ShapeDtypeStruct((nb, seq, d), BF16), st_shape, st_shape, cast.out_shape),
        scratch_shapes=[pltpu.VMEM((QUADS, 2, 2 * r, LANES), F32),
                        pltpu.VMEM((HALF_SLABS, 2 * r, LANES), F32),
                        pltpu.VMEM((HALF_SLABS, 2 * r, LANES), F32),
                        pltpu.VMEM((2, r, LANES), F32)],
        compiler_params=_params(("arbitrary",)),
        name="ssm_scan_split",
    )(u, w_b, split_lanes(a_re_t), split_lanes(a_im_t), w_cr, w_ci, d_skip.reshape(1, d), cast.operand)

    def join(s):
        return jnp.transpose(s.reshape(nb, 2, QUADS, 2 * LANES), (0, 2, 1, 3)).reshape(nb, SSM_LANES)

    return z, join(s_re), join(s_im), w_cast


SSM_OUT_COLS = 512


def _ssm_out_kernel(z_ref, x_ref, w_ref, *rest):
    o_ref = rest[len(rest) // 2]
    _cast_blocks(rest[:len(rest) // 2], rest[len(rest) // 2 + 1:])
    z = z_ref[...]
    for c in range(D_MODEL // SSM_OUT_COLS):
        cols = slice(c * SSM_OUT_COLS, (c + 1) * SSM_OUT_COLS)
        gate_cols = slice(D_MODEL + c * SSM_OUT_COLS, D_MODEL + (c + 1) * SSM_OUT_COLS)
        val = _dot(z, w_ref[:, cols])
        gate = _dot(z, w_ref[:, gate_cols])
        o_ref[:, cols] = x_ref[:, cols] + val * jax.nn.sigmoid(gate)


def _ssm_out(z, x, w, tm, casts=()):
    m, d = x.shape
    rows = pl.BlockSpec((tm, d), lambda i: (i, 0))
    casts = [_SideCast(cw, layer, m // tm) for cw, layer in casts]
    return pl.pallas_call(
        _ssm_out_kernel,
        grid=(m // tm,),
        in_specs=[rows, rows, _resident(w.shape)] + [c.in_spec for c in casts],
        out_specs=[rows] + [c.out_spec for c in casts],
        out_shape=[jax.ShapeDtypeStruct((m, d), F32)] + [c.out_shape for c in casts],
        compiler_params=_params(("parallel",)),
        name="ssm_out_glu",
    )(z, x, w, *[c.operand for c in casts])


def _ffn_kernel(x_ref, g_ref, wg_ref, wu_ref, wd_ref, gf_ref, o_ref, hn_ref, *, final_norm):
    h = pl.program_id(1)

    @pl.when(h == 0)
    def _():
        x = x_ref[...]
        hn_ref[...] = _rms(x, g_ref[...]).astype(BF16)
        o_ref[...] = x

    hn = hn_ref[...]
    act = (jax.nn.silu(_dot(hn, wg_ref[...])) * _dot(hn, wu_ref[...])).astype(BF16)
    o_ref[...] += _dot(act, wd_ref[...])

    if final_norm:
        @pl.when(h == pl.num_programs(1) - 1)
        def _():
            o_ref[...] = _rms(o_ref[...], gf_ref[...])


def _ffn(x, g, w_gate_up, w_down, g_final, *, tm, th, final_norm):
    m, d = x.shape
    n_h = FFN_HIDDEN // th
    rows = pl.BlockSpec((tm, d), lambda i, h: (i, 0))
    return pl.pallas_call(
        functools.partial(_ffn_kernel, final_norm=final_norm),
        grid=(m // tm, n_h),
        in_specs=[rows,
                  _resident((1, d)),
                  pl.BlockSpec((d, th), lambda i, h: (0, h)),
                  pl.BlockSpec((d, th), lambda i, h: (0, n_h + h)),
                  pl.BlockSpec((th, d), lambda i, h: (h, 0)),
                  _resident((1, d))],
        out_specs=rows,
        out_shape=jax.ShapeDtypeStruct((m, d), F32),
        scratch_shapes=[pltpu.VMEM((tm, d), BF16)],
        compiler_params=_params(("parallel", "arbitrary")),
        name="ffn_swiglu",
    )(x, g.reshape(1, d), w_gate_up, w_gate_up, w_down, g_final.reshape(1, d))


def _gmlp_gate_inputs(x, g, win_ref, v_gain):
    hn = _rms(x, g).astype(BF16)
    v = jax.nn.gelu(_dot(hn, win_ref[:, GMLP_WIDTH:]))
    vc = v - jnp.mean(v, axis=-1, keepdims=True)
    vn = (vc * lax.rsqrt(jnp.mean(vc * vc, axis=-1, keepdims=True) + EPS)) * v_gain
    u = jax.nn.gelu(_dot(hn, win_ref[:, :GMLP_WIDTH]))
    return u, vn


def _gmlp_prompt_kernel(x_ref, g_ref, win_ref, vg_ref, ws_ref, bs_ref, wout_ref, *rest):
    n_cast = (len(rest) - 2) // 2
    o_ref, gate_ref = rest[n_cast], rest[-1]
    _cast_blocks(rest[:n_cast], rest[n_cast + 1:-1])
    x = x_ref[...]
    u, vn = _gmlp_gate_inputs(x, g_ref[...], win_ref, vg_ref[...])
    vb = vn.astype(BF16)
    q_idx = lax.broadcasted_iota(jnp.int32, (CHUNK, CHUNK), 0)
    k_idx = lax.broadcasted_iota(jnp.int32, (CHUNK, CHUNK), 1)
    causal = k_idx <= q_idx
    for h in range(GMLP_HEADS):
        cols = slice(h * GMLP_HEAD_DIM, (h + 1) * GMLP_HEAD_DIM)
        ws = jnp.where(causal, ws_ref[h], 0.0).astype(BF16)
        bias = bs_ref[:, h:h + 1]
        for c in range(0, x.shape[0] // CHUNK, 2):
            lo = slice(c * CHUNK, (c + 1) * CHUNK)
            hi = slice((c + 1) * CHUNK, (c + 2) * CHUNK)
            s = _dot(ws, jnp.concatenate([vb[lo, cols], vb[hi, cols]], axis=1)) + bias
            gate_ref[lo, cols] = (u[lo, cols] * s[:, :GMLP_HEAD_DIM]).astype(BF16)
            gate_ref[hi, cols] = (u[hi, cols] * s[:, GMLP_HEAD_DIM:]).astype(BF16)
    o_ref[...] = x + _dot(gate_ref[...], wout_ref[...])


def _gmlp_sample_kernel(x_ref, g_ref, win_ref, vg_ref, wq_ref, bq_ref, wout_ref, o_ref, v_ref, *, steps):
    x = x_ref[...]
    u, vn = _gmlp_gate_inputs(x, g_ref[...], win_ref, vg_ref[...])
    v_ref[...] = vn
    nseq = x.shape[0] // steps
    gates = []
    for q in range(steps):
        s = bq_ref[q:q + 1, :]
        for k in range(q + 1):
            s = s + wq_ref[q * steps + k:q * steps + k + 1, :] * vn[k * nseq:(k + 1) * nseq, :]
        gates.append(u[q * nseq:(q + 1) * nseq, :] * s)
    gate = jnp.concatenate(gates, axis=0).astype(BF16)
    o_ref[...] = x + _dot(gate, wout_ref[...])


def _gmlp_prompt(x, g, w_in, v_gain, w_spatial, b_spatial_t, w_out, tm, casts=()):
    m, d = x.shape
    rows = pl.BlockSpec((tm, d), lambda i: (i, 0))
    casts = [_SideCast(cw, layer, m // tm) for cw, layer in casts]
    return pl.pallas_call(
        _gmlp_prompt_kernel,
        grid=(m // tm,),
        in_specs=[rows, _resident((1, d)), _resident(w_in.shape), _resident((1, GMLP_WIDTH)),
                  _resident(w_spatial.shape), _resident(b_spatial_t.shape), _resident(w_out.shape)]
                 + [c.in_spec for c in casts],
        out_specs=[rows] + [c.out_spec for c in casts],
        out_shape=[jax.ShapeDtypeStruct((m, d), F32)] + [c.out_shape for c in casts],
        scratch_shapes=[pltpu.VMEM((tm, GMLP_WIDTH), BF16)],
        compiler_params=_params(("parallel",)),
        name="gmlp_prompt",
    )(x, g.reshape(1, d), w_in, v_gain.reshape(1, GMLP_WIDTH), w_spatial, b_spatial_t, w_out,
      *[c.operand for c in casts])


def _gmlp_sample(x, g, w_in, v_gain, w_q, b_q, w_out, tm, steps):
    m, d = x.shape
    rows = pl.BlockSpec((tm, d), lambda i: (i, 0))
    return pl.pallas_call(
        functools.partial(_gmlp_sample_kernel, steps=steps),
        grid=(m // tm,),
        in_specs=[rows, _resident((1, d)), _resident(w_in.shape), _resident((1, GMLP_WIDTH)),
                  _resident(w_q.shape), _resident(b_q.shape), _resident(w_out.shape)],
        out_specs=(rows, pl.BlockSpec((tm, GMLP_WIDTH), lambda i: (i, 0))),
        out_shape=(jax.ShapeDtypeStruct((m, d), F32), jax.ShapeDtypeStruct((m, GMLP_WIDTH), F32)),
        compiler_params=_params(("parallel",)),
        name="gmlp_sample",
    )(x, g.reshape(1, d), w_in, v_gain.reshape(1, GMLP_WIDTH), w_q, b_q, w_out)


PROMPT_SCAN_STEPS = 64
SAMPLE_TILE_SEQS = 32


def kernel(x_prompt, x_sample, state_ssm_re, state_ssm_im, norm_mix, norm_ffn, norm_final, ssm_w_in, ssm_lambda_re, ssm_lambda_im, ssm_log_dt, ssm_b_re, ssm_b_im, ssm_c_re, ssm_c_im, ssm_d, ssm_w_out, gmlp_w_in, gmlp_v_gain, gmlp_w_spatial, gmlp_b_spatial, gmlp_w_out, ffn_w_gate_up, ffn_w_down):
    bsz, seq, d = x_prompt.shape
    dbsz, dseq, _ = x_sample.shape

    a_re, a_im, bb_re, bb_im = _discretize(ssm_lambda_re[0], ssm_lambda_im[0], ssm_log_dt[0],
                                           ssm_b_re[0], ssm_b_im[0])
    tables = _ssm_tables(a_re, a_im, bb_re, bb_im, ssm_c_re[0], ssm_c_im[0])

    def ffn(x_rows, layer, w_gu, w_dn, ffn_tm):
        return _ffn(x_rows, norm_ffn[layer], w_gu, w_dn, norm_final, tm=ffn_tm, th=512, final_norm=layer == 1)

    xp = x_prompt.reshape(bsz * seq, d)
    u, w_out0 = _norm_matmul(xp, norm_mix[0], ssm_w_in, 0, 512, casts=[(ssm_w_out, 0)])
    z, p_re, p_im, w_gu0 = _ssm_scan_split(u.reshape(bsz, seq, d), tables, ssm_d[0], PROMPT_SCAN_STEPS,
                                           ffn_w_gate_up, 0)
    xp1, w_dn0, g_w_out = _ssm_out(z.reshape(bsz * seq, d), xp, w_out0, 512,
                                   casts=[(ffn_w_down, 0), (gmlp_w_out, 0)])
    g_w_in = gmlp_w_in[0].astype(BF16)
    xp2 = ffn(xp1, 0, w_gu0, w_dn0, 1024)
    xp3, w_gu1, w_dn1 = _gmlp_prompt(xp2, norm_mix[1], g_w_in, gmlp_v_gain[0], gmlp_w_spatial[0],
                                     jnp.transpose(gmlp_b_spatial[0]), g_w_out, 256,
                                     casts=[(ffn_w_gate_up, 1), (ffn_w_down, 1)])
    y_prompt = ffn(xp3, 1, w_gu1, w_dn1, 1024).reshape(bsz, seq, d)

    n_bt = dbsz // SAMPLE_TILE_SEQS
    tile_rows = SAMPLE_TILE_SEQS * dseq

    def to_rows(a):
        return jnp.transpose(a.reshape(n_bt, SAMPLE_TILE_SEQS, dseq, d), (0, 2, 1, 3)).reshape(dbsz * dseq, d)

    def from_rows(a):
        return jnp.transpose(a.reshape(n_bt, dseq, SAMPLE_TILE_SEQS, d), (0, 2, 1, 3)).reshape(dbsz, dseq, d)

    xs = to_rows(x_sample)
    h0_re = state_ssm_re[0].reshape(dbsz, SSM_LANES)
    h0_im = state_ssm_im[0].reshape(dbsz, SSM_LANES)
    us, = _norm_matmul(xs, norm_mix[0], ssm_w_in, 0, dbsz * dseq)
    zs, s_re, s_im = _ssm_scan(us, tables, ssm_d[0], h0_re, h0_im, nb=SAMPLE_TILE_SEQS, tt=dseq, n_bt=n_bt, n_tt=1)
    xs1, = _ssm_out(zs, xs, w_out0, tile_rows)
    xs2 = ffn(xs1, 0, w_gu0, w_dn0, dbsz * dseq)
    w_q = jnp.repeat(gmlp_w_spatial[0][:, :dseq, :dseq].reshape(GMLP_HEADS, dseq * dseq).T, GMLP_HEAD_DIM, axis=1)
    b_q = jnp.repeat(gmlp_b_spatial[0][:, :dseq].T, GMLP_HEAD_DIM, axis=1)
    xs3, v_rows = _gmlp_sample(xs2, norm_mix[1], g_w_in, gmlp_v_gain[0], w_q, b_q, g_w_out, tile_rows, dseq)
    y_sample = from_rows(ffn(xs3, 1, w_gu1, w_dn1, dbsz * dseq))

    state_shape = (1, -1, SSM_GROUPS, SSM_STATE)
    return (y_prompt, y_sample,
            p_re.reshape(state_shape), p_im.reshape(state_shape),
            s_re.reshape(state_shape), s_im.reshape(state_shape),
            from_rows(v_rows)[None])
```

```python
import functools

import jax
import jax.numpy as jnp
from jax import lax
from jax.experimental import pallas as pl
from jax.experimental.pallas import tpu as pltpu

D_MODEL = 2048
SSM_GROUPS = 128
SSM_GROUP = 16
SSM_STATE = 64
SSM_LANES = SSM_GROUPS * SSM_STATE
GMLP_WIDTH = D_MODEL
GMLP_HEADS = 16
GMLP_HEAD_DIM = GMLP_WIDTH // GMLP_HEADS
CHUNK = 128
FFN_HIDDEN = 5632
EPS = 1e-6

LANES = 128
SUBLANES = 8
MXU_DIM = 256
VMEM_LIMIT = 56 * 1024 * 1024
FFN_VMEM_LIMIT = 60 * 1024 * 1024

PAIRS = SSM_GROUPS // 2
QUAD_K = 4 * 2 * SSM_GROUP
QUADS = PAIRS // 4
HALF_SLABS = PAIRS // 2
OCTS = D_MODEL // MXU_DIM
OCT_K = SSM_LANES // OCTS

BF16 = jnp.bfloat16
F32 = jnp.float32


def _resident(shape):
    zeros = (0,) * len(shape)
    return pl.BlockSpec(shape, lambda *_: zeros, pipeline_mode=pl.Buffered(1))


def _params(semantics):
    return pltpu.CompilerParams(dimension_semantics=semantics, vmem_limit_bytes=VMEM_LIMIT)


def _rms(x, g):
    ms = jnp.mean(x * x, axis=-1, keepdims=True)
    return (x * lax.rsqrt(ms + EPS)) * g


def _dot(a, b):
    return jnp.dot(a, b, preferred_element_type=F32)


BF16_ROWS = 16


class _SideCast:
    def __init__(self, w, layer, n_blocks, step_of=lambda i: i):
        _, r, c = w.shape
        assert r % (n_blocks * BF16_ROWS) == 0

        def block(*idx):
            return jnp.minimum(step_of(*idx), n_blocks - 1)

        self.operand = w
        self.in_spec = pl.BlockSpec((None, r // n_blocks, c), lambda *idx: (layer, block(*idx), 0))
        self.out_spec = pl.BlockSpec((r // n_blocks, c), lambda *idx: (block(*idx), 0))
        self.out_shape = jax.ShapeDtypeStruct((r, c), BF16)


def _cast_blocks(src_refs, dst_refs):
    for src, dst in zip(src_refs, dst_refs):
        dst[...] = src[...].astype(BF16)


def _discretize_kernel(lr_ref, li_ref, ldt_ref, br_ref, bi_ref, are_ref, aim_ref, bbr_ref, bbi_ref):
    lr = lr_ref[...]
    li = li_ref[...]
    dt = jnp.exp(ldt_ref[...])
    mag = jnp.exp(lr * dt)
    a_re = mag * jnp.cos(li * dt)
    a_im = mag * jnp.sin(li * dt)
    den = lr * lr + li * li
    nr = a_re - 1.0
    ni = a_im
    q_re = (nr * lr + ni * li) / den
    q_im = (ni * lr - nr * li) / den
    br = br_ref[...]
    bi = bi_ref[...]
    are_ref[...] = a_re
    aim_ref[...] = a_im
    bbr_ref[...] = q_re * br - q_im * bi
    bbi_ref[...] = q_re * bi + q_im * br


def _discretize(lambda_re, lambda_im, log_dt, b_re, b_im):
    g, p, c = b_re.shape
    full = (g, c, p)
    flat = (g * c * p // LANES, LANES)
    args = [
        jnp.broadcast_to(lambda_re[:, None, :], full).reshape(flat),
        jnp.broadcast_to(lambda_im[:, None, :], full).reshape(flat),
        jnp.broadcast_to(log_dt[:, None, None], full).reshape(flat),
        jnp.transpose(b_re, (0, 2, 1)).reshape(flat),
        jnp.transpose(b_im, (0, 2, 1)).reshape(flat),
    ]
    out = jax.ShapeDtypeStruct(flat, F32)
    a_re, a_im, bb_re, bb_im = pl.pallas_call(
        _discretize_kernel, out_shape=(out, out, out, out), name="ssm_discretize")(*args)
    a_re = a_re.reshape(full)[:, 0, :]
    a_im = a_im.reshape(full)[:, 0, :]
    return a_re, a_im, bb_re.reshape(full), bb_im.reshape(full)


def _ssm_tables(a_re, a_im, bb_re, bb_im, c_re, c_im):
    c, p = SSM_GROUP, SSM_STATE
    row = lax.broadcasted_iota(jnp.int32, (2 * QUAD_K, 2 * MXU_DIM), 0)
    col = lax.broadcasted_iota(jnp.int32, (2 * QUAD_K, 2 * MXU_DIM), 1)
    row_h, row_g8 = row // QUAD_K, (row % QUAD_K) // c
    col_s, col_g2 = col // MXU_DIM, (col % LANES) // p
    hit = row_g8 == 4 * row_h + 2 * col_s + col_g2
    re = bb_re.reshape(QUADS, QUAD_K, p)
    im = bb_im.reshape(QUADS, QUAD_K, p)
    tiled = jnp.concatenate([re, re, im, im] * 2, axis=-1)
    w_b = jnp.where(hit[None], jnp.concatenate([tiled, tiled], axis=1), 0.0).astype(BF16)

    groups = MXU_DIM // c
    row_g = lax.broadcasted_iota(jnp.int32, (OCT_K, MXU_DIM), 0) // p
    col_g = lax.broadcasted_iota(jnp.int32, (OCT_K, MXU_DIM), 1) // c

    def out_blocks(cm):
        cm = jnp.transpose(cm.reshape(OCTS, groups, c, p), (0, 1, 3, 2)).reshape(OCTS, OCT_K, c)
        return jnp.where((row_g == col_g)[None], jnp.tile(cm, (1, 1, groups)), 0.0).astype(BF16)

    a_re_t = jnp.broadcast_to(a_re.reshape(1, SSM_LANES), (SUBLANES, SSM_LANES))
    a_im_t = jnp.broadcast_to(a_im.reshape(1, SSM_LANES), (SUBLANES, SSM_LANES))
    return (w_b, a_re_t, a_im_t, out_blocks(c_re), out_blocks(-c_im))


def _norm_matmul_kernel(x_ref, g_ref, w_ref, *rest):
    n_cast = (len(rest) - 2) // 2
    o_ref, wb_ref = rest[n_cast], rest[-1]
    _cast_blocks(rest[:n_cast], rest[n_cast + 1:-1])

    @pl.when(pl.program_id(0) == 0)
    def _():
        wb_ref[...] = w_ref[...].astype(BF16)

    hn = _rms(x_ref[...], g_ref[...]).astype(BF16)
    o_ref[...] = _dot(hn, wb_ref[...])


def _norm_matmul(x, g, w, layer, tm, casts=()):
    m, d = x.shape
    n = w.shape[2]
    casts = [_SideCast(cw, cl, m // tm) for cw, cl in casts]
    return pl.pallas_call(
        _norm_matmul_kernel,
        grid=(m // tm,),
        in_specs=[pl.BlockSpec((tm, d), lambda i: (i, 0)), _resident((1, d)),
                  pl.BlockSpec((None, d, n), lambda i: (layer, 0, 0), pipeline_mode=pl.Buffered(1))]
                 + [c.in_spec for c in casts],
        out_specs=[pl.BlockSpec((tm, n), lambda i: (i, 0))] + [c.out_spec for c in casts],
        out_shape=[jax.ShapeDtypeStruct((m, n), F32)] + [c.out_shape for c in casts],
        scratch_shapes=[pltpu.VMEM((d, n), BF16)],
        compiler_params=_params(("arbitrary",)),
        name="ssm_norm_in_proj",
    )(x, g.reshape(1, d), w, *[c.operand for c in casts])


SCAN_COLS = 1024


def _ssm_scan_kernel(u_ref, wb_ref, are_ref, aim_ref, wcr_ref, wci_ref, dsk_ref, h0r_ref, h0i_ref,
                     z_ref, sr_ref, si_ref, bur_ref, bui_ref, *, nb, tt):
    @pl.when(pl.program_id(1) == 0)
    def _():
        sr_ref[...] = h0r_ref[...]
        si_ref[...] = h0i_ref[...]

    u = u_ref[...]
    ub = u.astype(BF16)
    for q in range(QUADS):
        lhs = ub[:, QUAD_K * q:QUAD_K * (q + 1)]
        for h in range(2):
            res = _dot(lhs, wb_ref[q, QUAD_K * h:QUAD_K * (h + 1), :])
            for s in range(2):
                j = 4 * q + 2 * h + s
                bur_ref[:, LANES * j:LANES * (j + 1)] = res[:, MXU_DIM * s:MXU_DIM * s + LANES]
                bui_ref[:, LANES * j:LANES * (j + 1)] = res[:, MXU_DIM * s + LANES:MXU_DIM * (s + 1)]

    for cb in range(SSM_LANES // SCAN_COLS):
        cols = slice(cb * SCAN_COLS, (cb + 1) * SCAN_COLS)
        ar = are_ref[:, cols]
        ai = aim_ref[:, cols]
        for sg in range(nb // SUBLANES):
            seqs = slice(sg * SUBLANES, (sg + 1) * SUBLANES)
            sr = sr_ref[seqs, cols]
            si = si_ref[seqs, cols]
            for t in range(tt):
                rows = slice(t * nb + sg * SUBLANES, t * nb + (sg + 1) * SUBLANES)
                nr = ar * sr - ai * si + bur_ref[rows, cols]
                ni = ar * si + ai * sr + bui_ref[rows, cols]
                sr, si = nr, ni
                bur_ref[rows, cols] = sr
                bui_ref[rows, cols] = si
            sr_ref[seqs, cols] = sr
            si_ref[seqs, cols] = si

    for o in range(OCTS):
        kk = slice(o * OCT_K, (o + 1) * OCT_K)
        nn = slice(o * MXU_DIM, (o + 1) * MXU_DIM)
        y = _dot(bur_ref[:, kk].astype(BF16), wcr_ref[o]) + _dot(bui_ref[:, kk].astype(BF16), wci_ref[o])
        y = y + dsk_ref[:, nn] * u[:, nn]
        z_ref[:, nn] = jax.nn.gelu(y).astype(BF16)


def _ssm_scan(u, tables, d_skip, h0_re, h0_im, *, nb, tt, n_bt, n_tt):
    w_b, a_re_t, a_im_t, w_cr, w_ci = tables
    m, d = u.shape
    r = nb * tt
    assert m == r * n_bt * n_tt and nb % SUBLANES == 0
    rows = pl.BlockSpec((r, d), lambda b, t: (b * n_tt + t, 0))
    state = pl.BlockSpec((nb, SSM_LANES), lambda b, t: (b, 0))
    st_shape = jax.ShapeDtypeStruct((nb * n_bt, SSM_LANES), F32)
    return pl.pallas_call(
        functools.partial(_ssm_scan_kernel, nb=nb, tt=tt),
        grid=(n_bt, n_tt),
        in_specs=[rows, _resident(w_b.shape), _resident(a_re_t.shape), _resident(a_im_t.shape),
                  _resident(w_cr.shape), _resident(w_ci.shape), _resident((1, d)), state, state],
        out_specs=(rows, state, state),
        out_shape=(jax.ShapeDtypeStruct((m, d), BF16), st_shape, st_shape),
        scratch_shapes=[pltpu.VMEM((r, SSM_LANES), F32), pltpu.VMEM((r, SSM_LANES), F32)],
        compiler_params=_params(("parallel", "arbitrary")),
        name="ssm_scan",
    )(u, w_b, a_re_t, a_im_t, w_cr, w_ci, d_skip.reshape(1, d), h0_re, h0_im)


SCAN_SLABS = 8


def _ssm_scan_split_kernel(u_ref, wb_ref, are_ref, aim_ref, wcr_ref, wci_ref, dsk_ref, cast_src_ref,
                           z_ref, sr_ref, si_ref, cast_dst_ref, lhs_ref, bur_ref, bui_ref, y_ref, *, nb, tt):
    r = nb * tt
    i = pl.program_id(0)
    _cast_blocks([cast_src_ref], [cast_dst_ref])

    @pl.when(i == 0)
    def _():
        lhs_ref[...] = jnp.zeros_like(lhs_ref)
        sr_ref[...] = jnp.zeros_like(sr_ref)
        si_ref[...] = jnp.zeros_like(si_ref)

    def project_in(q):
        for b in range(nb):
            blk = u_ref[b, :, LANES * q:LANES * (q + 1)]
            for h in range(2):
                lhs_ref[q, h, pl.ds(2 * b + h, tt, stride=2 * nb), :] = blk
        lhs = jnp.concatenate([lhs_ref[q, 0], lhs_ref[q, 1]], axis=1).astype(BF16)
        res = _dot(lhs, wb_ref[q])
        for s in range(2):
            bur_ref[2 * q + s] = res[:, MXU_DIM * s:MXU_DIM * s + LANES]
            bui_ref[2 * q + s] = res[:, MXU_DIM * s + LANES:MXU_DIM * (s + 1)]

    def recur(slabs):
        ar = [are_ref[:, LANES * k:LANES * (k + 1)] for k in slabs]
        ai = [aim_ref[:, LANES * k:LANES * (k + 1)] for k in slabs]
        sr = [sr_ref[:, LANES * k:LANES * (k + 1)] for k in slabs]
        si = [si_ref[:, LANES * k:LANES * (k + 1)] for k in slabs]
        for t in range(tt):
            rows = slice(SUBLANES * t, SUBLANES * (t + 1))
            for n, k in enumerate(slabs):
                nr = ar[n] * sr[n] - ai[n] * si[n] + bur_ref[k, rows, :]
                ni = ar[n] * si[n] + ai[n] * sr[n] + bui_ref[k, rows, :]
                sr[n], si[n] = nr, ni
                bur_ref[k, rows, :] = nr
                bui_ref[k, rows, :] = ni
        for n, k in enumerate(slabs):
            sr_ref[:, LANES * k:LANES * (k + 1)] = sr[n]
            si_ref[:, LANES * k:LANES * (k + 1)] = si[n]

    def project_out(o):
        order = [(2 * (2 * o + ql) + s, h) for ql in range(2) for h in range(2) for s in range(2)]
        lre = jnp.concatenate([bur_ref[k, pl.ds(h, r, stride=2), :] for k, h in order], axis=1).astype(BF16)
        lim = jnp.concatenate([bui_ref[k, pl.ds(h, r, stride=2), :] for k, h in order], axis=1).astype(BF16)
        y = _dot(lre, wcr_ref[o]) + _dot(lim, wci_ref[o])
        for half in range(2):
            y_ref[2 * o + half] = y[:, LANES * half:LANES * (half + 1)]
        for b in range(nb):
            for half in range(2):
                cols = slice(MXU_DIM * o + LANES * half, MXU_DIM * o + LANES * (half + 1))
                yb = y_ref[2 * o + half, pl.ds(b, tt, stride=nb), :] + dsk_ref[:, cols] * u_ref[b, :, cols]
                z_ref[b, :, cols] = jax.nn.gelu(yb).astype(BF16)

    quads_per_group, tiles_per_group = SCAN_SLABS // 2, SCAN_SLABS // 4
    for g in range(HALF_SLABS // SCAN_SLABS):
        for q in range(quads_per_group * g, quads_per_group * (g + 1)):
            project_in(q)
        recur(range(SCAN_SLABS * g, SCAN_SLABS * (g + 1)))
        for o in range(tiles_per_group * g, tiles_per_group * (g + 1)):
            project_out(o)


def _ssm_scan_split(u, tables, d_skip, tt, cast_w, cast_layer):
    w_b, a_re_t, a_im_t, w_cr, w_ci = tables
    nb, seq, d = u.shape
    assert 2 * nb == SUBLANES and seq % tt == 0
    cast = _SideCast(cast_w, cast_layer, seq // tt)

    def split_lanes(a):
        halves = jnp.transpose(a[0].reshape(QUADS, 2, 2 * LANES), (1, 0, 2)).reshape(2, HALF_SLABS * LANES)
        return jnp.tile(halves, (nb, 1))

    rows = pl.BlockSpec((nb, tt, d), lambda i: (0, i, 0))
    st_shape = jax.ShapeDtypeStruct((SUBLANES, HALF_SLABS * LANES), F32)
    r = nb * tt
    z, s_re, s_im, w_cast = pl.pallas_call(
        functools.partial(_ssm_scan_split_kernel, nb=nb, tt=tt),
        grid=(seq // tt,),
        in_specs=[rows, _resident(w_b.shape), _resident(st_shape.shape), _resident(st_shape.shape),
                  _resident(w_cr.shape), _resident(w_ci.shape), _resident((1, d)), cast.in_spec],
        out_specs=(rows, pl.BlockSpec(st_shape.shape, lambda i: (0, 0)), pl.BlockSpec(st_shape.shape, lambda i: (0, 0)),
                   cast.out_spec),
        out_shape=(jax.ShapeDtypeStruct((nb, seq, d), BF16), st_shape, st_shape, cast.out_shape),
        scratch_shapes=[pltpu.VMEM((QUADS, 2, 2 * r, LANES), F32),
                        pltpu.VMEM((HALF_SLABS, 2 * r, LANES), F32),
                        pltpu.VMEM((HALF_SLABS, 2 * r, LANES), F32),
                        pltpu.VMEM((2 * OCTS, r, LANES), F32)],
        compiler_params=_params(("arbitrary",)),
        name="ssm_scan_split",
    )(u, w_b, split_lanes(a_re_t), split_lanes(a_im_t), w_cr, w_ci, d_skip.reshape(1, d), cast.operand)

    def join(s):
        return jnp.transpose(s.reshape(nb, 2, QUADS, 2 * LANES), (0, 2, 1, 3)).reshape(nb, SSM_LANES)

    return z, join(s_re), join(s_im), w_cast


SSM_OUT_COLS = 512


def _ssm_out_kernel(z_ref, x_ref, w_ref, *rest):
    o_ref = rest[len(rest) // 2]
    _cast_blocks(rest[:len(rest) // 2], rest[len(rest) // 2 + 1:])
    z = z_ref[...]
    for c in range(D_MODEL // SSM_OUT_COLS):
        cols = slice(c * SSM_OUT_COLS, (c + 1) * SSM_OUT_COLS)
        gate_cols = slice(D_MODEL + c * SSM_OUT_COLS, D_MODEL + (c + 1) * SSM_OUT_COLS)
        val = _dot(z, w_ref[:, cols])
        gate = _dot(z, w_ref[:, gate_cols])
        o_ref[:, cols] = x_ref[:, cols] + val * jax.nn.sigmoid(gate)


def _ssm_out(z, x, w, tm, casts=()):
    m, d = x.shape
    rows = pl.BlockSpec((tm, d), lambda i: (i, 0))
    casts = [_SideCast(cw, layer, m // tm) for cw, layer in casts]
    return pl.pallas_call(
        _ssm_out_kernel,
        grid=(m // tm,),
        in_specs=[rows, rows, _resident(w.shape)] + [c.in_spec for c in casts],
        out_specs=[rows] + [c.out_spec for c in casts],
        out_shape=[jax.ShapeDtypeStruct((m, d), F32)] + [c.out_shape for c in casts],
        compiler_params=_params(("parallel",)),
        name="ssm_out_glu",
    )(z, x, w, *[c.operand for c in casts])


def _ffn_kernel(x_ref, g_ref, wg_ref, wu_ref, wd_ref, gf_ref, *rest, final_norm):
    n_cast = (len(rest) - 2) // 2
    o_ref, hn_ref = rest[n_cast], rest[-1]
    _cast_blocks(rest[:n_cast], rest[n_cast + 1:-1])
    h = pl.program_id(1)

    @pl.when(h == 0)
    def _():
        x = x_ref[...]
        hn_ref[...] = _rms(x, g_ref[...]).astype(BF16)
        o_ref[...] = x

    hn = hn_ref[...]
    act = (jax.nn.silu(_dot(hn, wg_ref[...])) * _dot(hn, wu_ref[...])).astype(BF16)
    o_ref[...] += _dot(act, wd_ref[...])

    if final_norm:
        @pl.when(h == pl.num_programs(1) - 1)
        def _():
            o_ref[...] = _rms(o_ref[...], gf_ref[...])


def _ffn(x, g, w_gate_up, w_down, g_final, *, tm, th, final_norm, casts=()):
    m, d = x.shape
    n_h = FFN_HIDDEN // th
    rows = pl.BlockSpec((tm, d), lambda i, h: (i, 0))
    assert all(nb <= (m // tm) * n_h for _, _, nb in casts)
    casts = [_SideCast(cw, layer, nb, lambda i, h: i * n_h + h) for cw, layer, nb in casts]
    return pl.pallas_call(
        functools.partial(_ffn_kernel, final_norm=final_norm),
        grid=(m // tm, n_h),
        in_specs=[rows,
                  _resident((1, d)),
                  pl.BlockSpec((d, th), lambda i, h: (0, h)),
                  pl.BlockSpec((d, th), lambda i, h: (0, n_h + h)),
                  pl.BlockSpec((th, d), lambda i, h: (h, 0)),
                  _resident((1, d))] + [c.in_spec for c in casts],
        out_specs=[rows] + [c.out_spec for c in casts],
        out_shape=[jax.ShapeDtypeStruct((m, d), F32)] + [c.out_shape for c in casts],
        scratch_shapes=[pltpu.VMEM((tm, d), BF16)],
        compiler_params=pltpu.CompilerParams(dimension_semantics=("arbitrary", "arbitrary"),
                                             vmem_limit_bytes=FFN_VMEM_LIMIT),
        name="ffn_swiglu",
    )(x, g.reshape(1, d), w_gate_up, w_gate_up, w_down, g_final.reshape(1, d), *[c.operand for c in casts])


def _gmlp_gate_inputs(x, g, win_ref, v_gain):
    hn = _rms(x, g).astype(BF16)
    v = jax.nn.gelu(_dot(hn, win_ref[:, GMLP_WIDTH:]))
    vc = v - jnp.mean(v, axis=-1, keepdims=True)
    vn = (vc * lax.rsqrt(jnp.mean(vc * vc, axis=-1, keepdims=True) + EPS)) * v_gain
    u = jax.nn.gelu(_dot(hn, win_ref[:, :GMLP_WIDTH]))
    return u, vn


def _gmlp_prompt_kernel(x_ref, g_ref, win_ref, vg_ref, ws_ref, bs_ref, wout_ref, *rest):
    n_cast = (len(rest) - 2) // 2
    o_ref, gate_ref = rest[n_cast], rest[-1]
    _cast_blocks(rest[:n_cast], rest[n_cast + 1:-1])
    x = x_ref[...]
    u, vn = _gmlp_gate_inputs(x, g_ref[...], win_ref, vg_ref[...])
    vb = vn.astype(BF16)
    q_idx = lax.broadcasted_iota(jnp.int32, (CHUNK, CHUNK), 0)
    k_idx = lax.broadcasted_iota(jnp.int32, (CHUNK, CHUNK), 1)
    causal = k_idx <= q_idx
    for h in range(GMLP_HEADS):
        cols = slice(h * GMLP_HEAD_DIM, (h + 1) * GMLP_HEAD_DIM)
        ws = jnp.where(causal, ws_ref[h], 0.0).astype(BF16)
        bias = bs_ref[:, h:h + 1]
        for c in range(0, x.shape[0] // CHUNK, 2):
            lo = slice(c * CHUNK, (c + 1) * CHUNK)
            hi = slice((c + 1) * CHUNK, (c + 2) * CHUNK)
            s = _dot(ws, jnp.concatenate([vb[lo, cols], vb[hi, cols]], axis=1)) + bias
            gate_ref[lo, cols] = (u[lo, cols] * s[:, :GMLP_HEAD_DIM]).astype(BF16)
            gate_ref[hi, cols] = (u[hi, cols] * s[:, GMLP_HEAD_DIM:]).astype(BF16)
    o_ref[...] = x + _dot(gate_ref[...], wout_ref[...])


def _gmlp_sample_kernel(x_ref, g_ref, win_ref, vg_ref, wq_ref, bq_ref, wout_ref, o_ref, v_ref, *, steps):
    x = x_ref[...]
    u, vn = _gmlp_gate_inputs(x, g_ref[...], win_ref, vg_ref[...])
    v_ref[...] = vn
    nseq = x.shape[0] // steps
    gates = []
    for q in range(steps):
        s = bq_ref[q:q + 1, :]
        for k in range(q + 1):
            s = s + wq_ref[q * steps + k:q * steps + k + 1, :] * vn[k * nseq:(k + 1) * nseq, :]
        gates.append(u[q * nseq:(q + 1) * nseq, :] * s)
    gate = jnp.concatenate(gates, axis=0).astype(BF16)
    o_ref[...] = x + _dot(gate, wout_ref[...])


def _gmlp_prompt(x, g, w_in, v_gain, w_spatial, b_spatial_t, w_out, tm, casts=()):
    m, d = x.shape
    rows = pl.BlockSpec((tm, d), lambda i: (i, 0))
    casts = [_SideCast(cw, layer, m // tm) for cw, layer in casts]
    return pl.pallas_call(
        _gmlp_prompt_kernel,
        grid=(m // tm,),
        in_specs=[rows, _resident((1, d)), _resident(w_in.shape), _resident((1, GMLP_WIDTH)),
                  _resident(w_spatial.shape), _resident(b_spatial_t.shape), _resident(w_out.shape)]
                 + [c.in_spec for c in casts],
        out_specs=[rows] + [c.out_spec for c in casts],
        out_shape=[jax.ShapeDtypeStruct((m, d), F32)] + [c.out_shape for c in casts],
        scratch_shapes=[pltpu.VMEM((tm, GMLP_WIDTH), BF16)],
        compiler_params=_params(("parallel",)),
        name="gmlp_prompt",
    )(x, g.reshape(1, d), w_in, v_gain.reshape(1, GMLP_WIDTH), w_spatial, b_spatial_t, w_out,
      *[c.operand for c in casts])


def _gmlp_sample(x, g, w_in, v_gain, w_q, b_q, w_out, tm, steps):
    m, d = x.shape
    rows = pl.BlockSpec((tm, d), lambda i: (i, 0))
    return pl.pallas_call(
        functools.partial(_gmlp_sample_kernel, steps=steps),
        grid=(m // tm,),
        in_specs=[rows, _resident((1, d)), _resident(w_in.shape), _resident((1, GMLP_WIDTH)),
                  _resident(w_q.shape), _resident(b_q.shape), _resident(w_out.shape)],
        out_specs=(rows, pl.BlockSpec((tm, GMLP_WIDTH), lambda i: (i, 0))),
        out_shape=(jax.ShapeDtypeStruct((m, d), F32), jax.ShapeDtypeStruct((m, GMLP_WIDTH), F32)),
        compiler_params=_params(("parallel",)),
        name="gmlp_sample",
    )(x, g.reshape(1, d), w_in, v_gain.reshape(1, GMLP_WIDTH), w_q, b_q, w_out)


PROMPT_SCAN_STEPS = 64
SAMPLE_TILE_SEQS = 32


def kernel(x_prompt, x_sample, state_ssm_re, state_ssm_im, norm_mix, norm_ffn, norm_final, ssm_w_in, ssm_lambda_re, ssm_lambda_im, ssm_log_dt, ssm_b_re, ssm_b_im, ssm_c_re, ssm_c_im, ssm_d, ssm_w_out, gmlp_w_in, gmlp_v_gain, gmlp_w_spatial, gmlp_b_spatial, gmlp_w_out, ffn_w_gate_up, ffn_w_down):
    bsz, seq, d = x_prompt.shape
    dbsz, dseq, _ = x_sample.shape

    a_re, a_im, bb_re, bb_im = _discretize(ssm_lambda_re[0], ssm_lambda_im[0], ssm_log_dt[0],
                                           ssm_b_re[0], ssm_b_im[0])
    tables = _ssm_tables(a_re, a_im, bb_re, bb_im, ssm_c_re[0], ssm_c_im[0])

    def ffn(x_rows, layer, w_gu, w_dn, ffn_tm, casts=()):
        return _ffn(x_rows, norm_ffn[layer], w_gu, w_dn, norm_final, tm=ffn_tm, th=512, final_norm=layer == 1,
                    casts=casts)

    xp = x_prompt.reshape(bsz * seq, d)
    u, w_out0 = _norm_matmul(xp, norm_mix[0], ssm_w_in, 0, 512, casts=[(ssm_w_out, 0)])
    z, p_re, p_im, w_gu0 = _ssm_scan_split(u.reshape(bsz, seq, d), tables, ssm_d[0], PROMPT_SCAN_STEPS,
                                           ffn_w_gate_up, 0)
    xp1, w_dn0, g_w_out = _ssm_out(z.reshape(bsz * seq, d), xp, w_out0, 512,
                                   casts=[(ffn_w_down, 0), (gmlp_w_out, 0)])
    xp2, w_gu1, w_dn1 = ffn(xp1, 0, w_gu0, w_dn0, 1024, casts=[(ffn_w_gate_up, 1, 64), (ffn_w_down, 1, 88)])
    g_w_in = gmlp_w_in[0].astype(BF16)
    xp3, = _gmlp_prompt(xp2, norm_mix[1], g_w_in, gmlp_v_gain[0], gmlp_w_spatial[0],
                        jnp.transpose(gmlp_b_spatial[0]), g_w_out, 512)
    y_prompt = ffn(xp3, 1, w_gu1, w_dn1, 1024)[0].reshape(bsz, seq, d)

    n_bt = dbsz // SAMPLE_TILE_SEQS
    tile_rows = SAMPLE_TILE_SEQS * dseq

    def to_rows(a):
        return jnp.transpose(a.reshape(n_bt, SAMPLE_TILE_SEQS, dseq, d), (0, 2, 1, 3)).reshape(dbsz * dseq, d)

    def from_rows(a):
        return jnp.transpose(a.reshape(n_bt, dseq, SAMPLE_TILE_SEQS, d), (0, 2, 1, 3)).reshape(dbsz, dseq, d)

    xs = to_rows(x_sample)
    h0_re = state_ssm_re[0].reshape(dbsz, SSM_LANES)
    h0_im = state_ssm_im[0].reshape(dbsz, SSM_LANES)
    us, = _norm_matmul(xs, norm_mix[0], ssm_w_in, 0, dbsz * dseq)
    zs, s_re, s_im = _ssm_scan(us, tables, ssm_d[0], h0_re, h0_im, nb=SAMPLE_TILE_SEQS, tt=dseq, n_bt=n_bt, n_tt=1)
    xs1, = _ssm_out(zs, xs, w_out0, tile_rows)
    xs2, = ffn(xs1, 0, w_gu0, w_dn0, dbsz * dseq)
    w_q = jnp.repeat(gmlp_w_spatial[0][:, :dseq, :dseq].reshape(GMLP_HEADS, dseq * dseq).T, GMLP_HEAD_DIM, axis=1)
    b_q = jnp.repeat(gmlp_b_spatial[0][:, :dseq].T, GMLP_HEAD_DIM, axis=1)
    xs3, v_rows = _gmlp_sample(xs2, norm_mix[1], g_w_in, gmlp_v_gain[0], w_q, b_q, g_w_out, tile_rows, dseq)
    y_sample = from_rows(ffn(xs3, 1, w_gu1, w_dn1, dbsz * dseq)[0])

    state_shape = (1, -1, SSM_GROUPS, SSM_STATE)
    return (y_prompt, y_sample,
            p_re.reshape(state_shape), p_im.reshape(state_shape),
            s_re.reshape(state_shape), s_im.reshape(state_shape),
            from_rows(v_rows)[None])
```

```python
import functools

import jax
import jax.numpy as jnp
from jax import lax
from jax.experimental import pallas as pl
from jax.experimental.pallas import tpu as pltpu

D_MODEL = 2048
SSM_GROUPS = 128
SSM_GROUP = 16
SSM_STATE = 64
SSM_LANES = SSM_GROUPS * SSM_STATE
GMLP_WIDTH = D_MODEL
GMLP_HEADS = 16
GMLP_HEAD_DIM = GMLP_WIDTH // GMLP_HEADS
CHUNK = 128
FFN_HIDDEN = 5632
EPS = 1e-6

LANES = 128
SUBLANES = 8
MXU_DIM = 256
VMEM_LIMIT = 56 * 1024 * 1024
BIG_VMEM_LIMIT = 60 * 1024 * 1024

PAIRS = SSM_GROUPS // 2
QUAD_K = 4 * 2 * SSM_GROUP
QUADS = PAIRS // 4
HALF_SLABS = PAIRS // 2
OCTS = D_MODEL // MXU_DIM
OCT_K = SSM_LANES // OCTS

BF16 = jnp.bfloat16
F32 = jnp.float32


def _resident(shape):
    zeros = (0,) * len(shape)
    return pl.BlockSpec(shape, lambda *_: zeros, pipeline_mode=pl.Buffered(1))


def _params(semantics):
    return pltpu.CompilerParams(dimension_semantics=semantics, vmem_limit_bytes=VMEM_LIMIT)


def _rms(x, g):
    ms = jnp.mean(x * x, axis=-1, keepdims=True)
    return (x * lax.rsqrt(ms + EPS)) * g


def _dot(a, b):
    return jnp.dot(a, b, preferred_element_type=F32)


BF16_ROWS = 16


class _SideCast:
    def __init__(self, w, layer, n_blocks, step_of=lambda i: i):
        _, r, c = w.shape
        assert r % (n_blocks * BF16_ROWS) == 0

        def block(*idx):
            return jnp.minimum(step_of(*idx), n_blocks - 1)

        self.operand = w
        self.in_spec = pl.BlockSpec((None, r // n_blocks, c), lambda *idx: (layer, block(*idx), 0))
        self.out_spec = pl.BlockSpec((r // n_blocks, c), lambda *idx: (block(*idx), 0))
        self.out_shape = jax.ShapeDtypeStruct((r, c), BF16)


def _cast_blocks(src_refs, dst_refs):
    for src, dst in zip(src_refs, dst_refs):
        dst[...] = src[...].astype(BF16)


def _discretize_kernel(lr_ref, li_ref, ldt_ref, br_ref, bi_ref, are_ref, aim_ref, bbr_ref, bbi_ref):
    lr = lr_ref[...]
    li = li_ref[...]
    dt = jnp.exp(ldt_ref[...])
    mag = jnp.exp(lr * dt)
    a_re = mag * jnp.cos(li * dt)
    a_im = mag * jnp.sin(li * dt)
    den = lr * lr + li * li
    nr = a_re - 1.0
    ni = a_im
    q_re = (nr * lr + ni * li) / den
    q_im = (ni * lr - nr * li) / den
    br = br_ref[...]
    bi = bi_ref[...]
    are_ref[...] = a_re
    aim_ref[...] = a_im
    bbr_ref[...] = q_re * br - q_im * bi
    bbi_ref[...] = q_re * bi + q_im * br


def _discretize(lambda_re, lambda_im, log_dt, b_re, b_im):
    g, p, c = b_re.shape
    full = (g, c, p)
    flat = (g * c * p // LANES, LANES)
    args = [
        jnp.broadcast_to(lambda_re[:, None, :], full).reshape(flat),
        jnp.broadcast_to(lambda_im[:, None, :], full).reshape(flat),
        jnp.broadcast_to(log_dt[:, None, None], full).reshape(flat),
        jnp.transpose(b_re, (0, 2, 1)).reshape(flat),
        jnp.transpose(b_im, (0, 2, 1)).reshape(flat),
    ]
    out = jax.ShapeDtypeStruct(flat, F32)
    a_re, a_im, bb_re, bb_im = pl.pallas_call(
        _discretize_kernel, out_shape=(out, out, out, out), name="ssm_discretize")(*args)
    a_re = a_re.reshape(full)[:, 0, :]
    a_im = a_im.reshape(full)[:, 0, :]
    return a_re, a_im, bb_re.reshape(full), bb_im.reshape(full)


def _ssm_tables(a_re, a_im, bb_re, bb_im, c_re, c_im):
    c, p = SSM_GROUP, SSM_STATE
    row = lax.broadcasted_iota(jnp.int32, (2 * QUAD_K, 2 * MXU_DIM), 0)
    col = lax.broadcasted_iota(jnp.int32, (2 * QUAD_K, 2 * MXU_DIM), 1)
    row_h, row_g8 = row // QUAD_K, (row % QUAD_K) // c
    col_s, col_g2 = col // MXU_DIM, (col % LANES) // p
    hit = row_g8 == 4 * row_h + 2 * col_s + col_g2
    re = bb_re.reshape(QUADS, QUAD_K, p)
    im = bb_im.reshape(QUADS, QUAD_K, p)
    tiled = jnp.concatenate([re, re, im, im] * 2, axis=-1)
    w_b = jnp.where(hit[None], jnp.concatenate([tiled, tiled], axis=1), 0.0).astype(BF16)

    groups = MXU_DIM // c
    row_g = lax.broadcasted_iota(jnp.int32, (OCT_K, MXU_DIM), 0) // p
    col_g = lax.broadcasted_iota(jnp.int32, (OCT_K, MXU_DIM), 1) // c

    def out_blocks(cm):
        cm = jnp.transpose(cm.reshape(OCTS, groups, c, p), (0, 1, 3, 2)).reshape(OCTS, OCT_K, c)
        return jnp.where((row_g == col_g)[None], jnp.tile(cm, (1, 1, groups)), 0.0).astype(BF16)

    a_re_t = jnp.broadcast_to(a_re.reshape(1, SSM_LANES), (SUBLANES, SSM_LANES))
    a_im_t = jnp.broadcast_to(a_im.reshape(1, SSM_LANES), (SUBLANES, SSM_LANES))
    return (w_b, a_re_t, a_im_t, out_blocks(c_re), out_blocks(-c_im))


def _norm_matmul_kernel(x_ref, g_ref, w_ref, *rest):
    n_cast = (len(rest) - 2) // 2
    o_ref, wb_ref = rest[n_cast], rest[-1]
    _cast_blocks(rest[:n_cast], rest[n_cast + 1:-1])

    @pl.when(pl.program_id(0) == 0)
    def _():
        wb_ref[...] = w_ref[...].astype(BF16)

    half = x_ref.shape[0] // 2
    for rows in (slice(0, half), slice(half, 2 * half)):
        hn = _rms(x_ref[rows, :], g_ref[...]).astype(BF16)
        o_ref[rows, :] = _dot(hn, wb_ref[...])


def _norm_matmul(x, g, w, layer, tm, casts=()):
    m, d = x.shape
    n = w.shape[2]
    casts = [_SideCast(cw, cl, m // tm) for cw, cl in casts]
    return pl.pallas_call(
        _norm_matmul_kernel,
        grid=(m // tm,),
        in_specs=[pl.BlockSpec((tm, d), lambda i: (i, 0)), _resident((1, d)),
                  pl.BlockSpec((None, d, n), lambda i: (layer, 0, 0), pipeline_mode=pl.Buffered(1))]
                 + [c.in_spec for c in casts],
        out_specs=[pl.BlockSpec((tm, n), lambda i: (i, 0))] + [c.out_spec for c in casts],
        out_shape=[jax.ShapeDtypeStruct((m, n), F32)] + [c.out_shape for c in casts],
        scratch_shapes=[pltpu.VMEM((d, n), BF16)],
        compiler_params=_params(("arbitrary",)),
        name="ssm_norm_in_proj",
    )(x, g.reshape(1, d), w, *[c.operand for c in casts])


SCAN_COLS = 1024


def _ssm_scan_kernel(u_ref, wb_ref, are_ref, aim_ref, wcr_ref, wci_ref, dsk_ref, h0r_ref, h0i_ref,
                     z_ref, sr_ref, si_ref, bur_ref, bui_ref, *, nb, tt):
    @pl.when(pl.program_id(1) == 0)
    def _():
        sr_ref[...] = h0r_ref[...]
        si_ref[...] = h0i_ref[...]

    u = u_ref[...]
    ub = u.astype(BF16)
    for q in range(QUADS):
        lhs = ub[:, QUAD_K * q:QUAD_K * (q + 1)]
        for h in range(2):
            res = _dot(lhs, wb_ref[q, QUAD_K * h:QUAD_K * (h + 1), :])
            for s in range(2):
                j = 4 * q + 2 * h + s
                bur_ref[:, LANES * j:LANES * (j + 1)] = res[:, MXU_DIM * s:MXU_DIM * s + LANES]
                bui_ref[:, LANES * j:LANES * (j + 1)] = res[:, MXU_DIM * s + LANES:MXU_DIM * (s + 1)]

    for cb in range(SSM_LANES // SCAN_COLS):
        cols = slice(cb * SCAN_COLS, (cb + 1) * SCAN_COLS)
        ar = are_ref[:, cols]
        ai = aim_ref[:, cols]
        for sg in range(nb // SUBLANES):
            seqs = slice(sg * SUBLANES, (sg + 1) * SUBLANES)
            sr = sr_ref[seqs, cols]
            si = si_ref[seqs, cols]
            for t in range(tt):
                rows = slice(t * nb + sg * SUBLANES, t * nb + (sg + 1) * SUBLANES)
                nr = ar * sr - ai * si + bur_ref[rows, cols]
                ni = ar * si + ai * sr + bui_ref[rows, cols]
                sr, si = nr, ni
                bur_ref[rows, cols] = sr
                bui_ref[rows, cols] = si
            sr_ref[seqs, cols] = sr
            si_ref[seqs, cols] = si

    for o in range(OCTS):
        kk = slice(o * OCT_K, (o + 1) * OCT_K)
        nn = slice(o * MXU_DIM, (o + 1) * MXU_DIM)
        y = _dot(bur_ref[:, kk].astype(BF16), wcr_ref[o]) + _dot(bui_ref[:, kk].astype(BF16), wci_ref[o])
        y = y + dsk_ref[:, nn] * u[:, nn]
        z_ref[:, nn] = jax.nn.gelu(y).astype(BF16)


def _ssm_scan(u, tables, d_skip, h0_re, h0_im, *, nb, tt, n_bt, n_tt):
    w_b, a_re_t, a_im_t, w_cr, w_ci = tables
    m, d = u.shape
    r = nb * tt
    assert m == r * n_bt * n_tt and nb % SUBLANES == 0
    rows = pl.BlockSpec((r, d), lambda b, t: (b * n_tt + t, 0))
    state = pl.BlockSpec((nb, SSM_LANES), lambda b, t: (b, 0))
    st_shape = jax.ShapeDtypeStruct((nb * n_bt, SSM_LANES), F32)
    return pl.pallas_call(
        functools.partial(_ssm_scan_kernel, nb=nb, tt=tt),
        grid=(n_bt, n_tt),
        in_specs=[rows, _resident(w_b.shape), _resident(a_re_t.shape), _resident(a_im_t.shape),
                  _resident(w_cr.shape), _resident(w_ci.shape), _resident((1, d)), state, state],
        out_specs=(rows, state, state),
        out_shape=(jax.ShapeDtypeStruct((m, d), BF16), st_shape, st_shape),
        scratch_shapes=[pltpu.VMEM((r, SSM_LANES), F32), pltpu.VMEM((r, SSM_LANES), F32)],
        compiler_params=_params(("parallel", "arbitrary")),
        name="ssm_scan",
    )(u, w_b, a_re_t, a_im_t, w_cr, w_ci, d_skip.reshape(1, d), h0_re, h0_im)


SCAN_SLABS = 8


def _ssm_scan_split_kernel(u_ref, wb_ref, are_ref, aim_ref, wcr_ref, wci_ref, dsk_ref, *rest, nb, tt):
    n_cast = (len(rest) - 7) // 2
    z_ref, sr_ref, si_ref = rest[n_cast:n_cast + 3]
    lhs_ref, bur_ref, bui_ref, y_ref = rest[-4:]
    _cast_blocks(rest[:n_cast], rest[n_cast + 3:-4])
    r = nb * tt
    i = pl.program_id(0)

    @pl.when(i == 0)
    def _():
        lhs_ref[...] = jnp.zeros_like(lhs_ref)
        sr_ref[...] = jnp.zeros_like(sr_ref)
        si_ref[...] = jnp.zeros_like(si_ref)

    def project_in(q):
        for b in range(nb):
            blk = u_ref[b, :, LANES * q:LANES * (q + 1)]
            for h in range(2):
                lhs_ref[q, h, pl.ds(2 * b + h, tt, stride=2 * nb), :] = blk
        lhs = jnp.concatenate([lhs_ref[q, 0], lhs_ref[q, 1]], axis=1).astype(BF16)
        res = _dot(lhs, wb_ref[q])
        for s in range(2):
            bur_ref[2 * q + s] = res[:, MXU_DIM * s:MXU_DIM * s + LANES]
            bui_ref[2 * q + s] = res[:, MXU_DIM * s + LANES:MXU_DIM * (s + 1)]

    def recur(slabs):
        ar = [are_ref[:, LANES * k:LANES * (k + 1)] for k in slabs]
        ai = [aim_ref[:, LANES * k:LANES * (k + 1)] for k in slabs]
        sr = [sr_ref[:, LANES * k:LANES * (k + 1)] for k in slabs]
        si = [si_ref[:, LANES * k:LANES * (k + 1)] for k in slabs]
        for t in range(tt):
            rows = slice(SUBLANES * t, SUBLANES * (t + 1))
            for n, k in enumerate(slabs):
                nr = ar[n] * sr[n] - ai[n] * si[n] + bur_ref[k, rows, :]
                ni = ar[n] * si[n] + ai[n] * sr[n] + bui_ref[k, rows, :]
                sr[n], si[n] = nr, ni
                bur_ref[k, rows, :] = nr
                bui_ref[k, rows, :] = ni
        for n, k in enumerate(slabs):
            sr_ref[:, LANES * k:LANES * (k + 1)] = sr[n]
            si_ref[:, LANES * k:LANES * (k + 1)] = si[n]

    def project_out(o):
        order = [(2 * (2 * o + ql) + s, h) for ql in range(2) for h in range(2) for s in range(2)]
        lre = jnp.concatenate([bur_ref[k, pl.ds(h, r, stride=2), :] for k, h in order], axis=1).astype(BF16)
        lim = jnp.concatenate([bui_ref[k, pl.ds(h, r, stride=2), :] for k, h in order], axis=1).astype(BF16)
        y = _dot(lre, wcr_ref[o]) + _dot(lim, wci_ref[o])
        for half in range(2):
            y_ref[half] = y[:, LANES * half:LANES * (half + 1)]
        for b in range(nb):
            for half in range(2):
                cols = slice(MXU_DIM * o + LANES * half, MXU_DIM * o + LANES * (half + 1))
                yb = y_ref[half, pl.ds(b, tt, stride=nb), :] + dsk_ref[:, cols] * u_ref[b, :, cols]
                z_ref[b, :, cols] = jax.nn.gelu(yb).astype(BF16)

    for q in range(QUADS):
        project_in(q)
    for k0 in range(0, HALF_SLABS, SCAN_SLABS):
        recur(range(k0, k0 + SCAN_SLABS))
    for o in range(OCTS):
        project_out(o)


def _ssm_scan_split(u, tables, d_skip, tt, casts):
    w_b, a_re_t, a_im_t, w_cr, w_ci = tables
    nb, seq, d = u.shape
    assert 2 * nb == SUBLANES and seq % tt == 0
    casts = [_SideCast(cw, layer, seq // tt) for cw, layer in casts]

    def split_lanes(a):
        halves = jnp.transpose(a[0].reshape(QUADS, 2, 2 * LANES), (1, 0, 2)).reshape(2, HALF_SLABS * LANES)
        return jnp.tile(halves, (nb, 1))

    rows = pl.BlockSpec((nb, tt, d), lambda i: (0, i, 0))
    st_shape = jax.ShapeDtypeStruct((SUBLANES, HALF_SLABS * LANES), F32)
    r = nb * tt
    state = pl.BlockSpec(st_shape.shape, lambda i: (0, 0))
    z, s_re, s_im, *w_cast = pl.pallas_call(
        functools.partial(_ssm_scan_split_kernel, nb=nb, tt=tt),
        grid=(seq // tt,),
        in_specs=[rows, _resident(w_b.shape), _resident(st_shape.shape), _resident(st_shape.shape),
                  _resident(w_cr.shape), _resident(w_ci.shape), _resident((1, d))] + [c.in_spec for c in casts],
        out_specs=[rows, state, state] + [c.out_spec for c in casts],
        out_shape=[jax.ShapeDtypeStruct((nb, seq, d), BF16), st_shape, st_shape] + [c.out_shape for c in casts],
        scratch_shapes=[pltpu.VMEM((QUADS, 2, 2 * r, LANES), F32),
                        pltpu.VMEM((HALF_SLABS, 2 * r, LANES), F32),
                        pltpu.VMEM((HALF_SLABS, 2 * r, LANES), F32),
                        pltpu.VMEM((2, r, LANES), F32)],
        compiler_params=pltpu.CompilerParams(dimension_semantics=("arbitrary",), vmem_limit_bytes=BIG_VMEM_LIMIT),
        name="ssm_scan_split",
    )(u, w_b, split_lanes(a_re_t), split_lanes(a_im_t), w_cr, w_ci, d_skip.reshape(1, d),
      *[c.operand for c in casts])

    def join(s):
        return jnp.transpose(s.reshape(nb, 2, QUADS, 2 * LANES), (0, 2, 1, 3)).reshape(nb, SSM_LANES)

    return (z, join(s_re), join(s_im), *w_cast)


SSM_OUT_COLS = 512


def _ssm_out_kernel(z_ref, x_ref, w_ref, *rest):
    o_ref = rest[len(rest) // 2]
    _cast_blocks(rest[:len(rest) // 2], rest[len(rest) // 2 + 1:])
    z = z_ref[...]
    for c in range(D_MODEL // SSM_OUT_COLS):
        cols = slice(c * SSM_OUT_COLS, (c + 1) * SSM_OUT_COLS)
        gate_cols = slice(D_MODEL + c * SSM_OUT_COLS, D_MODEL + (c + 1) * SSM_OUT_COLS)
        val = _dot(z, w_ref[:, cols])
        gate = _dot(z, w_ref[:, gate_cols])
        o_ref[:, cols] = x_ref[:, cols] + val * jax.nn.sigmoid(gate)


def _ssm_out(z, x, w, tm, casts=()):
    m, d = x.shape
    rows = pl.BlockSpec((tm, d), lambda i: (i, 0))
    casts = [_SideCast(cw, layer, m // tm) for cw, layer in casts]
    return pl.pallas_call(
        _ssm_out_kernel,
        grid=(m // tm,),
        in_specs=[rows, rows, _resident(w.shape)] + [c.in_spec for c in casts],
        out_specs=[rows] + [c.out_spec for c in casts],
        out_shape=[jax.ShapeDtypeStruct((m, d), F32)] + [c.out_shape for c in casts],
        compiler_params=_params(("parallel",)),
        name="ssm_out_glu",
    )(z, x, w, *[c.operand for c in casts])


def _ffn_kernel(x_ref, g_ref, wg_ref, wu_ref, wd_ref, gf_ref, *rest, final_norm):
    n_cast = (len(rest) - 2) // 2
    o_ref, hn_ref = rest[n_cast], rest[-1]
    _cast_blocks(rest[:n_cast], rest[n_cast + 1:-1])
    h = pl.program_id(1)

    @pl.when(h == 0)
    def _():
        x = x_ref[...]
        hn_ref[...] = _rms(x, g_ref[...]).astype(BF16)
        o_ref[...] = x

    hn = hn_ref[...]
    act = (jax.nn.silu(_dot(hn, wg_ref[...])) * _dot(hn, wu_ref[...])).astype(BF16)
    o_ref[...] += _dot(act, wd_ref[...])

    if final_norm:
        @pl.when(h == pl.num_programs(1) - 1)
        def _():
            o_ref[...] = _rms(o_ref[...], gf_ref[...])


def _ffn(x, g, w_gate_up, w_down, g_final, *, tm, th, final_norm, casts=()):
    m, d = x.shape
    n_h = FFN_HIDDEN // th
    rows = pl.BlockSpec((tm, d), lambda i, h: (i, 0))
    assert all(nb <= (m // tm) * n_h for _, _, nb in casts)
    casts = [_SideCast(cw, layer, nb, lambda i, h: i * n_h + h) for cw, layer, nb in casts]
    return pl.pallas_call(
        functools.partial(_ffn_kernel, final_norm=final_norm),
        grid=(m // tm, n_h),
        in_specs=[rows,
                  _resident((1, d)),
                  pl.BlockSpec((d, th), lambda i, h: (0, h)),
                  pl.BlockSpec((d, th), lambda i, h: (0, n_h + h)),
                  pl.BlockSpec((th, d), lambda i, h: (h, 0)),
                  _resident((1, d))] + [c.in_spec for c in casts],
        out_specs=[rows] + [c.out_spec for c in casts],
        out_shape=[jax.ShapeDtypeStruct((m, d), F32)] + [c.out_shape for c in casts],
        scratch_shapes=[pltpu.VMEM((tm, d), BF16)],
        compiler_params=pltpu.CompilerParams(dimension_semantics=("arbitrary", "arbitrary"),
                                             vmem_limit_bytes=BIG_VMEM_LIMIT),
        name="ffn_swiglu",
    )(x, g.reshape(1, d), w_gate_up, w_gate_up, w_down, g_final.reshape(1, d), *[c.operand for c in casts])


def _gmlp_gate_inputs(x, g, win_ref, v_gain):
    hn = _rms(x, g).astype(BF16)
    v = jax.nn.gelu(_dot(hn, win_ref[:, GMLP_WIDTH:]))
    vc = v - jnp.mean(v, axis=-1, keepdims=True)
    vn = (vc * lax.rsqrt(jnp.mean(vc * vc, axis=-1, keepdims=True) + EPS)) * v_gain
    u = jax.nn.gelu(_dot(hn, win_ref[:, :GMLP_WIDTH]))
    return u, vn


def _gmlp_prompt_kernel(x_ref, g_ref, win_ref, vg_ref, ws_ref, bs_ref, wout_ref, *rest):
    n_cast = (len(rest) - 2) // 2
    o_ref, gate_ref = rest[n_cast], rest[-1]
    _cast_blocks(rest[:n_cast], rest[n_cast + 1:-1])
    x = x_ref[...]
    u, vn = _gmlp_gate_inputs(x, g_ref[...], win_ref, vg_ref[...])
    vb = vn.astype(BF16)
    q_idx = lax.broadcasted_iota(jnp.int32, (CHUNK, CHUNK), 0)
    k_idx = lax.broadcasted_iota(jnp.int32, (CHUNK, CHUNK), 1)
    causal = k_idx <= q_idx
    for h in range(GMLP_HEADS):
        cols = slice(h * GMLP_HEAD_DIM, (h + 1) * GMLP_HEAD_DIM)
        ws = jnp.where(causal, ws_ref[h], 0.0).astype(BF16)
        bias = bs_ref[:, h:h + 1]
        for c in range(0, x.shape[0] // CHUNK, 2):
            lo = slice(c * CHUNK, (c + 1) * CHUNK)
            hi = slice((c + 1) * CHUNK, (c + 2) * CHUNK)
            s = _dot(ws, jnp.concatenate([vb[lo, cols], vb[hi, cols]], axis=1)) + bias
            gate_ref[lo, cols] = (u[lo, cols] * s[:, :GMLP_HEAD_DIM]).astype(BF16)
            gate_ref[hi, cols] = (u[hi, cols] * s[:, GMLP_HEAD_DIM:]).astype(BF16)
    o_ref[...] = x + _dot(gate_ref[...], wout_ref[...])


def _gmlp_sample_kernel(x_ref, g_ref, win_ref, vg_ref, wq_ref, bq_ref, wout_ref, o_ref, v_ref, *, steps):
    x = x_ref[...]
    u, vn = _gmlp_gate_inputs(x, g_ref[...], win_ref, vg_ref[...])
    v_ref[...] = vn
    nseq = x.shape[0] // steps
    gates = []
    for q in range(steps):
        s = bq_ref[q:q + 1, :]
        for k in range(q + 1):
            s = s + wq_ref[q * steps + k:q * steps + k + 1, :] * vn[k * nseq:(k + 1) * nseq, :]
        gates.append(u[q * nseq:(q + 1) * nseq, :] * s)
    gate = jnp.concatenate(gates, axis=0).astype(BF16)
    o_ref[...] = x + _dot(gate, wout_ref[...])


def _gmlp_prompt(x, g, w_in, v_gain, w_spatial, b_spatial_t, w_out, tm, casts=()):
    m, d = x.shape
    rows = pl.BlockSpec((tm, d), lambda i: (i, 0))
    casts = [_SideCast(cw, layer, m // tm) for cw, layer in casts]
    return pl.pallas_call(
        _gmlp_prompt_kernel,
        grid=(m // tm,),
        in_specs=[rows, _resident((1, d)), _resident(w_in.shape), _resident((1, GMLP_WIDTH)),
                  _resident(w_spatial.shape), _resident(b_spatial_t.shape), _resident(w_out.shape)]
                 + [c.in_spec for c in casts],
        out_specs=[rows] + [c.out_spec for c in casts],
        out_shape=[jax.ShapeDtypeStruct((m, d), F32)] + [c.out_shape for c in casts],
        scratch_shapes=[pltpu.VMEM((tm, GMLP_WIDTH), BF16)],
        compiler_params=_params(("parallel",)),
        name="gmlp_prompt",
    )(x, g.reshape(1, d), w_in, v_gain.reshape(1, GMLP_WIDTH), w_spatial, b_spatial_t, w_out,
      *[c.operand for c in casts])


def _gmlp_sample(x, g, w_in, v_gain, w_q, b_q, w_out, tm, steps):
    m, d = x.shape
    rows = pl.BlockSpec((tm, d), lambda i: (i, 0))
    return pl.pallas_call(
        functools.partial(_gmlp_sample_kernel, steps=steps),
        grid=(m // tm,),
        in_specs=[rows, _resident((1, d)), _resident(w_in.shape), _resident((1, GMLP_WIDTH)),
                  _resident(w_q.shape), _resident(b_q.shape), _resident(w_out.shape)],
        out_specs=(rows, pl.BlockSpec((tm, GMLP_WIDTH), lambda i: (i, 0))),
        out_shape=(jax.ShapeDtypeStruct((m, d), F32), jax.ShapeDtypeStruct((m, GMLP_WIDTH), F32)),
        compiler_params=_params(("parallel",)),
        name="gmlp_sample",
    )(x, g.reshape(1, d), w_in, v_gain.reshape(1, GMLP_WIDTH), w_q, b_q, w_out)


PROMPT_SCAN_STEPS = 64
SAMPLE_TILE_SEQS = 32


def kernel(x_prompt, x_sample, state_ssm_re, state_ssm_im, norm_mix, norm_ffn, norm_final, ssm_w_in, ssm_lambda_re, ssm_lambda_im, ssm_log_dt, ssm_b_re, ssm_b_im, ssm_c_re, ssm_c_im, ssm_d, ssm_w_out, gmlp_w_in, gmlp_v_gain, gmlp_w_spatial, gmlp_b_spatial, gmlp_w_out, ffn_w_gate_up, ffn_w_down):
    bsz, seq, d = x_prompt.shape
    dbsz, dseq, _ = x_sample.shape

    a_re, a_im, bb_re, bb_im = _discretize(ssm_lambda_re[0], ssm_lambda_im[0], ssm_log_dt[0],
                                           ssm_b_re[0], ssm_b_im[0])
    tables = _ssm_tables(a_re, a_im, bb_re, bb_im, ssm_c_re[0], ssm_c_im[0])

    def ffn(x_rows, layer, w_gu, w_dn, ffn_tm, casts=()):
        return _ffn(x_rows, norm_ffn[layer], w_gu, w_dn, norm_final, tm=ffn_tm, th=512, final_norm=layer == 1,
                    casts=casts)

    xp = x_prompt.reshape(bsz * seq, d)
    u, w_out0 = _norm_matmul(xp, norm_mix[0], ssm_w_in, 0, 512, casts=[(ssm_w_out, 0)])
    z, p_re, p_im, w_gu0, g_w_in = _ssm_scan_split(u.reshape(bsz, seq, d), tables, ssm_d[0], PROMPT_SCAN_STEPS,
                                                   casts=[(ffn_w_gate_up, 0), (gmlp_w_in, 0)])
    xp1, w_dn0, g_w_out = _ssm_out(z.reshape(bsz * seq, d), xp, w_out0, 512,
                                   casts=[(ffn_w_down, 0), (gmlp_w_out, 0)])
    xp2, w_gu1, w_dn1 = ffn(xp1, 0, w_gu0, w_dn0, 1024, casts=[(ffn_w_gate_up, 1, 64), (ffn_w_down, 1, 88)])
    xp3, = _gmlp_prompt(xp2, norm_mix[1], g_w_in, gmlp_v_gain[0], gmlp_w_spatial[0],
                        jnp.transpose(gmlp_b_spatial[0]), g_w_out, 512)
    y_prompt = ffn(xp3, 1, w_gu1, w_dn1, 1024)[0].reshape(bsz, seq, d)

    n_bt = dbsz // SAMPLE_TILE_SEQS
    tile_rows = SAMPLE_TILE_SEQS * dseq

    def to_rows(a):
        return jnp.transpose(a.reshape(n_bt, SAMPLE_TILE_SEQS, dseq, d), (0, 2, 1, 3)).reshape(dbsz * dseq, d)

    def from_rows(a):
        return jnp.transpose(a.reshape(n_bt, dseq, SAMPLE_TILE_SEQS, d), (0, 2, 1, 3)).reshape(dbsz, dseq, d)

    xs = to_rows(x_sample)
    h0_re = state_ssm_re[0].reshape(dbsz, SSM_LANES)
    h0_im = state_ssm_im[0].reshape(dbsz, SSM_LANES)
    us, = _norm_matmul(xs, norm_mix[0], ssm_w_in, 0, dbsz * dseq)
    zs, s_re, s_im = _ssm_scan(us, tables, ssm_d[0], h0_re, h0_im, nb=SAMPLE_TILE_SEQS, tt=dseq, n_bt=n_bt, n_tt=1)
    xs1, = _ssm_out(zs, xs, w_out0, dbsz * dseq)
    xs2, = ffn(xs1, 0, w_gu0, w_dn0, dbsz * dseq)
    w_q = jnp.repeat(gmlp_w_spatial[0][:, :dseq, :dseq].reshape(GMLP_HEADS, dseq * dseq).T, GMLP_HEAD_DIM, axis=1)
    b_q = jnp.repeat(gmlp_b_spatial[0][:, :dseq].T, GMLP_HEAD_DIM, axis=1)
    xs3, v_rows = _gmlp_sample(xs2, norm_mix[1], g_w_in, gmlp_v_gain[0], w_q, b_q, g_w_out, tile_rows, dseq)
    y_sample = from_rows(ffn(xs3, 1, w_gu1, w_dn1, dbsz * dseq)[0])

    state_shape = (1, -1, SSM_GROUPS, SSM_STATE)
    return (y_prompt, y_sample,
            p_re.reshape(state_shape), p_im.reshape(state_shape),
            s_re.reshape(state_shape), s_im.reshape(state_shape),
            from_rows(v_rows)[None])
```

```python
import functools

import jax
import jax.numpy as jnp
from jax import lax
from jax.experimental import pallas as pl
from jax.experimental.pallas import tpu as pltpu

D_MODEL = 2048
SSM_GROUPS = 128
SSM_GROUP = 16
SSM_STATE = 64
SSM_LANES = SSM_GROUPS * SSM_STATE
GMLP_WIDTH = D_MODEL
GMLP_HEADS = 16
GMLP_HEAD_DIM = GMLP_WIDTH // GMLP_HEADS
CHUNK = 128
FFN_HIDDEN = 5632
EPS = 1e-6

LANES = 128
SUBLANES = 8
MXU_DIM = 256
VMEM_LIMIT = 56 * 1024 * 1024
BIG_VMEM_LIMIT = 60 * 1024 * 1024

PAIRS = SSM_GROUPS // 2
QUAD_K = 4 * 2 * SSM_GROUP
QUADS = PAIRS // 4
HALF_SLABS = PAIRS // 2
OCTS = D_MODEL // MXU_DIM
OCT_K = SSM_LANES // OCTS

BF16 = jnp.bfloat16
F32 = jnp.float32


def _resident(shape):
    zeros = (0,) * len(shape)
    return pl.BlockSpec(shape, lambda *_: zeros, pipeline_mode=pl.Buffered(1))


def _params(semantics):
    return pltpu.CompilerParams(dimension_semantics=semantics, vmem_limit_bytes=VMEM_LIMIT)


def _rms(x, g):
    ms = jnp.mean(x * x, axis=-1, keepdims=True)
    return (x * lax.rsqrt(ms + EPS)) * g


def _dot(a, b):
    return jnp.dot(a, b, preferred_element_type=F32)


BF16_ROWS = 16


class _SideCast:
    def __init__(self, w, layer, n_blocks, step_of=lambda i: i, col_blocks=None):
        _, r, c = w.shape
        assert r % (n_blocks * BF16_ROWS) == 0

        def block(*idx):
            return jnp.minimum(step_of(*idx), n_blocks - 1)

        self.operand = w
        self.in_spec = pl.BlockSpec((None, r // n_blocks, c), lambda *idx: (layer, block(*idx), 0))
        if col_blocks is None:
            self.out_spec = pl.BlockSpec((r // n_blocks, c), lambda *idx: (block(*idx), 0))
            self.out_shape = jax.ShapeDtypeStruct((r, c), BF16)
        else:
            assert c % (col_blocks * LANES) == 0
            self.out_spec = pl.BlockSpec((col_blocks, r // n_blocks, c // col_blocks),
                                         lambda *idx: (0, block(*idx), 0))
            self.out_shape = jax.ShapeDtypeStruct((col_blocks, r, c // col_blocks), BF16)


def _cast_blocks(src_refs, dst_refs):
    for src, dst in zip(src_refs, dst_refs):
        if len(dst.shape) == 2:
            dst[...] = src[...].astype(BF16)
        else:
            width = dst.shape[2]
            for j in range(dst.shape[0]):
                dst[j] = src[:, j * width:(j + 1) * width].astype(BF16)


def _discretize_kernel(lr_ref, li_ref, ldt_ref, br_ref, bi_ref, are_ref, aim_ref, bbr_ref, bbi_ref):
    lr = lr_ref[...]
    li = li_ref[...]
    dt = jnp.exp(ldt_ref[...])
    mag = jnp.exp(lr * dt)
    a_re = mag * jnp.cos(li * dt)
    a_im = mag * jnp.sin(li * dt)
    den = lr * lr + li * li
    nr = a_re - 1.0
    ni = a_im
    q_re = (nr * lr + ni * li) / den
    q_im = (ni * lr - nr * li) / den
    br = br_ref[...]
    bi = bi_ref[...]
    are_ref[...] = a_re
    aim_ref[...] = a_im
    bbr_ref[...] = q_re * br - q_im * bi
    bbi_ref[...] = q_re * bi + q_im * br


def _discretize(lambda_re, lambda_im, log_dt, b_re, b_im):
    g, p, c = b_re.shape
    full = (g, c, p)
    flat = (g * c * p // LANES, LANES)
    args = [
        jnp.broadcast_to(lambda_re[:, None, :], full).reshape(flat),
        jnp.broadcast_to(lambda_im[:, None, :], full).reshape(flat),
        jnp.broadcast_to(log_dt[:, None, None], full).reshape(flat),
        jnp.transpose(b_re, (0, 2, 1)).reshape(flat),
        jnp.transpose(b_im, (0, 2, 1)).reshape(flat),
    ]
    out = jax.ShapeDtypeStruct(flat, F32)
    a_re, a_im, bb_re, bb_im = pl.pallas_call(
        _discretize_kernel, out_shape=(out, out, out, out), name="ssm_discretize")(*args)
    a_re = a_re.reshape(full)[:, 0, :]
    a_im = a_im.reshape(full)[:, 0, :]
    return a_re, a_im, bb_re.reshape(full), bb_im.reshape(full)


def _ssm_tables(a_re, a_im, bb_re, bb_im, c_re, c_im):
    c, p = SSM_GROUP, SSM_STATE
    row = lax.broadcasted_iota(jnp.int32, (2 * QUAD_K, 2 * MXU_DIM), 0)
    col = lax.broadcasted_iota(jnp.int32, (2 * QUAD_K, 2 * MXU_DIM), 1)
    row_h, row_g8 = row // QUAD_K, (row % QUAD_K) // c
    col_s, col_g2 = col // MXU_DIM, (col % LANES) // p
    hit = row_g8 == 4 * row_h + 2 * col_s + col_g2
    re = bb_re.reshape(QUADS, QUAD_K, p)
    im = bb_im.reshape(QUADS, QUAD_K, p)
    tiled = jnp.concatenate([re, re, im, im] * 2, axis=-1)
    w_b = jnp.where(hit[None], jnp.concatenate([tiled, tiled], axis=1), 0.0).astype(BF16)

    groups = MXU_DIM // c
    row_g = lax.broadcasted_iota(jnp.int32, (OCT_K, MXU_DIM), 0) // p
    col_g = lax.broadcasted_iota(jnp.int32, (OCT_K, MXU_DIM), 1) // c

    def out_blocks(cm):
        cm = jnp.transpose(cm.reshape(OCTS, groups, c, p), (0, 1, 3, 2)).reshape(OCTS, OCT_K, c)
        return jnp.where((row_g == col_g)[None], jnp.tile(cm, (1, 1, groups)), 0.0).astype(BF16)

    a_re_t = jnp.broadcast_to(a_re.reshape(1, SSM_LANES), (SUBLANES, SSM_LANES))
    a_im_t = jnp.broadcast_to(a_im.reshape(1, SSM_LANES), (SUBLANES, SSM_LANES))
    return (w_b, a_re_t, a_im_t, out_blocks(c_re), out_blocks(-c_im))


def _norm_matmul_kernel(x_ref, g_ref, w_ref, *rest):
    n_cast = (len(rest) - 2) // 2
    o_ref, wb_ref = rest[n_cast], rest[-1]
    _cast_blocks(rest[:n_cast], rest[n_cast + 1:-1])

    @pl.when(pl.program_id(0) == 0)
    def _():
        wb_ref[...] = w_ref[...].astype(BF16)

    half = x_ref.shape[0] // 2
    for rows in (slice(0, half), slice(half, 2 * half)):
        hn = _rms(x_ref[rows, :], g_ref[...]).astype(BF16)
        o_ref[rows, :] = _dot(hn, wb_ref[...])


def _norm_matmul(x, g, w, layer, tm, casts=()):
    m, d = x.shape
    n = w.shape[2]
    casts = [_SideCast(cw, cl, m // tm) for cw, cl in casts]
    return pl.pallas_call(
        _norm_matmul_kernel,
        grid=(m // tm,),
        in_specs=[pl.BlockSpec((tm, d), lambda i: (i, 0)), _resident((1, d)),
                  pl.BlockSpec((None, d, n), lambda i: (layer, 0, 0), pipeline_mode=pl.Buffered(1))]
                 + [c.in_spec for c in casts],
        out_specs=[pl.BlockSpec((tm, n), lambda i: (i, 0))] + [c.out_spec for c in casts],
        out_shape=[jax.ShapeDtypeStruct((m, n), F32)] + [c.out_shape for c in casts],
        scratch_shapes=[pltpu.VMEM((d, n), BF16)],
        compiler_params=_params(("arbitrary",)),
        name="ssm_norm_in_proj",
    )(x, g.reshape(1, d), w, *[c.operand for c in casts])


SCAN_COLS = 1024


def _ssm_scan_kernel(u_ref, wb_ref, are_ref, aim_ref, wcr_ref, wci_ref, dsk_ref, h0r_ref, h0i_ref,
                     z_ref, sr_ref, si_ref, bur_ref, bui_ref, *, nb, tt):
    @pl.when(pl.program_id(1) == 0)
    def _():
        sr_ref[...] = h0r_ref[...]
        si_ref[...] = h0i_ref[...]

    u = u_ref[...]
    ub = u.astype(BF16)
    for q in range(QUADS):
        lhs = ub[:, QUAD_K * q:QUAD_K * (q + 1)]
        for h in range(2):
            res = _dot(lhs, wb_ref[q, QUAD_K * h:QUAD_K * (h + 1), :])
            for s in range(2):
                j = 4 * q + 2 * h + s
                bur_ref[:, LANES * j:LANES * (j + 1)] = res[:, MXU_DIM * s:MXU_DIM * s + LANES]
                bui_ref[:, LANES * j:LANES * (j + 1)] = res[:, MXU_DIM * s + LANES:MXU_DIM * (s + 1)]

    for cb in range(SSM_LANES // SCAN_COLS):
        cols = slice(cb * SCAN_COLS, (cb + 1) * SCAN_COLS)
        ar = are_ref[:, cols]
        ai = aim_ref[:, cols]
        for sg in range(nb // SUBLANES):
            seqs = slice(sg * SUBLANES, (sg + 1) * SUBLANES)
            sr = sr_ref[seqs, cols]
            si = si_ref[seqs, cols]
            for t in range(tt):
                rows = slice(t * nb + sg * SUBLANES, t * nb + (sg + 1) * SUBLANES)
                nr = ar * sr - ai * si + bur_ref[rows, cols]
                ni = ar * si + ai * sr + bui_ref[rows, cols]
                sr, si = nr, ni
                bur_ref[rows, cols] = sr
                bui_ref[rows, cols] = si
            sr_ref[seqs, cols] = sr
            si_ref[seqs, cols] = si

    for o in range(OCTS):
        kk = slice(o * OCT_K, (o + 1) * OCT_K)
        nn = slice(o * MXU_DIM, (o + 1) * MXU_DIM)
        y = _dot(bur_ref[:, kk].astype(BF16), wcr_ref[o]) + _dot(bui_ref[:, kk].astype(BF16), wci_ref[o])
        y = y + dsk_ref[:, nn] * u[:, nn]
        z_ref[:, nn] = jax.nn.gelu(y).astype(BF16)


def _ssm_scan(u, tables, d_skip, h0_re, h0_im, *, nb, tt, n_bt, n_tt):
    w_b, a_re_t, a_im_t, w_cr, w_ci = tables
    m, d = u.shape
    r = nb * tt
    assert m == r * n_bt * n_tt and nb % SUBLANES == 0
    rows = pl.BlockSpec((r, d), lambda b, t: (b * n_tt + t, 0))
    state = pl.BlockSpec((nb, SSM_LANES), lambda b, t: (b, 0))
    st_shape = jax.ShapeDtypeStruct((nb * n_bt, SSM_LANES), F32)
    return pl.pallas_call(
        functools.partial(_ssm_scan_kernel, nb=nb, tt=tt),
        grid=(n_bt, n_tt),
        in_specs=[rows, _resident(w_b.shape), _resident(a_re_t.shape), _resident(a_im_t.shape),
                  _resident(w_cr.shape), _resident(w_ci.shape), _resident((1, d)), state, state],
        out_specs=(rows, state, state),
        out_shape=(jax.ShapeDtypeStruct((m, d), BF16), st_shape, st_shape),
        scratch_shapes=[pltpu.VMEM((r, SSM_LANES), F32), pltpu.VMEM((r, SSM_LANES), F32)],
        compiler_params=_params(("parallel", "arbitrary")),
        name="ssm_scan",
    )(u, w_b, a_re_t, a_im_t, w_cr, w_ci, d_skip.reshape(1, d), h0_re, h0_im)


SCAN_SLABS = 8


def _ssm_scan_split_kernel(u_ref, wb_ref, are_ref, aim_ref, wcr_ref, wci_ref, dsk_ref, *rest, nb, tt):
    n_cast = (len(rest) - 7) // 2
    z_ref, sr_ref, si_ref = rest[n_cast:n_cast + 3]
    lhs_ref, bur_ref, bui_ref, y_ref = rest[-4:]
    _cast_blocks(rest[:n_cast], rest[n_cast + 3:-4])
    r = nb * tt
    i = pl.program_id(0)

    @pl.when(i == 0)
    def _():
        lhs_ref[...] = jnp.zeros_like(lhs_ref)
        sr_ref[...] = jnp.zeros_like(sr_ref)
        si_ref[...] = jnp.zeros_like(si_ref)

    def project_in(q):
        for b in range(nb):
            blk = u_ref[b, :, LANES * q:LANES * (q + 1)]
            for h in range(2):
                lhs_ref[q, h, pl.ds(2 * b + h, tt, stride=2 * nb), :] = blk
        lhs = jnp.concatenate([lhs_ref[q, 0], lhs_ref[q, 1]], axis=1).astype(BF16)
        res = _dot(lhs, wb_ref[q])
        for s in range(2):
            bur_ref[2 * q + s] = res[:, MXU_DIM * s:MXU_DIM * s + LANES]
            bui_ref[2 * q + s] = res[:, MXU_DIM * s + LANES:MXU_DIM * (s + 1)]

    def recur(slabs):
        ar = [are_ref[:, LANES * k:LANES * (k + 1)] for k in slabs]
        ai = [aim_ref[:, LANES * k:LANES * (k + 1)] for k in slabs]
        sr = [sr_ref[:, LANES * k:LANES * (k + 1)] for k in slabs]
        si = [si_ref[:, LANES * k:LANES * (k + 1)] for k in slabs]
        for t in range(tt):
            rows = slice(SUBLANES * t, SUBLANES * (t + 1))
            for n, k in enumerate(slabs):
                nr = ar[n] * sr[n] - ai[n] * si[n] + bur_ref[k, rows, :]
                ni = ar[n] * si[n] + ai[n] * sr[n] + bui_ref[k, rows, :]
                sr[n], si[n] = nr, ni
                bur_ref[k, rows, :] = nr
                bui_ref[k, rows, :] = ni
        for n, k in enumerate(slabs):
            sr_ref[:, LANES * k:LANES * (k + 1)] = sr[n]
            si_ref[:, LANES * k:LANES * (k + 1)] = si[n]

    def project_out(o):
        order = [(2 * (2 * o + ql) + s, h) for ql in range(2) for h in range(2) for s in range(2)]
        lre = jnp.concatenate([bur_ref[k, pl.ds(h, r, stride=2), :] for k, h in order], axis=1).astype(BF16)
        lim = jnp.concatenate([bui_ref[k, pl.ds(h, r, stride=2), :] for k, h in order], axis=1).astype(BF16)
        y = _dot(lre, wcr_ref[o]) + _dot(lim, wci_ref[o])
        for half in range(2):
            y_ref[half] = y[:, LANES * half:LANES * (half + 1)]
        for b in range(nb):
            for half in range(2):
                cols = slice(MXU_DIM * o + LANES * half, MXU_DIM * o + LANES * (half + 1))
                yb = y_ref[half, pl.ds(b, tt, stride=nb), :] + dsk_ref[:, cols] * u_ref[b, :, cols]
                z_ref[b, :, cols] = jax.nn.gelu(yb).astype(BF16)

    for q in range(QUADS):
        project_in(q)
    for k0 in range(0, HALF_SLABS, SCAN_SLABS):
        recur(range(k0, k0 + SCAN_SLABS))
    for o in range(OCTS):
        project_out(o)


def _ssm_scan_split(u, tables, d_skip, tt, casts):
    w_b, a_re_t, a_im_t, w_cr, w_ci = tables
    nb, seq, d = u.shape
    assert 2 * nb == SUBLANES and seq % tt == 0
    casts = [_SideCast(cw, layer, seq // tt, col_blocks=cb) for cw, layer, cb in casts]

    def split_lanes(a):
        halves = jnp.transpose(a[0].reshape(QUADS, 2, 2 * LANES), (1, 0, 2)).reshape(2, HALF_SLABS * LANES)
        return jnp.tile(halves, (nb, 1))

    rows = pl.BlockSpec((nb, tt, d), lambda i: (0, i, 0))
    st_shape = jax.ShapeDtypeStruct((SUBLANES, HALF_SLABS * LANES), F32)
    r = nb * tt
    state = pl.BlockSpec(st_shape.shape, lambda i: (0, 0))
    z, s_re, s_im, *w_cast = pl.pallas_call(
        functools.partial(_ssm_scan_split_kernel, nb=nb, tt=tt),
        grid=(seq // tt,),
        in_specs=[rows, _resident(w_b.shape), _resident(st_shape.shape), _resident(st_shape.shape),
                  _resident(w_cr.shape), _resident(w_ci.shape), _resident((1, d))] + [c.in_spec for c in casts],
        out_specs=[rows, state, state] + [c.out_spec for c in casts],
        out_shape=[jax.ShapeDtypeStruct((nb, seq, d), BF16), st_shape, st_shape] + [c.out_shape for c in casts],
        scratch_shapes=[pltpu.VMEM((QUADS, 2, 2 * r, LANES), F32),
                        pltpu.VMEM((HALF_SLABS, 2 * r, LANES), F32),
                        pltpu.VMEM((HALF_SLABS, 2 * r, LANES), F32),
                        pltpu.VMEM((2, r, LANES), F32)],
        compiler_params=pltpu.CompilerParams(dimension_semantics=("arbitrary",), vmem_limit_bytes=BIG_VMEM_LIMIT),
        name="ssm_scan_split",
    )(u, w_b, split_lanes(a_re_t), split_lanes(a_im_t), w_cr, w_ci, d_skip.reshape(1, d),
      *[c.operand for c in casts])

    def join(s):
        return jnp.transpose(s.reshape(nb, 2, QUADS, 2 * LANES), (0, 2, 1, 3)).reshape(nb, SSM_LANES)

    return (z, join(s_re), join(s_im), *w_cast)


SSM_OUT_COLS = 512


def _ssm_out_kernel(z_ref, x_ref, w_ref, *rest):
    o_ref = rest[len(rest) // 2]
    _cast_blocks(rest[:len(rest) // 2], rest[len(rest) // 2 + 1:])
    z = z_ref[...]
    for c in range(D_MODEL // SSM_OUT_COLS):
        cols = slice(c * SSM_OUT_COLS, (c + 1) * SSM_OUT_COLS)
        gate_cols = slice(D_MODEL + c * SSM_OUT_COLS, D_MODEL + (c + 1) * SSM_OUT_COLS)
        val = _dot(z, w_ref[:, cols])
        gate = _dot(z, w_ref[:, gate_cols])
        o_ref[:, cols] = x_ref[:, cols] + val * jax.nn.sigmoid(gate)


def _ssm_out(z, x, w, tm, casts=()):
    m, d = x.shape
    rows = pl.BlockSpec((tm, d), lambda i: (i, 0))
    casts = [_SideCast(cw, layer, m // tm) for cw, layer in casts]
    return pl.pallas_call(
        _ssm_out_kernel,
        grid=(m // tm,),
        in_specs=[rows, rows, _resident(w.shape)] + [c.in_spec for c in casts],
        out_specs=[rows] + [c.out_spec for c in casts],
        out_shape=[jax.ShapeDtypeStruct((m, d), F32)] + [c.out_shape for c in casts],
        compiler_params=_params(("parallel",)),
        name="ssm_out_glu",
    )(z, x, w, *[c.operand for c in casts])


def _ffn_kernel(x_ref, g_ref, wg_ref, wu_ref, wd_ref, gf_ref, *rest, final_norm):
    n_cast = (len(rest) - 2) // 2
    o_ref, hn_ref = rest[n_cast], rest[-1]
    _cast_blocks(rest[:n_cast], rest[n_cast + 1:-1])
    h = pl.program_id(1)

    @pl.when(h == 0)
    def _():
        x = x_ref[...]
        hn_ref[...] = _rms(x, g_ref[...]).astype(BF16)
        o_ref[...] = x

    hn = hn_ref[...]
    act = (jax.nn.silu(_dot(hn, wg_ref[...])) * _dot(hn, wu_ref[...])).astype(BF16)
    o_ref[...] += _dot(act, wd_ref[...])

    if final_norm:
        @pl.when(h == pl.num_programs(1) - 1)
        def _():
            o_ref[...] = _rms(o_ref[...], gf_ref[...])


def _ffn(x, g, w_gate_up, w_down, g_final, *, tm, final_norm, casts=()):
    m, d = x.shape
    th = w_gate_up.shape[2]
    n_h = FFN_HIDDEN // th
    assert w_gate_up.shape == (2 * n_h, d, th)
    rows = pl.BlockSpec((tm, d), lambda i, h: (i, 0))
    assert all(nb <= (m // tm) * n_h for _, _, nb, _ in casts)
    casts = [_SideCast(cw, layer, nb, lambda i, h: i * n_h + h, col_blocks=cb) for cw, layer, nb, cb in casts]
    return pl.pallas_call(
        functools.partial(_ffn_kernel, final_norm=final_norm),
        grid=(m // tm, n_h),
        in_specs=[rows,
                  _resident((1, d)),
                  pl.BlockSpec((None, d, th), lambda i, h: (h, 0, 0)),
                  pl.BlockSpec((None, d, th), lambda i, h: (n_h + h, 0, 0)),
                  pl.BlockSpec((th, d), lambda i, h: (h, 0)),
                  _resident((1, d))] + [c.in_spec for c in casts],
        out_specs=[rows] + [c.out_spec for c in casts],
        out_shape=[jax.ShapeDtypeStruct((m, d), F32)] + [c.out_shape for c in casts],
        scratch_shapes=[pltpu.VMEM((tm, d), BF16)],
        compiler_params=pltpu.CompilerParams(dimension_semantics=("arbitrary", "arbitrary"),
                                             vmem_limit_bytes=BIG_VMEM_LIMIT),
        name="ffn_swiglu",
    )(x, g.reshape(1, d), w_gate_up, w_gate_up, w_down, g_final.reshape(1, d), *[c.operand for c in casts])


def _gmlp_gate_inputs(x, g, win_ref, v_gain):
    hn = _rms(x, g).astype(BF16)
    v = jax.nn.gelu(_dot(hn, win_ref[:, GMLP_WIDTH:]))
    vc = v - jnp.mean(v, axis=-1, keepdims=True)
    vn = (vc * lax.rsqrt(jnp.mean(vc * vc, axis=-1, keepdims=True) + EPS)) * v_gain
    u = jax.nn.gelu(_dot(hn, win_ref[:, :GMLP_WIDTH]))
    return u, vn


def _gmlp_prompt_kernel(x_ref, g_ref, win_ref, vg_ref, ws_ref, bs_ref, wout_ref, *rest):
    n_cast = (len(rest) - 2) // 2
    o_ref, gate_ref = rest[n_cast], rest[-1]
    _cast_blocks(rest[:n_cast], rest[n_cast + 1:-1])
    x = x_ref[...]
    u, vn = _gmlp_gate_inputs(x, g_ref[...], win_ref, vg_ref[...])
    vb = vn.astype(BF16)
    q_idx = lax.broadcasted_iota(jnp.int32, (CHUNK, CHUNK), 0)
    k_idx = lax.broadcasted_iota(jnp.int32, (CHUNK, CHUNK), 1)
    causal = k_idx <= q_idx
    for h in range(GMLP_HEADS):
        cols = slice(h * GMLP_HEAD_DIM, (h + 1) * GMLP_HEAD_DIM)
        ws = jnp.where(causal, ws_ref[h], 0.0).astype(BF16)
        bias = bs_ref[:, h:h + 1]
        for c in range(0, x.shape[0] // CHUNK, 2):
            lo = slice(c * CHUNK, (c + 1) * CHUNK)
            hi = slice((c + 1) * CHUNK, (c + 2) * CHUNK)
            s = _dot(ws, jnp.concatenate([vb[lo, cols], vb[hi, cols]], axis=1)) + bias
            gate_ref[lo, cols] = (u[lo, cols] * s[:, :GMLP_HEAD_DIM]).astype(BF16)
            gate_ref[hi, cols] = (u[hi, cols] * s[:, GMLP_HEAD_DIM:]).astype(BF16)
    o_ref[...] = x + _dot(gate_ref[...], wout_ref[...])


def _gmlp_sample_kernel(x_ref, g_ref, win_ref, vg_ref, wq_ref, bq_ref, wout_ref, o_ref, v_ref, *, steps):
    x = x_ref[...]
    u, vn = _gmlp_gate_inputs(x, g_ref[...], win_ref, vg_ref[...])
    v_ref[...] = vn
    nseq = x.shape[0] // steps
    gates = []
    for q in range(steps):
        s = bq_ref[q:q + 1, :]
        for k in range(q + 1):
            s = s + wq_ref[q * steps + k:q * steps + k + 1, :] * vn[k * nseq:(k + 1) * nseq, :]
        gates.append(u[q * nseq:(q + 1) * nseq, :] * s)
    gate = jnp.concatenate(gates, axis=0).astype(BF16)
    o_ref[...] = x + _dot(gate, wout_ref[...])


def _gmlp_prompt(x, g, w_in, v_gain, w_spatial, b_spatial_t, w_out, tm, casts=()):
    m, d = x.shape
    rows = pl.BlockSpec((tm, d), lambda i: (i, 0))
    casts = [_SideCast(cw, layer, m // tm) for cw, layer in casts]
    return pl.pallas_call(
        _gmlp_prompt_kernel,
        grid=(m // tm,),
        in_specs=[rows, _resident((1, d)), _resident(w_in.shape), _resident((1, GMLP_WIDTH)),
                  _resident(w_spatial.shape), _resident(b_spatial_t.shape), _resident(w_out.shape)]
                 + [c.in_spec for c in casts],
        out_specs=[rows] + [c.out_spec for c in casts],
        out_shape=[jax.ShapeDtypeStruct((m, d), F32)] + [c.out_shape for c in casts],
        scratch_shapes=[pltpu.VMEM((tm, GMLP_WIDTH), BF16)],
        compiler_params=_params(("parallel",)),
        name="gmlp_prompt",
    )(x, g.reshape(1, d), w_in, v_gain.reshape(1, GMLP_WIDTH), w_spatial, b_spatial_t, w_out,
      *[c.operand for c in casts])


def _gmlp_sample(x, g, w_in, v_gain, w_q, b_q, w_out, tm, steps):
    m, d = x.shape
    rows = pl.BlockSpec((tm, d), lambda i: (i, 0))
    return pl.pallas_call(
        functools.partial(_gmlp_sample_kernel, steps=steps),
        grid=(m // tm,),
        in_specs=[rows, _resident((1, d)), _resident(w_in.shape), _resident((1, GMLP_WIDTH)),
                  _resident(w_q.shape), _resident(b_q.shape), _resident(w_out.shape)],
        out_specs=(rows, pl.BlockSpec((tm, GMLP_WIDTH), lambda i: (i, 0))),
        out_shape=(jax.ShapeDtypeStruct((m, d), F32), jax.ShapeDtypeStruct((m, GMLP_WIDTH), F32)),
        compiler_params=_params(("parallel",)),
        name="gmlp_sample",
    )(x, g.reshape(1, d), w_in, v_gain.reshape(1, GMLP_WIDTH), w_q, b_q, w_out)


FFN_HIDDEN_TILE = 512
PROMPT_SCAN_STEPS = 64
SAMPLE_TILE_SEQS = 32


def kernel(x_prompt, x_sample, state_ssm_re, state_ssm_im, norm_mix, norm_ffn, norm_final, ssm_w_in, ssm_lambda_re, ssm_lambda_im, ssm_log_dt, ssm_b_re, ssm_b_im, ssm_c_re, ssm_c_im, ssm_d, ssm_w_out, gmlp_w_in, gmlp_v_gain, gmlp_w_spatial, gmlp_b_spatial, gmlp_w_out, ffn_w_gate_up, ffn_w_down):
    bsz, seq, d = x_prompt.shape
    dbsz, dseq, _ = x_sample.shape

    a_re, a_im, bb_re, bb_im = _discretize(ssm_lambda_re[0], ssm_lambda_im[0], ssm_log_dt[0],
                                           ssm_b_re[0], ssm_b_im[0])
    tables = _ssm_tables(a_re, a_im, bb_re, bb_im, ssm_c_re[0], ssm_c_im[0])

    def ffn(x_rows, layer, w_gu, w_dn, ffn_tm, casts=()):
        return _ffn(x_rows, norm_ffn[layer], w_gu, w_dn, norm_final, tm=ffn_tm, final_norm=layer == 1, casts=casts)

    gu_blocks = 2 * FFN_HIDDEN // FFN_HIDDEN_TILE

    xp = x_prompt.reshape(bsz * seq, d)
    u, w_out0 = _norm_matmul(xp, norm_mix[0], ssm_w_in, 0, 512, casts=[(ssm_w_out, 0)])
    z, p_re, p_im, w_gu0, g_w_in = _ssm_scan_split(u.reshape(bsz, seq, d), tables, ssm_d[0], PROMPT_SCAN_STEPS,
                                                   casts=[(ffn_w_gate_up, 0, gu_blocks), (gmlp_w_in, 0, None)])
    xp1, w_dn0, g_w_out = _ssm_out(z.reshape(bsz * seq, d), xp, w_out0, 512,
                                   casts=[(ffn_w_down, 0), (gmlp_w_out, 0)])
    xp2, w_gu1, w_dn1 = ffn(xp1, 0, w_gu0, w_dn0, 1024,
                            casts=[(ffn_w_gate_up, 1, 64, gu_blocks), (ffn_w_down, 1, 88, None)])
    xp3, = _gmlp_prompt(xp2, norm_mix[1], g_w_in, gmlp_v_gain[0], gmlp_w_spatial[0],
                        jnp.transpose(gmlp_b_spatial[0]), g_w_out, 512)
    y_prompt = ffn(xp3, 1, w_gu1, w_dn1, 1024)[0].reshape(bsz, seq, d)

    n_bt = dbsz // SAMPLE_TILE_SEQS
    tile_rows = SAMPLE_TILE_SEQS * dseq

    def to_rows(a):
        return jnp.transpose(a.reshape(n_bt, SAMPLE_TILE_SEQS, dseq, d), (0, 2, 1, 3)).reshape(dbsz * dseq, d)

    def from_rows(a):
        return jnp.transpose(a.reshape(n_bt, dseq, SAMPLE_TILE_SEQS, d), (0, 2, 1, 3)).reshape(dbsz, dseq, d)

    xs = to_rows(x_sample)
    h0_re = state_ssm_re[0].reshape(dbsz, SSM_LANES)
    h0_im = state_ssm_im[0].reshape(dbsz, SSM_LANES)
    us, = _norm_matmul(xs, norm_mix[0], ssm_w_in, 0, dbsz * dseq)
    zs, s_re, s_im = _ssm_scan(us, tables, ssm_d[0], h0_re, h0_im, nb=SAMPLE_TILE_SEQS, tt=dseq, n_bt=n_bt, n_tt=1)
    xs1, = _ssm_out(zs, xs, w_out0, tile_rows)
    xs2, = ffn(xs1, 0, w_gu0, w_dn0, dbsz * dseq)
    w_q = jnp.repeat(gmlp_w_spatial[0][:, :dseq, :dseq].reshape(GMLP_HEADS, dseq * dseq).T, GMLP_HEAD_DIM, axis=1)
    b_q = jnp.repeat(gmlp_b_spatial[0][:, :dseq].T, GMLP_HEAD_DIM, axis=1)
    xs3, v_rows = _gmlp_sample(xs2, norm_mix[1], g_w_in, gmlp_v_gain[0], w_q, b_q, g_w_out, tile_rows, dseq)
    y_sample = from_rows(ffn(xs3, 1, w_gu1, w_dn1, dbsz * dseq)[0])

    state_shape = (1, -1, SSM_GROUPS, SSM_STATE)
    return (y_prompt, y_sample,
            p_re.reshape(state_shape), p_im.reshape(state_shape),
            s_re.reshape(state_shape), s_im.reshape(state_shape),
            from_rows(v_rows)[None])
```

```python
import functools

import jax
import jax.numpy as jnp
from jax import lax
from jax.experimental import pallas as pl
from jax.experimental.pallas import tpu as pltpu

D_MODEL = 2048
SSM_GROUPS = 128
SSM_GROUP = 16
SSM_STATE = 64
SSM_LANES = SSM_GROUPS * SSM_STATE
GMLP_WIDTH = D_MODEL
GMLP_HEADS = 16
GMLP_HEAD_DIM = GMLP_WIDTH // GMLP_HEADS
CHUNK = 128
FFN_HIDDEN = 5632
EPS = 1e-6

LANES = 128
SUBLANES = 8
MXU_DIM = 256
VMEM_LIMIT = 56 * 1024 * 1024
BIG_VMEM_LIMIT = 60 * 1024 * 1024

PAIRS = SSM_GROUPS // 2
QUAD_K = 4 * 2 * SSM_GROUP
QUADS = PAIRS // 4
HALF_SLABS = PAIRS // 2
OCTS = D_MODEL // MXU_DIM
OCT_K = SSM_LANES // OCTS

BF16 = jnp.bfloat16
F32 = jnp.float32


def _resident(shape):
    zeros = (0,) * len(shape)
    return pl.BlockSpec(shape, lambda *_: zeros, pipeline_mode=pl.Buffered(1))


def _params(semantics):
    return pltpu.CompilerParams(dimension_semantics=semantics, vmem_limit_bytes=VMEM_LIMIT)


def _rms(x, g):
    ms = jnp.mean(x * x, axis=-1, keepdims=True)
    return (x * lax.rsqrt(ms + EPS)) * g


def _dot(a, b):
    return jnp.dot(a, b, preferred_element_type=F32)


BF16_ROWS = 16


class _SideCast:
    def __init__(self, w, layer, n_blocks, step_of=lambda i: i):
        _, r, c = w.shape
        assert r % (n_blocks * BF16_ROWS) == 0

        def block(*idx):
            return jnp.minimum(step_of(*idx), n_blocks - 1)

        self.operand = w
        self.in_spec = pl.BlockSpec((None, r // n_blocks, c), lambda *idx: (layer, block(*idx), 0))
        self.out_spec = pl.BlockSpec((r // n_blocks, c), lambda *idx: (block(*idx), 0))
        self.out_shape = jax.ShapeDtypeStruct((r, c), BF16)


def _cast_blocks(src_refs, dst_refs):
    for src, dst in zip(src_refs, dst_refs):
        dst[...] = src[...].astype(BF16)


def _slice_count(rows, steps):
    return max(n for n in range(1, steps + 1) if rows % (n * BF16_ROWS) == 0)


def _discretize_kernel(lr_ref, li_ref, ldt_ref, br_ref, bi_ref, are_ref, aim_ref, bbr_ref, bbi_ref):
    lr = lr_ref[...]
    li = li_ref[...]
    dt = jnp.exp(ldt_ref[...])
    mag = jnp.exp(lr * dt)
    a_re = mag * jnp.cos(li * dt)
    a_im = mag * jnp.sin(li * dt)
    den = lr * lr + li * li
    nr = a_re - 1.0
    ni = a_im
    q_re = (nr * lr + ni * li) / den
    q_im = (ni * lr - nr * li) / den
    br = br_ref[...]
    bi = bi_ref[...]
    are_ref[...] = a_re
    aim_ref[...] = a_im
    bbr_ref[...] = q_re * br - q_im * bi
    bbi_ref[...] = q_re * bi + q_im * br


def _discretize(lambda_re, lambda_im, log_dt, b_re, b_im):
    g, p, c = b_re.shape
    full = (g, c, p)
    flat = (g * c * p // LANES, LANES)
    args = [
        jnp.broadcast_to(lambda_re[:, None, :], full).reshape(flat),
        jnp.broadcast_to(lambda_im[:, None, :], full).reshape(flat),
        jnp.broadcast_to(log_dt[:, None, None], full).reshape(flat),
        jnp.transpose(b_re, (0, 2, 1)).reshape(flat),
        jnp.transpose(b_im, (0, 2, 1)).reshape(flat),
    ]
    out = jax.ShapeDtypeStruct(flat, F32)
    a_re, a_im, bb_re, bb_im = pl.pallas_call(
        _discretize_kernel, out_shape=(out, out, out, out), name="ssm_discretize")(*args)
    a_re = a_re.reshape(full)[:, 0, :]
    a_im = a_im.reshape(full)[:, 0, :]
    return a_re, a_im, bb_re.reshape(full), bb_im.reshape(full)


def _ssm_tables(a_re, a_im, bb_re, bb_im, c_re, c_im):
    c, p = SSM_GROUP, SSM_STATE
    row = lax.broadcasted_iota(jnp.int32, (2 * QUAD_K, 2 * MXU_DIM), 0)
    col = lax.broadcasted_iota(jnp.int32, (2 * QUAD_K, 2 * MXU_DIM), 1)
    row_h, row_g8 = row // QUAD_K, (row % QUAD_K) // c
    col_s, col_g2 = col // MXU_DIM, (col % LANES) // p
    hit = row_g8 == 4 * row_h + 2 * col_s + col_g2
    re = bb_re.reshape(QUADS, QUAD_K, p)
    im = bb_im.reshape(QUADS, QUAD_K, p)
    tiled = jnp.concatenate([re, re, im, im] * 2, axis=-1)
    w_b = jnp.where(hit[None], jnp.concatenate([tiled, tiled], axis=1), 0.0).astype(BF16)

    groups = MXU_DIM // c
    row_g = lax.broadcasted_iota(jnp.int32, (OCT_K, MXU_DIM), 0) // p
    col_g = lax.broadcasted_iota(jnp.int32, (OCT_K, MXU_DIM), 1) // c

    def out_blocks(cm):
        cm = jnp.transpose(cm.reshape(OCTS, groups, c, p), (0, 1, 3, 2)).reshape(OCTS, OCT_K, c)
        return jnp.where((row_g == col_g)[None], jnp.tile(cm, (1, 1, groups)), 0.0).astype(BF16)

    a_re_t = jnp.broadcast_to(a_re.reshape(1, SSM_LANES), (SUBLANES, SSM_LANES))
    a_im_t = jnp.broadcast_to(a_im.reshape(1, SSM_LANES), (SUBLANES, SSM_LANES))
    return (w_b, a_re_t, a_im_t, out_blocks(c_re), out_blocks(-c_im))


def _norm_matmul_kernel(x_ref, g_ref, w_ref, *rest):
    n_cast = (len(rest) - 2) // 2
    o_ref, wb_ref = rest[n_cast], rest[-1]
    _cast_blocks(rest[:n_cast], rest[n_cast + 1:-1])

    @pl.when(pl.program_id(0) == 0)
    def _():
        wb_ref[...] = w_ref[...].astype(BF16)

    half = x_ref.shape[0] // 2
    for rows in (slice(0, half), slice(half, 2 * half)):
        hn = _rms(x_ref[rows, :], g_ref[...]).astype(BF16)
        o_ref[rows, :] = _dot(hn, wb_ref[...])


def _norm_matmul(x, g, w, layer, tm, casts=()):
    m, d = x.shape
    n = w.shape[2]
    casts = [_SideCast(cw, cl, m // tm) for cw, cl in casts]
    return pl.pallas_call(
        _norm_matmul_kernel,
        grid=(m // tm,),
        in_specs=[pl.BlockSpec((tm, d), lambda i: (i, 0)), _resident((1, d)),
                  pl.BlockSpec((None, d, n), lambda i: (layer, 0, 0), pipeline_mode=pl.Buffered(1))]
                 + [c.in_spec for c in casts],
        out_specs=[pl.BlockSpec((tm, n), lambda i: (i, 0))] + [c.out_spec for c in casts],
        out_shape=[jax.ShapeDtypeStruct((m, n), F32)] + [c.out_shape for c in casts],
        scratch_shapes=[pltpu.VMEM((d, n), BF16)],
        compiler_params=_params(("arbitrary",)),
        name="ssm_norm_in_proj",
    )(x, g.reshape(1, d), w, *[c.operand for c in casts])


SCAN_COLS = 1024


def _ssm_scan_kernel(u_ref, wb_ref, are_ref, aim_ref, wcr_ref, wci_ref, dsk_ref, h0r_ref, h0i_ref,
                     z_ref, sr_ref, si_ref, bur_ref, bui_ref, *, nb, tt):
    @pl.when(pl.program_id(1) == 0)
    def _():
        sr_ref[...] = h0r_ref[...]
        si_ref[...] = h0i_ref[...]

    u = u_ref[...]
    ub = u.astype(BF16)
    for q in range(QUADS):
        lhs = ub[:, QUAD_K * q:QUAD_K * (q + 1)]
        for h in range(2):
            res = _dot(lhs, wb_ref[q, QUAD_K * h:QUAD_K * (h + 1), :])
            for s in range(2):
                j = 4 * q + 2 * h + s
                bur_ref[:, LANES * j:LANES * (j + 1)] = res[:, MXU_DIM * s:MXU_DIM * s + LANES]
                bui_ref[:, LANES * j:LANES * (j + 1)] = res[:, MXU_DIM * s + LANES:MXU_DIM * (s + 1)]

    for cb in range(SSM_LANES // SCAN_COLS):
        cols = slice(cb * SCAN_COLS, (cb + 1) * SCAN_COLS)
        ar = are_ref[:, cols]
        ai = aim_ref[:, cols]
        for sg in range(nb // SUBLANES):
            seqs = slice(sg * SUBLANES, (sg + 1) * SUBLANES)
            sr = sr_ref[seqs, cols]
            si = si_ref[seqs, cols]
            for t in range(tt):
                rows = slice(t * nb + sg * SUBLANES, t * nb + (sg + 1) * SUBLANES)
                nr = ar * sr - ai * si + bur_ref[rows, cols]
                ni = ar * si + ai * sr + bui_ref[rows, cols]
                sr, si = nr, ni
                bur_ref[rows, cols] = sr
                bui_ref[rows, cols] = si
            sr_ref[seqs, cols] = sr
            si_ref[seqs, cols] = si

    for o in range(OCTS):
        kk = slice(o * OCT_K, (o + 1) * OCT_K)
        nn = slice(o * MXU_DIM, (o + 1) * MXU_DIM)
        y = _dot(bur_ref[:, kk].astype(BF16), wcr_ref[o]) + _dot(bui_ref[:, kk].astype(BF16), wci_ref[o])
        y = y + dsk_ref[:, nn] * u[:, nn]
        z_ref[:, nn] = jax.nn.gelu(y).astype(BF16)


def _ssm_scan(u, tables, d_skip, h0_re, h0_im, *, nb, tt, n_bt, n_tt):
    w_b, a_re_t, a_im_t, w_cr, w_ci = tables
    m, d = u.shape
    r = nb * tt
    assert m == r * n_bt * n_tt and nb % SUBLANES == 0
    rows = pl.BlockSpec((r, d), lambda b, t: (b * n_tt + t, 0))
    state = pl.BlockSpec((nb, SSM_LANES), lambda b, t: (b, 0))
    st_shape = jax.ShapeDtypeStruct((nb * n_bt, SSM_LANES), F32)
    return pl.pallas_call(
        functools.partial(_ssm_scan_kernel, nb=nb, tt=tt),
        grid=(n_bt, n_tt),
        in_specs=[rows, _resident(w_b.shape), _resident(a_re_t.shape), _resident(a_im_t.shape),
                  _resident(w_cr.shape), _resident(w_ci.shape), _resident((1, d)), state, state],
        out_specs=(rows, state, state),
        out_shape=(jax.ShapeDtypeStruct((m, d), BF16), st_shape, st_shape),
        scratch_shapes=[pltpu.VMEM((r, SSM_LANES), F32), pltpu.VMEM((r, SSM_LANES), F32)],
        compiler_params=_params(("parallel", "arbitrary")),
        name="ssm_scan",
    )(u, w_b, a_re_t, a_im_t, w_cr, w_ci, d_skip.reshape(1, d), h0_re, h0_im)


SCAN_SLABS = 8


def _ssm_scan_split_kernel(u_ref, wb_ref, are_ref, aim_ref, wcr_ref, wci_ref, dsk_ref, *rest, nb, tt):
    n_cast = (len(rest) - 7) // 2
    z_ref, sr_ref, si_ref = rest[n_cast:n_cast + 3]
    lhs_ref, bur_ref, bui_ref, y_ref = rest[-4:]
    _cast_blocks(rest[:n_cast], rest[n_cast + 3:-4])
    r = nb * tt
    i = pl.program_id(0)

    @pl.when(i == 0)
    def _():
        lhs_ref[...] = jnp.zeros_like(lhs_ref)
        sr_ref[...] = jnp.zeros_like(sr_ref)
        si_ref[...] = jnp.zeros_like(si_ref)

    def project_in(q):
        for b in range(nb):
            blk = u_ref[b, :, LANES * q:LANES * (q + 1)]
            for h in range(2):
                lhs_ref[q, h, pl.ds(2 * b + h, tt, stride=2 * nb), :] = blk
        lhs = jnp.concatenate([lhs_ref[q, 0], lhs_ref[q, 1]], axis=1).astype(BF16)
        res = _dot(lhs, wb_ref[q])
        for s in range(2):
            bur_ref[2 * q + s] = res[:, MXU_DIM * s:MXU_DIM * s + LANES]
            bui_ref[2 * q + s] = res[:, MXU_DIM * s + LANES:MXU_DIM * (s + 1)]

    def recur(slabs):
        ar = [are_ref[:, LANES * k:LANES * (k + 1)] for k in slabs]
        ai = [aim_ref[:, LANES * k:LANES * (k + 1)] for k in slabs]
        sr = [sr_ref[:, LANES * k:LANES * (k + 1)] for k in slabs]
        si = [si_ref[:, LANES * k:LANES * (k + 1)] for k in slabs]
        for t in range(tt):
            rows = slice(SUBLANES * t, SUBLANES * (t + 1))
            for n, k in enumerate(slabs):
                nr = ar[n] * sr[n] - ai[n] * si[n] + bur_ref[k, rows, :]
                ni = ar[n] * si[n] + ai[n] * sr[n] + bui_ref[k, rows, :]
                sr[n], si[n] = nr, ni
                bur_ref[k, rows, :] = nr
                bui_ref[k, rows, :] = ni
        for n, k in enumerate(slabs):
            sr_ref[:, LANES * k:LANES * (k + 1)] = sr[n]
            si_ref[:, LANES * k:LANES * (k + 1)] = si[n]

    def project_out(o):
        order = [(2 * (2 * o + ql) + s, h) for ql in range(2) for h in range(2) for s in range(2)]
        lre = jnp.concatenate([bur_ref[k, pl.ds(h, r, stride=2), :] for k, h in order], axis=1).astype(BF16)
        lim = jnp.concatenate([bui_ref[k, pl.ds(h, r, stride=2), :] for k, h in order], axis=1).astype(BF16)
        y = _dot(lre, wcr_ref[o]) + _dot(lim, wci_ref[o])
        for half in range(2):
            y_ref[half] = y[:, LANES * half:LANES * (half + 1)]
        for b in range(nb):
            for half in range(2):
                cols = slice(MXU_DIM * o + LANES * half, MXU_DIM * o + LANES * (half + 1))
                yb = y_ref[half, pl.ds(b, tt, stride=nb), :] + dsk_ref[:, cols] * u_ref[b, :, cols]
                z_ref[b, :, cols] = jax.nn.gelu(yb).astype(BF16)

    for q in range(QUADS):
        project_in(q)
    for k0 in range(0, HALF_SLABS, SCAN_SLABS):
        recur(range(k0, k0 + SCAN_SLABS))
    for o in range(OCTS):
        project_out(o)


def _ssm_scan_split(u, tables, d_skip, tt, casts):
    w_b, a_re_t, a_im_t, w_cr, w_ci = tables
    nb, seq, d = u.shape
    assert 2 * nb == SUBLANES and seq % tt == 0
    casts = [_SideCast(cw, layer, seq // tt) for cw, layer in casts]

    def split_lanes(a):
        halves = jnp.transpose(a[0].reshape(QUADS, 2, 2 * LANES), (1, 0, 2)).reshape(2, HALF_SLABS * LANES)
        return jnp.tile(halves, (nb, 1))

    rows = pl.BlockSpec((nb, tt, d), lambda i: (0, i, 0))
    st_shape = jax.ShapeDtypeStruct((SUBLANES, HALF_SLABS * LANES), F32)
    r = nb * tt
    state = pl.BlockSpec(st_shape.shape, lambda i: (0, 0))
    z, s_re, s_im, *w_cast = pl.pallas_call(
        functools.partial(_ssm_scan_split_kernel, nb=nb, tt=tt),
        grid=(seq // tt,),
        in_specs=[rows, _resident(w_b.shape), _resident(st_shape.shape), _resident(st_shape.shape),
                  _resident(w_cr.shape), _resident(w_ci.shape), _resident((1, d))] + [c.in_spec for c in casts],
        out_specs=[rows, state, state] + [c.out_spec for c in casts],
        out_shape=[jax.ShapeDtypeStruct((nb, seq, d), BF16), st_shape, st_shape] + [c.out_shape for c in casts],
        scratch_shapes=[pltpu.VMEM((QUADS, 2, 2 * r, LANES), F32),
                        pltpu.VMEM((HALF_SLABS, 2 * r, LANES), F32),
                        pltpu.VMEM((HALF_SLABS, 2 * r, LANES), F32),
                        pltpu.VMEM((2, r, LANES), F32)],
        compiler_params=pltpu.CompilerParams(dimension_semantics=("arbitrary",), vmem_limit_bytes=BIG_VMEM_LIMIT),
        name="ssm_scan_split",
    )(u, w_b, split_lanes(a_re_t), split_lanes(a_im_t), w_cr, w_ci, d_skip.reshape(1, d),
      *[c.operand for c in casts])

    def join(s):
        return jnp.transpose(s.reshape(nb, 2, QUADS, 2 * LANES), (0, 2, 1, 3)).reshape(nb, SSM_LANES)

    return (z, join(s_re), join(s_im), *w_cast)


SSM_OUT_COLS = 512


def _ssm_out_kernel(z_ref, x_ref, w_ref, *rest):
    o_ref = rest[len(rest) // 2]
    _cast_blocks(rest[:len(rest) // 2], rest[len(rest) // 2 + 1:])
    z = z_ref[...]
    for c in range(D_MODEL // SSM_OUT_COLS):
        cols = slice(c * SSM_OUT_COLS, (c + 1) * SSM_OUT_COLS)
        gate_cols = slice(D_MODEL + c * SSM_OUT_COLS, D_MODEL + (c + 1) * SSM_OUT_COLS)
        val = _dot(z, w_ref[:, cols])
        gate = _dot(z, w_ref[:, gate_cols])
        o_ref[:, cols] = x_ref[:, cols] + val * jax.nn.sigmoid(gate)


def _ssm_out(z, x, w, tm, casts=()):
    m, d = x.shape
    rows = pl.BlockSpec((tm, d), lambda i: (i, 0))
    casts = [_SideCast(cw, layer, m // tm) for cw, layer in casts]
    return pl.pallas_call(
        _ssm_out_kernel,
        grid=(m // tm,),
        in_specs=[rows, rows, _resident(w.shape)] + [c.in_spec for c in casts],
        out_specs=[rows] + [c.out_spec for c in casts],
        out_shape=[jax.ShapeDtypeStruct((m, d), F32)] + [c.out_shape for c in casts],
        compiler_params=_params(("parallel",)),
        name="ssm_out_glu",
    )(z, x, w, *[c.operand for c in casts])


def _ffn_kernel(x_ref, g_ref, wg_ref, wu_ref, wd_ref, gf_ref, *rest, final_norm):
    n_cast = (len(rest) - 2) // 2
    o_ref, hn_ref = rest[n_cast], rest[-1]
    _cast_blocks(rest[:n_cast], rest[n_cast + 1:-1])
    h = pl.program_id(1)

    @pl.when(h == 0)
    def _():
        x = x_ref[...]
        hn_ref[...] = _rms(x, g_ref[...]).astype(BF16)
        o_ref[...] = x

    hn = hn_ref[...]
    act = (jax.nn.silu(_dot(hn, wg_ref[...])) * _dot(hn, wu_ref[...])).astype(BF16)
    o_ref[...] += _dot(act, wd_ref[...])

    if final_norm:
        @pl.when(h == pl.num_programs(1) - 1)
        def _():
            o_ref[...] = _rms(o_ref[...], gf_ref[...])


def _ffn(x, g, w_gate_up, w_down, g_final, *, tm, th, final_norm, casts=()):
    m, d = x.shape
    n_h = FFN_HIDDEN // th
    steps = (m // tm) * n_h
    rows = pl.BlockSpec((tm, d), lambda i, h: (i, 0))
    casts = [_SideCast(cw, layer, _slice_count(cw.shape[1], steps), lambda i, h: i * n_h + h) for cw, layer in casts]
    return pl.pallas_call(
        functools.partial(_ffn_kernel, final_norm=final_norm),
        grid=(m // tm, n_h),
        in_specs=[rows,
                  _resident((1, d)),
                  pl.BlockSpec((d, th), lambda i, h: (0, h)),
                  pl.BlockSpec((d, th), lambda i, h: (0, n_h + h)),
                  pl.BlockSpec((th, d), lambda i, h: (h, 0)),
                  _resident((1, d))] + [c.in_spec for c in casts],
        out_specs=[rows] + [c.out_spec for c in casts],
        out_shape=[jax.ShapeDtypeStruct((m, d), F32)] + [c.out_shape for c in casts],
        scratch_shapes=[pltpu.VMEM((tm, d), BF16)],
        compiler_params=pltpu.CompilerParams(dimension_semantics=("arbitrary", "arbitrary"),
                                             vmem_limit_bytes=BIG_VMEM_LIMIT),
        name="ffn_swiglu",
    )(x, g.reshape(1, d), w_gate_up, w_gate_up, w_down, g_final.reshape(1, d), *[c.operand for c in casts])


def _gmlp_gate_inputs(x, g, win_ref, v_gain):
    hn = _rms(x, g).astype(BF16)
    v = jax.nn.gelu(_dot(hn, win_ref[:, GMLP_WIDTH:]))
    vc = v - jnp.mean(v, axis=-1, keepdims=True)
    vn = (vc * lax.rsqrt(jnp.mean(vc * vc, axis=-1, keepdims=True) + EPS)) * v_gain
    u = jax.nn.gelu(_dot(hn, win_ref[:, :GMLP_WIDTH]))
    return u, vn


def _gmlp_prompt_kernel(x_ref, g_ref, win_ref, vg_ref, ws_ref, bs_ref, wout_ref, *rest):
    n_cast = (len(rest) - 2) // 2
    o_ref, gate_ref = rest[n_cast], rest[-1]
    _cast_blocks(rest[:n_cast], rest[n_cast + 1:-1])
    x = x_ref[...]
    u, vn = _gmlp_gate_inputs(x, g_ref[...], win_ref, vg_ref[...])
    vb = vn.astype(BF16)
    q_idx = lax.broadcasted_iota(jnp.int32, (CHUNK, CHUNK), 0)
    k_idx = lax.broadcasted_iota(jnp.int32, (CHUNK, CHUNK), 1)
    causal = k_idx <= q_idx
    for h in range(GMLP_HEADS):
        cols = slice(h * GMLP_HEAD_DIM, (h + 1) * GMLP_HEAD_DIM)
        ws = jnp.where(causal, ws_ref[h], 0.0).astype(BF16)
        bias = bs_ref[:, h:h + 1]
        for c in range(0, x.shape[0] // CHUNK, 2):
            lo = slice(c * CHUNK, (c + 1) * CHUNK)
            hi = slice((c + 1) * CHUNK, (c + 2) * CHUNK)
            s = _dot(ws, jnp.concatenate([vb[lo, cols], vb[hi, cols]], axis=1)) + bias
            gate_ref[lo, cols] = (u[lo, cols] * s[:, :GMLP_HEAD_DIM]).astype(BF16)
            gate_ref[hi, cols] = (u[hi, cols] * s[:, GMLP_HEAD_DIM:]).astype(BF16)
    o_ref[...] = x + _dot(gate_ref[...], wout_ref[...])


def _gmlp_sample_kernel(x_ref, g_ref, win_ref, vg_ref, wq_ref, bq_ref, wout_ref, o_ref, v_ref, *, steps):
    x = x_ref[...]
    u, vn = _gmlp_gate_inputs(x, g_ref[...], win_ref, vg_ref[...])
    v_ref[...] = vn
    nseq = x.shape[0] // steps
    gates = []
    for q in range(steps):
        s = bq_ref[q:q + 1, :]
        for k in range(q + 1):
            s = s + wq_ref[q * steps + k:q * steps + k + 1, :] * vn[k * nseq:(k + 1) * nseq, :]
        gates.append(u[q * nseq:(q + 1) * nseq, :] * s)
    gate = jnp.concatenate(gates, axis=0).astype(BF16)
    o_ref[...] = x + _dot(gate, wout_ref[...])


def _gmlp_prompt(x, g, w_in, v_gain, w_spatial, b_spatial_t, w_out, tm, casts=()):
    m, d = x.shape
    rows = pl.BlockSpec((tm, d), lambda i: (i, 0))
    casts = [_SideCast(cw, layer, m // tm) for cw, layer in casts]
    return pl.pallas_call(
        _gmlp_prompt_kernel,
        grid=(m // tm,),
        in_specs=[rows, _resident((1, d)), _resident(w_in.shape), _resident((1, GMLP_WIDTH)),
                  _resident(w_spatial.shape), _resident(b_spatial_t.shape), _resident(w_out.shape)]
                 + [c.in_spec for c in casts],
        out_specs=[rows] + [c.out_spec for c in casts],
        out_shape=[jax.ShapeDtypeStruct((m, d), F32)] + [c.out_shape for c in casts],
        scratch_shapes=[pltpu.VMEM((tm, GMLP_WIDTH), BF16)],
        compiler_params=_params(("parallel",)),
        name="gmlp_prompt",
    )(x, g.reshape(1, d), w_in, v_gain.reshape(1, GMLP_WIDTH), w_spatial, b_spatial_t, w_out,
      *[c.operand for c in casts])


def _gmlp_sample(x, g, w_in, v_gain, w_q, b_q, w_out, tm, steps):
    m, d = x.shape
    rows = pl.BlockSpec((tm, d), lambda i: (i, 0))
    return pl.pallas_call(
        functools.partial(_gmlp_sample_kernel, steps=steps),
        grid=(m // tm,),
        in_specs=[rows, _resident((1, d)), _resident(w_in.shape), _resident((1, GMLP_WIDTH)),
                  _resident(w_q.shape), _resident(b_q.shape), _resident(w_out.shape)],
        out_specs=(rows, pl.BlockSpec((tm, GMLP_WIDTH), lambda i: (i, 0))),
        out_shape=(jax.ShapeDtypeStruct((m, d), F32), jax.ShapeDtypeStruct((m, GMLP_WIDTH), F32)),
        compiler_params=_params(("parallel",)),
        name="gmlp_sample",
    )(x, g.reshape(1, d), w_in, v_gain.reshape(1, GMLP_WIDTH), w_q, b_q, w_out)


ROW_TILE = 512
FFN_ROW_TILE = 1024
FFN_HIDDEN_TILE = 512
PROMPT_SCAN_STEPS = 64
SAMPLE_TILE_SEQS = 64


def kernel(x_prompt, x_sample, state_ssm_re, state_ssm_im, norm_mix, norm_ffn, norm_final, ssm_w_in, ssm_lambda_re, ssm_lambda_im, ssm_log_dt, ssm_b_re, ssm_b_im, ssm_c_re, ssm_c_im, ssm_d, ssm_w_out, gmlp_w_in, gmlp_v_gain, gmlp_w_spatial, gmlp_b_spatial, gmlp_w_out, ffn_w_gate_up, ffn_w_down):
    bsz, seq, d = x_prompt.shape
    dbsz, dseq, _ = x_sample.shape

    a_re, a_im, bb_re, bb_im = _discretize(ssm_lambda_re[0], ssm_lambda_im[0], ssm_log_dt[0],
                                           ssm_b_re[0], ssm_b_im[0])
    tables = _ssm_tables(a_re, a_im, bb_re, bb_im, ssm_c_re[0], ssm_c_im[0])

    def ffn(x_rows, layer, w_gu, w_dn, ffn_tm, casts=()):
        return _ffn(x_rows, norm_ffn[layer], w_gu, w_dn, norm_final, tm=ffn_tm, th=FFN_HIDDEN_TILE,
                    final_norm=layer == 1, casts=casts)

    xp = x_prompt.reshape(bsz * seq, d)
    u, w_out0 = _norm_matmul(xp, norm_mix[0], ssm_w_in, 0, ROW_TILE, casts=[(ssm_w_out, 0)])
    z, p_re, p_im, w_gu0, g_w_in = _ssm_scan_split(u.reshape(bsz, seq, d), tables, ssm_d[0], PROMPT_SCAN_STEPS,
                                                   casts=[(ffn_w_gate_up, 0), (gmlp_w_in, 0)])
    xp1, w_dn0, g_w_out = _ssm_out(z.reshape(bsz * seq, d), xp, w_out0, ROW_TILE,
                                   casts=[(ffn_w_down, 0), (gmlp_w_out, 0)])
    xp2, w_gu1, w_dn1 = ffn(xp1, 0, w_gu0, w_dn0, FFN_ROW_TILE, casts=[(ffn_w_gate_up, 1), (ffn_w_down, 1)])
    xp3, = _gmlp_prompt(xp2, norm_mix[1], g_w_in, gmlp_v_gain[0], gmlp_w_spatial[0],
                        jnp.transpose(gmlp_b_spatial[0]), g_w_out, ROW_TILE)
    y_prompt = ffn(xp3, 1, w_gu1, w_dn1, FFN_ROW_TILE)[0].reshape(bsz, seq, d)

    n_bt = dbsz // SAMPLE_TILE_SEQS
    tile_rows = SAMPLE_TILE_SEQS * dseq

    def to_rows(a):
        return jnp.transpose(a.reshape(n_bt, SAMPLE_TILE_SEQS, dseq, d), (0, 2, 1, 3)).reshape(dbsz * dseq, d)

    def from_rows(a):
        return jnp.transpose(a.reshape(n_bt, dseq, SAMPLE_TILE_SEQS, d), (0, 2, 1, 3)).reshape(dbsz, dseq, d)

    xs = to_rows(x_sample)
    h0_re = state_ssm_re[0].reshape(dbsz, SSM_LANES)
    h0_im = state_ssm_im[0].reshape(dbsz, SSM_LANES)
    us, = _norm_matmul(xs, norm_mix[0], ssm_w_in, 0, dbsz * dseq)
    zs, s_re, s_im = _ssm_scan(us, tables, ssm_d[0], h0_re, h0_im, nb=SAMPLE_TILE_SEQS, tt=dseq, n_bt=n_bt, n_tt=1)
    xs1, = _ssm_out(zs, xs, w_out0, tile_rows)
    xs2, = ffn(xs1, 0, w_gu0, w_dn0, dbsz * dseq)
    w_q = jnp.repeat(gmlp_w_spatial[0][:, :dseq, :dseq].reshape(GMLP_HEADS, dseq * dseq).T, GMLP_HEAD_DIM, axis=1)
    b_q = jnp.repeat(gmlp_b_spatial[0][:, :dseq].T, GMLP_HEAD_DIM, axis=1)
    xs3, v_rows = _gmlp_sample(xs2, norm_mix[1], g_w_in, gmlp_v_gain[0], w_q, b_q, g_w_out, tile_rows, dseq)
    y_sample = from_rows(ffn(xs3, 1, w_gu1, w_dn1, dbsz * dseq)[0])

    state_shape = (1, -1, SSM_GROUPS, SSM_STATE)
    return (y_prompt, y_sample,
            p_re.reshape(state_shape), p_im.reshape(state_shape),
            s_re.reshape(state_shape), s_im.reshape(state_shape),
            from_rows(v_rows)[None])
```

```python
import functools

import jax
import jax.numpy as jnp
from jax import lax
from jax.experimental import pallas as pl
from jax.experimental.pallas import tpu as pltpu

D_MODEL = 2048
SSM_GROUPS = 128
SSM_GROUP = 16
SSM_STATE = 64
SSM_LANES = SSM_GROUPS * SSM_STATE
GMLP_WIDTH = D_MODEL
GMLP_HEADS = 16
GMLP_HEAD_DIM = GMLP_WIDTH // GMLP_HEADS
CHUNK = 128
FFN_HIDDEN = 5632
EPS = 1e-6

LANES = 128
SUBLANES = 8
MXU_DIM = 256
VMEM_LIMIT = 56 * 1024 * 1024
BIG_VMEM_LIMIT = 60 * 1024 * 1024

PAIRS = SSM_GROUPS // 2
QUAD_K = 4 * 2 * SSM_GROUP
QUADS = PAIRS // 4
HALF_SLABS = PAIRS // 2
OCTS = D_MODEL // MXU_DIM
OCT_K = SSM_LANES // OCTS

BF16 = jnp.bfloat16
F32 = jnp.float32


def _resident(shape):
    zeros = (0,) * len(shape)
    return pl.BlockSpec(shape, lambda *_: zeros, pipeline_mode=pl.Buffered(1))


def _params(semantics):
    return pltpu.CompilerParams(dimension_semantics=semantics, vmem_limit_bytes=VMEM_LIMIT)


def _rms(x, g):
    ms = jnp.mean(x * x, axis=-1, keepdims=True)
    return (x * lax.rsqrt(ms + EPS)) * g


def _dot(a, b):
    return jnp.dot(a, b, preferred_element_type=F32)


BF16_ROWS = 16


class _SideCast:
    def __init__(self, w, layer, n_blocks, step_of=lambda i: i):
        _, r, c = w.shape
        assert r % (n_blocks * BF16_ROWS) == 0

        def block(*idx):
            return jnp.minimum(step_of(*idx), n_blocks - 1)

        self.operand = w
        self.in_spec = pl.BlockSpec((None, r // n_blocks, c), lambda *idx: (layer, block(*idx), 0))
        self.out_spec = pl.BlockSpec((r // n_blocks, c), lambda *idx: (block(*idx), 0))
        self.out_shape = jax.ShapeDtypeStruct((r, c), BF16)


def _cast_blocks(src_refs, dst_refs):
    for src, dst in zip(src_refs, dst_refs):
        dst[...] = src[...].astype(BF16)


def _slice_count(rows, steps):
    return max(n for n in range(1, steps + 1) if rows % (n * BF16_ROWS) == 0)


def _zoh(lr, li, log_dt):
    dt = jnp.exp(log_dt)
    mag = jnp.exp(lr * dt)
    a_re = mag * jnp.cos(li * dt)
    a_im = mag * jnp.sin(li * dt)
    den = lr * lr + li * li
    nr = a_re - 1.0
    ni = a_im
    q_re = (nr * lr + ni * li) / den
    q_im = (ni * lr - nr * li) / den
    return a_re, a_im, q_re, q_im


def _ssm_tables_kernel(lrg_ref, lig_ref, ldtg_ref, lr_ref, li_ref, ldt_ref, br_ref, bi_ref, cr_ref, ci_ref,
                       are_ref, aim_ref, wb_ref, wcr_ref, wci_ref):
    c, p = SSM_GROUP, SSM_STATE
    a_re, a_im, _, _ = _zoh(lrg_ref[...], lig_ref[...], ldtg_ref[...])
    are_ref[...] = a_re
    aim_ref[...] = a_im

    _, _, q_re, q_im = _zoh(lr_ref[...], li_ref[...], ldt_ref[...])
    br = br_ref[...]
    bi = bi_ref[...]
    bb_re = (q_re * br - q_im * bi).astype(BF16)
    bb_im = (q_re * bi + q_im * br).astype(BF16)

    src = lax.broadcasted_iota(jnp.int32, (2 * p, 2 * MXU_DIM), 0)
    dst = lax.broadcasted_iota(jnp.int32, (2 * p, 2 * MXU_DIM), 1)
    spread = ((src % p == dst % p) & (src // p == (dst % MXU_DIM) // LANES)).astype(BF16)
    row = lax.broadcasted_iota(jnp.int32, (2 * QUAD_K, 2 * MXU_DIM), 0)
    col = lax.broadcasted_iota(jnp.int32, (2 * QUAD_K, 2 * MXU_DIM), 1)
    hit = (row % QUAD_K) // c == 4 * (row // QUAD_K) + 2 * (col // MXU_DIM) + (col % LANES) // p
    for q in range(QUADS):
        rows = slice(QUAD_K * q, QUAD_K * (q + 1))
        both = jnp.concatenate([bb_re[rows, :], bb_im[rows, :]], axis=1)
        tile = _dot(both, spread)
        wb_ref[q] = jnp.where(hit, jnp.concatenate([tile, tile], axis=0), 0.0).astype(BF16)

    groups = MXU_DIM // c
    eye = (lax.broadcasted_iota(jnp.int32, (MXU_DIM, MXU_DIM), 0)
           == lax.broadcasted_iota(jnp.int32, (MXU_DIM, MXU_DIM), 1)).astype(BF16)
    same_group = (lax.broadcasted_iota(jnp.int32, (OCT_K, MXU_DIM), 0) // p
                  == lax.broadcasted_iota(jnp.int32, (OCT_K, MXU_DIM), 1) // c)
    transpose_lhs = (((0,), (0,)), ((), ()))
    for o in range(OCTS):
        rows = slice(MXU_DIM * o, MXU_DIM * (o + 1))
        for src_ref, dst_ref, sign in ((cr_ref, wcr_ref, 1.0), (ci_ref, wci_ref, -1.0)):
            blk = (sign * src_ref[rows, :]).astype(BF16)
            by_state = lax.dot_general(blk, eye, transpose_lhs, preferred_element_type=F32)
            dst_ref[o] = jnp.where(same_group, jnp.concatenate([by_state] * groups, axis=0), 0.0).astype(BF16)


def _ssm_tables(lambda_re, lambda_im, log_dt, b_re, b_im, c_re, c_im):
    g, p, c = b_re.shape
    rows = (g * c, p)

    def per_channel(a):
        return jnp.broadcast_to(a[:, None, :], (g, c, p)).reshape(rows)

    a_shape = jax.ShapeDtypeStruct((g, p), F32)
    wc_shape = jax.ShapeDtypeStruct((OCTS, OCT_K, MXU_DIM), BF16)
    a_re, a_im, w_b, w_cr, w_ci = pl.pallas_call(
        _ssm_tables_kernel,
        out_shape=(a_shape, a_shape, jax.ShapeDtypeStruct((QUADS, 2 * QUAD_K, 2 * MXU_DIM), BF16), wc_shape, wc_shape),
        compiler_params=pltpu.CompilerParams(vmem_limit_bytes=VMEM_LIMIT),
        name="ssm_tables",
    )(lambda_re, lambda_im, jnp.broadcast_to(log_dt[:, None], (g, p)),
      per_channel(lambda_re), per_channel(lambda_im), per_channel(log_dt[:, None]),
      jnp.transpose(b_re, (0, 2, 1)).reshape(rows), jnp.transpose(b_im, (0, 2, 1)).reshape(rows),
      c_re.reshape(rows), c_im.reshape(rows))
    a_re_t = jnp.broadcast_to(a_re.reshape(1, SSM_LANES), (SUBLANES, SSM_LANES))
    a_im_t = jnp.broadcast_to(a_im.reshape(1, SSM_LANES), (SUBLANES, SSM_LANES))
    return (w_b, a_re_t, a_im_t, w_cr, w_ci)


def _norm_matmul_kernel(x_ref, g_ref, w_ref, *rest):
    n_cast = (len(rest) - 2) // 2
    o_ref, wb_ref = rest[n_cast], rest[-1]
    _cast_blocks(rest[:n_cast], rest[n_cast + 1:-1])

    @pl.when(pl.program_id(0) == 0)
    def _():
        wb_ref[...] = w_ref[...].astype(BF16)

    half = x_ref.shape[0] // 2
    for rows in (slice(0, half), slice(half, 2 * half)):
        hn = _rms(x_ref[rows, :], g_ref[...]).astype(BF16)
        o_ref[rows, :] = _dot(hn, wb_ref[...])


def _norm_matmul(x, g, w, layer, tm, casts=()):
    m, d = x.shape
    n = w.shape[2]
    casts = [_SideCast(cw, cl, m // tm) for cw, cl in casts]
    return pl.pallas_call(
        _norm_matmul_kernel,
        grid=(m // tm,),
        in_specs=[pl.BlockSpec((tm, d), lambda i: (i, 0)), _resident((1, d)),
                  pl.BlockSpec((None, d, n), lambda i: (layer, 0, 0), pipeline_mode=pl.Buffered(1))]
                 + [c.in_spec for c in casts],
        out_specs=[pl.BlockSpec((tm, n), lambda i: (i, 0))] + [c.out_spec for c in casts],
        out_shape=[jax.ShapeDtypeStruct((m, n), F32)] + [c.out_shape for c in casts],
        scratch_shapes=[pltpu.VMEM((d, n), BF16)],
        compiler_params=_params(("arbitrary",)),
        name="ssm_norm_in_proj",
    )(x, g.reshape(1, d), w, *[c.operand for c in casts])


SCAN_COLS = 1024


def _ssm_scan_kernel(u_ref, wb_ref, are_ref, aim_ref, wcr_ref, wci_ref, dsk_ref, h0r_ref, h0i_ref,
                     z_ref, sr_ref, si_ref, bur_ref, bui_ref, *, nb, tt):
    @pl.when(pl.program_id(1) == 0)
    def _():
        sr_ref[...] = h0r_ref[...]
        si_ref[...] = h0i_ref[...]

    u = u_ref[...]
    ub = u.astype(BF16)
    for q in range(QUADS):
        lhs = ub[:, QUAD_K * q:QUAD_K * (q + 1)]
        for h in range(2):
            res = _dot(lhs, wb_ref[q, QUAD_K * h:QUAD_K * (h + 1), :])
            for s in range(2):
                j = 4 * q + 2 * h + s
                bur_ref[:, LANES * j:LANES * (j + 1)] = res[:, MXU_DIM * s:MXU_DIM * s + LANES]
                bui_ref[:, LANES * j:LANES * (j + 1)] = res[:, MXU_DIM * s + LANES:MXU_DIM * (s + 1)]

    for cb in range(SSM_LANES // SCAN_COLS):
        cols = slice(cb * SCAN_COLS, (cb + 1) * SCAN_COLS)
        ar = are_ref[:, cols]
        ai = aim_ref[:, cols]
        for sg in range(nb // SUBLANES):
            seqs = slice(sg * SUBLANES, (sg + 1) * SUBLANES)
            sr = sr_ref[seqs, cols]
            si = si_ref[seqs, cols]
            for t in range(tt):
                rows = slice(t * nb + sg * SUBLANES, t * nb + (sg + 1) * SUBLANES)
                nr = ar * sr - ai * si + bur_ref[rows, cols]
                ni = ar * si + ai * sr + bui_ref[rows, cols]
                sr, si = nr, ni
                bur_ref[rows, cols] = sr
                bui_ref[rows, cols] = si
            sr_ref[seqs, cols] = sr
            si_ref[seqs, cols] = si

    for o in range(OCTS):
        kk = slice(o * OCT_K, (o + 1) * OCT_K)
        nn = slice(o * MXU_DIM, (o + 1) * MXU_DIM)
        y = _dot(bur_ref[:, kk].astype(BF16), wcr_ref[o]) + _dot(bui_ref[:, kk].astype(BF16), wci_ref[o])
        y = y + dsk_ref[:, nn] * u[:, nn]
        z_ref[:, nn] = jax.nn.gelu(y).astype(BF16)


def _ssm_scan(u, tables, d_skip, h0_re, h0_im, *, nb, tt, n_bt, n_tt):
    w_b, a_re_t, a_im_t, w_cr, w_ci = tables
    m, d = u.shape
    r = nb * tt
    assert m == r * n_bt * n_tt and nb % SUBLANES == 0
    rows = pl.BlockSpec((r, d), lambda b, t: (b * n_tt + t, 0))
    state = pl.BlockSpec((nb, SSM_LANES), lambda b, t: (b, 0))
    st_shape = jax.ShapeDtypeStruct((nb * n_bt, SSM_LANES), F32)
    return pl.pallas_call(
        functools.partial(_ssm_scan_kernel, nb=nb, tt=tt),
        grid=(n_bt, n_tt),
        in_specs=[rows, _resident(w_b.shape), _resident(a_re_t.shape), _resident(a_im_t.shape),
                  _resident(w_cr.shape), _resident(w_ci.shape), _resident((1, d)), state, state],
        out_specs=(rows, state, state),
        out_shape=(jax.ShapeDtypeStruct((m, d), BF16), st_shape, st_shape),
        scratch_shapes=[pltpu.VMEM((r, SSM_LANES), F32), pltpu.VMEM((r, SSM_LANES), F32)],
        compiler_params=_params(("parallel", "arbitrary")),
        name="ssm_scan",
    )(u, w_b, a_re_t, a_im_t, w_cr, w_ci, d_skip.reshape(1, d), h0_re, h0_im)


SCAN_SLABS = 8


def _ssm_scan_split_kernel(u_ref, wb_ref, are_ref, aim_ref, wcr_ref, wci_ref, dsk_ref, *rest, nb, tt):
    n_cast = (len(rest) - 7) // 2
    z_ref, sr_ref, si_ref = rest[n_cast:n_cast + 3]
    lhs_ref, bur_ref, bui_ref, y_ref = rest[-4:]
    _cast_blocks(rest[:n_cast], rest[n_cast + 3:-4])
    r = nb * tt
    i = pl.program_id(0)

    @pl.when(i == 0)
    def _():
        lhs_ref[...] = jnp.zeros_like(lhs_ref)
        sr_ref[...] = jnp.zeros_like(sr_ref)
        si_ref[...] = jnp.zeros_like(si_ref)

    def project_in(q):
        for b in range(nb):
            blk = u_ref[b, :, LANES * q:LANES * (q + 1)]
            for h in range(2):
                lhs_ref[q, h, pl.ds(2 * b + h, tt, stride=2 * nb), :] = blk
        lhs = jnp.concatenate([lhs_ref[q, 0], lhs_ref[q, 1]], axis=1).astype(BF16)
        res = _dot(lhs, wb_ref[q])
        for s in range(2):
            bur_ref[2 * q + s] = res[:, MXU_DIM * s:MXU_DIM * s + LANES]
            bui_ref[2 * q + s] = res[:, MXU_DIM * s + LANES:MXU_DIM * (s + 1)]

    def recur(slabs):
        ar = [are_ref[:, LANES * k:LANES * (k + 1)] for k in slabs]
        ai = [aim_ref[:, LANES * k:LANES * (k + 1)] for k in slabs]
        sr = [sr_ref[:, LANES * k:LANES * (k + 1)] for k in slabs]
        si = [si_ref[:, LANES * k:LANES * (k + 1)] for k in slabs]
        for t in range(tt):
            rows = slice(SUBLANES * t, SUBLANES * (t + 1))
            for n, k in enumerate(slabs):
                nr = ar[n] * sr[n] - ai[n] * si[n] + bur_ref[k, rows, :]
                ni = ar[n] * si[n] + ai[n] * sr[n] + bui_ref[k, rows, :]
                sr[n], si[n] = nr, ni
                bur_ref[k, rows, :] = nr
                bui_ref[k, rows, :] = ni
        for n, k in enumerate(slabs):
            sr_ref[:, LANES * k:LANES * (k + 1)] = sr[n]
            si_ref[:, LANES * k:LANES * (k + 1)] = si[n]

    def project_out(o):
        order = [(2 * (2 * o + ql) + s, h) for ql in range(2) for h in range(2) for s in range(2)]
        lre = jnp.concatenate([bur_ref[k, pl.ds(h, r, stride=2), :] for k, h in order], axis=1).astype(BF16)
        lim = jnp.concatenate([bui_ref[k, pl.ds(h, r, stride=2), :] for k, h in order], axis=1).astype(BF16)
        y = _dot(lre, wcr_ref[o]) + _dot(lim, wci_ref[o])
        for half in range(2):
            y_ref[half] = y[:, LANES * half:LANES * (half + 1)]
        for b in range(nb):
            for half in range(2):
                cols = slice(MXU_DIM * o + LANES * half, MXU_DIM * o + LANES * (half + 1))
                yb = y_ref[half, pl.ds(b, tt, stride=nb), :] + dsk_ref[:, cols] * u_ref[b, :, cols]
                z_ref[b, :, cols] = jax.nn.gelu(yb).astype(BF16)

    for q in range(QUADS):
        project_in(q)
    for k0 in range(0, HALF_SLABS, SCAN_SLABS):
        recur(range(k0, k0 + SCAN_SLABS))
    for o in range(OCTS):
        project_out(o)


def _ssm_scan_split(u, tables, d_skip, tt, casts):
    w_b, a_re_t, a_im_t, w_cr, w_ci = tables
    nb, seq, d = u.shape
    assert 2 * nb == SUBLANES and seq % tt == 0
    casts = [_SideCast(cw, layer, seq // tt) for cw, layer in casts]

    def split_lanes(a):
        halves = jnp.transpose(a[0].reshape(QUADS, 2, 2 * LANES), (1, 0, 2)).reshape(2, HALF_SLABS * LANES)
        return jnp.tile(halves, (nb, 1))

    rows = pl.BlockSpec((nb, tt, d), lambda i: (0, i, 0))
    st_shape = jax.ShapeDtypeStruct((SUBLANES, HALF_SLABS * LANES), F32)
    r = nb * tt
    state = pl.BlockSpec(st_shape.shape, lambda i: (0, 0))
    z, s_re, s_im, *w_cast = pl.pallas_call(
        functools.partial(_ssm_scan_split_kernel, nb=nb, tt=tt),
        grid=(seq // tt,),
        in_specs=[rows, _resident(w_b.shape), _resident(st_shape.shape), _resident(st_shape.shape),
                  _resident(w_cr.shape), _resident(w_ci.shape), _resident((1, d))] + [c.in_spec for c in casts],
        out_specs=[rows, state, state] + [c.out_spec for c in casts],
        out_shape=[jax.ShapeDtypeStruct((nb, seq, d), BF16), st_shape, st_shape] + [c.out_shape for c in casts],
        scratch_shapes=[pltpu.VMEM((QUADS, 2, 2 * r, LANES), F32),
                        pltpu.VMEM((HALF_SLABS, 2 * r, LANES), F32),
                        pltpu.VMEM((HALF_SLABS, 2 * r, LANES), F32),
                        pltpu.VMEM((2, r, LANES), F32)],
        compiler_params=pltpu.CompilerParams(dimension_semantics=("arbitrary",), vmem_limit_bytes=BIG_VMEM_LIMIT),
        name="ssm_scan_split",
    )(u, w_b, split_lanes(a_re_t), split_lanes(a_im_t), w_cr, w_ci, d_skip.reshape(1, d),
      *[c.operand for c in casts])

    def join(s):
        return jnp.transpose(s.reshape(nb, 2, QUADS, 2 * LANES), (0, 2, 1, 3)).reshape(nb, SSM_LANES)

    return (z, join(s_re), join(s_im), *w_cast)


SSM_OUT_COLS = 512


def _ssm_out_kernel(z_ref, x_ref, w_ref, *rest):
    o_ref = rest[len(rest) // 2]
    _cast_blocks(rest[:len(rest) // 2], rest[len(rest) // 2 + 1:])
    z = z_ref[...]
    for c in range(D_MODEL // SSM_OUT_COLS):
        cols = slice(c * SSM_OUT_COLS, (c + 1) * SSM_OUT_COLS)
        gate_cols = slice(D_MODEL + c * SSM_OUT_COLS, D_MODEL + (c + 1) * SSM_OUT_COLS)
        val = _dot(z, w_ref[:, cols])
        gate = _dot(z, w_ref[:, gate_cols])
        o_ref[:, cols] = x_ref[:, cols] + val * jax.nn.sigmoid(gate)


def _ssm_out(z, x, w, tm, casts=()):
    m, d = x.shape
    rows = pl.BlockSpec((tm, d), lambda i: (i, 0))
    casts = [_SideCast(cw, layer, m // tm) for cw, layer in casts]
    return pl.pallas_call(
        _ssm_out_kernel,
        grid=(m // tm,),
        in_specs=[rows, rows, _resident(w.shape)] + [c.in_spec for c in casts],
        out_specs=[rows] + [c.out_spec for c in casts],
        out_shape=[jax.ShapeDtypeStruct((m, d), F32)] + [c.out_shape for c in casts],
        compiler_params=_params(("parallel",)),
        name="ssm_out_glu",
    )(z, x, w, *[c.operand for c in casts])


def _ffn_kernel(x_ref, g_ref, wg_ref, wu_ref, wd_ref, gf_ref, *rest, final_norm):
    n_cast = (len(rest) - 2) // 2
    o_ref, hn_ref = rest[n_cast], rest[-1]
    _cast_blocks(rest[:n_cast], rest[n_cast + 1:-1])
    h = pl.program_id(1)

    @pl.when(h == 0)
    def _():
        x = x_ref[...]
        hn_ref[...] = _rms(x, g_ref[...]).astype(BF16)
        o_ref[...] = x

    hn = hn_ref[...]
    act = (jax.nn.silu(_dot(hn, wg_ref[...])) * _dot(hn, wu_ref[...])).astype(BF16)
    o_ref[...] += _dot(act, wd_ref[...])

    if final_norm:
        @pl.when(h == pl.num_programs(1) - 1)
        def _():
            o_ref[...] = _rms(o_ref[...], gf_ref[...])


def _ffn(x, g, w_gate_up, w_down, g_final, *, tm, th, final_norm, casts=()):
    m, d = x.shape
    n_h = FFN_HIDDEN // th
    steps = (m // tm) * n_h
    rows = pl.BlockSpec((tm, d), lambda i, h: (i, 0))
    casts = [_SideCast(cw, layer, _slice_count(cw.shape[1], steps), lambda i, h: i * n_h + h) for cw, layer in casts]
    return pl.pallas_call(
        functools.partial(_ffn_kernel, final_norm=final_norm),
        grid=(m // tm, n_h),
        in_specs=[rows,
                  _resident((1, d)),
                  pl.BlockSpec((d, th), lambda i, h: (0, h)),
                  pl.BlockSpec((d, th), lambda i, h: (0, n_h + h)),
                  pl.BlockSpec((th, d), lambda i, h: (h, 0)),
                  _resident((1, d))] + [c.in_spec for c in casts],
        out_specs=[rows] + [c.out_spec for c in casts],
        out_shape=[jax.ShapeDtypeStruct((m, d), F32)] + [c.out_shape for c in casts],
        scratch_shapes=[pltpu.VMEM((tm, d), BF16)],
        compiler_params=pltpu.CompilerParams(dimension_semantics=("arbitrary", "arbitrary"),
                                             vmem_limit_bytes=BIG_VMEM_LIMIT),
        name="ffn_swiglu",
    )(x, g.reshape(1, d), w_gate_up, w_gate_up, w_down, g_final.reshape(1, d), *[c.operand for c in casts])


def _gmlp_gate_inputs(x, g, win_ref, v_gain):
    hn = _rms(x, g).astype(BF16)
    v = jax.nn.gelu(_dot(hn, win_ref[:, GMLP_WIDTH:]))
    vc = v - jnp.mean(v, axis=-1, keepdims=True)
    vn = (vc * lax.rsqrt(jnp.mean(vc * vc, axis=-1, keepdims=True) + EPS)) * v_gain
    u = jax.nn.gelu(_dot(hn, win_ref[:, :GMLP_WIDTH]))
    return u, vn


def _gmlp_prompt_kernel(x_ref, g_ref, win_ref, vg_ref, ws_ref, bs_ref, wout_ref, *rest):
    n_cast = (len(rest) - 2) // 2
    o_ref, gate_ref = rest[n_cast], rest[-1]
    _cast_blocks(rest[:n_cast], rest[n_cast + 1:-1])
    x = x_ref[...]
    u, vn = _gmlp_gate_inputs(x, g_ref[...], win_ref, vg_ref[...])
    vb = vn.astype(BF16)
    q_idx = lax.broadcasted_iota(jnp.int32, (CHUNK, CHUNK), 0)
    k_idx = lax.broadcasted_iota(jnp.int32, (CHUNK, CHUNK), 1)
    causal = k_idx <= q_idx
    for h in range(GMLP_HEADS):
        cols = slice(h * GMLP_HEAD_DIM, (h + 1) * GMLP_HEAD_DIM)
        ws = jnp.where(causal, ws_ref[h], 0.0).astype(BF16)
        bias = bs_ref[:, h:h + 1]
        for c in range(0, x.shape[0] // CHUNK, 2):
            lo = slice(c * CHUNK, (c + 1) * CHUNK)
            hi = slice((c + 1) * CHUNK, (c + 2) * CHUNK)
            s = _dot(ws, jnp.concatenate([vb[lo, cols], vb[hi, cols]], axis=1)) + bias
            gate_ref[lo, cols] = (u[lo, cols] * s[:, :GMLP_HEAD_DIM]).astype(BF16)
            gate_ref[hi, cols] = (u[hi, cols] * s[:, GMLP_HEAD_DIM:]).astype(BF16)
    o_ref[...] = x + _dot(gate_ref[...], wout_ref[...])


def _gmlp_sample_kernel(x_ref, g_ref, win_ref, vg_ref, wq_ref, bq_ref, wout_ref, o_ref, v_ref, *, steps):
    x = x_ref[...]
    u, vn = _gmlp_gate_inputs(x, g_ref[...], win_ref, vg_ref[...])
    v_ref[...] = vn
    nseq = x.shape[0] // steps
    gates = []
    for q in range(steps):
        s = bq_ref[q:q + 1, :]
        for k in range(q + 1):
            s = s + wq_ref[q * steps + k:q * steps + k + 1, :] * vn[k * nseq:(k + 1) * nseq, :]
        gates.append(u[q * nseq:(q + 1) * nseq, :] * s)
    gate = jnp.concatenate(gates, axis=0).astype(BF16)
    o_ref[...] = x + _dot(gate, wout_ref[...])


def _gmlp_prompt(x, g, w_in, v_gain, w_spatial, b_spatial_t, w_out, tm, casts=()):
    m, d = x.shape
    rows = pl.BlockSpec((tm, d), lambda i: (i, 0))
    casts = [_SideCast(cw, layer, m // tm) for cw, layer in casts]
    return pl.pallas_call(
        _gmlp_prompt_kernel,
        grid=(m // tm,),
        in_specs=[rows, _resident((1, d)), _resident(w_in.shape), _resident((1, GMLP_WIDTH)),
                  _resident(w_spatial.shape), _resident(b_spatial_t.shape), _resident(w_out.shape)]
                 + [c.in_spec for c in casts],
        out_specs=[rows] + [c.out_spec for c in casts],
        out_shape=[jax.ShapeDtypeStruct((m, d), F32)] + [c.out_shape for c in casts],
        scratch_shapes=[pltpu.VMEM((tm, GMLP_WIDTH), BF16)],
        compiler_params=_params(("parallel",)),
        name="gmlp_prompt",
    )(x, g.reshape(1, d), w_in, v_gain.reshape(1, GMLP_WIDTH), w_spatial, b_spatial_t, w_out,
      *[c.operand for c in casts])


def _gmlp_sample(x, g, w_in, v_gain, w_q, b_q, w_out, tm, steps):
    m, d = x.shape
    rows = pl.BlockSpec((tm, d), lambda i: (i, 0))
    return pl.pallas_call(
        functools.partial(_gmlp_sample_kernel, steps=steps),
        grid=(m // tm,),
        in_specs=[rows, _resident((1, d)), _resident(w_in.shape), _resident((1, GMLP_WIDTH)),
                  _resident(w_q.shape), _resident(b_q.shape), _resident(w_out.shape)],
        out_specs=(rows, pl.BlockSpec((tm, GMLP_WIDTH), lambda i: (i, 0))),
        out_shape=(jax.ShapeDtypeStruct((m, d), F32), jax.ShapeDtypeStruct((m, GMLP_WIDTH), F32)),
        compiler_params=_params(("parallel",)),
        name="gmlp_sample",
    )(x, g.reshape(1, d), w_in, v_gain.reshape(1, GMLP_WIDTH), w_q, b_q, w_out)


ROW_TILE = 512
FFN_ROW_TILE = 1024
FFN_HIDDEN_TILE = 512
PROMPT_SCAN_STEPS = 64
SAMPLE_TILE_SEQS = 64


def kernel(x_prompt, x_sample, state_ssm_re, state_ssm_im, norm_mix, norm_ffn, norm_final, ssm_w_in, ssm_lambda_re, ssm_lambda_im, ssm_log_dt, ssm_b_re, ssm_b_im, ssm_c_re, ssm_c_im, ssm_d, ssm_w_out, gmlp_w_in, gmlp_v_gain, gmlp_w_spatial, gmlp_b_spatial, gmlp_w_out, ffn_w_gate_up, ffn_w_down):
    bsz, seq, d = x_prompt.shape
    dbsz, dseq, _ = x_sample.shape

    tables = _ssm_tables(ssm_lambda_re[0], ssm_lambda_im[0], ssm_log_dt[0], ssm_b_re[0], ssm_b_im[0],
                         ssm_c_re[0], ssm_c_im[0])

    def ffn(x_rows, layer, w_gu, w_dn, ffn_tm, casts=()):
        return _ffn(x_rows, norm_ffn[layer], w_gu, w_dn, norm_final, tm=ffn_tm, th=FFN_HIDDEN_TILE,
                    final_norm=layer == 1, casts=casts)

    xp = x_prompt.reshape(bsz * seq, d)
    u, w_out0 = _norm_matmul(xp, norm_mix[0], ssm_w_in, 0, ROW_TILE, casts=[(ssm_w_out, 0)])
    z, p_re, p_im, w_gu0, g_w_in = _ssm_scan_split(u.reshape(bsz, seq, d), tables, ssm_d[0], PROMPT_SCAN_STEPS,
                                                   casts=[(ffn_w_gate_up, 0), (gmlp_w_in, 0)])
    xp1, w_dn0, g_w_out = _ssm_out(z.reshape(bsz * seq, d), xp, w_out0, ROW_TILE,
                                   casts=[(ffn_w_down, 0), (gmlp_w_out, 0)])
    xp2, w_gu1, w_dn1 = ffn(xp1, 0, w_gu0, w_dn0, FFN_ROW_TILE, casts=[(ffn_w_gate_up, 1), (ffn_w_down, 1)])
    xp3, = _gmlp_prompt(xp2, norm_mix[1], g_w_in, gmlp_v_gain[0], gmlp_w_spatial[0],
                        jnp.transpose(gmlp_b_spatial[0]), g_w_out, ROW_TILE)
    y_prompt = ffn(xp3, 1, w_gu1, w_dn1, FFN_ROW_TILE)[0].reshape(bsz, seq, d)

    n_bt = dbsz // SAMPLE_TILE_SEQS
    tile_rows = SAMPLE_TILE_SEQS * dseq

    def to_rows(a):
        return jnp.transpose(a.reshape(n_bt, SAMPLE_TILE_SEQS, dseq, d), (0, 2, 1, 3)).reshape(dbsz * dseq, d)

    def from_rows(a):
        return jnp.transpose(a.reshape(n_bt, dseq, SAMPLE_TILE_SEQS, d), (0, 2, 1, 3)).reshape(dbsz, dseq, d)

    xs = to_rows(x_sample)
    h0_re = state_ssm_re[0].reshape(dbsz, SSM_LANES)
    h0_im = state_ssm_im[0].reshape(dbsz, SSM_LANES)
    us, = _norm_matmul(xs, norm_mix[0], ssm_w_in, 0, dbsz * dseq)
    zs, s_re, s_im = _ssm_scan(us, tables, ssm_d[0], h0_re, h0_im, nb=SAMPLE_TILE_SEQS, tt=dseq, n_bt=n_bt, n_tt=1)
    xs1, = _ssm_out(zs, xs, w_out0, tile_rows)
    xs2, = ffn(xs1, 0, w_gu0, w_dn0, dbsz * dseq)
    w_q = jnp.repeat(gmlp_w_spatial[0][:, :dseq, :dseq].reshape(GMLP_HEADS, dseq * dseq).T, GMLP_HEAD_DIM, axis=1)
    b_q = jnp.repeat(gmlp_b_spatial[0][:, :dseq].T, GMLP_HEAD_DIM, axis=1)
    xs3, v_rows = _gmlp_sample(xs2, norm_mix[1], g_w_in, gmlp_v_gain[0], w_q, b_q, g_w_out, tile_rows, dseq)
    y_sample = from_rows(ffn(xs3, 1, w_gu1, w_dn1, dbsz * dseq)[0])

    state_shape = (1, -1, SSM_GROUPS, SSM_STATE)
    return (y_prompt, y_sample,
            p_re.reshape(state_shape), p_im.reshape(state_shape),
            s_re.reshape(state_shape), s_im.reshape(state_shape),
            from_rows(v_rows)[None])
```

```python
import functools

import jax
import jax.numpy as jnp
from jax import lax
from jax.experimental import pallas as pl
from jax.experimental.pallas import tpu as pltpu

D_MODEL = 2048
SSM_GROUPS = 128
SSM_GROUP = 16
SSM_STATE = 64
SSM_LANES = SSM_GROUPS * SSM_STATE
GMLP_WIDTH = D_MODEL
GMLP_HEADS = 16
GMLP_HEAD_DIM = GMLP_WIDTH // GMLP_HEADS
CHUNK = 128
FFN_HIDDEN = 5632
EPS = 1e-6

LANES = 128
SUBLANES = 8
MXU_DIM = 256
VMEM_LIMIT = 56 * 1024 * 1024
BIG_VMEM_LIMIT = 60 * 1024 * 1024

PAIRS = SSM_GROUPS // 2
QUAD_K = 4 * 2 * SSM_GROUP
QUADS = PAIRS // 4
HALF_SLABS = PAIRS // 2
OCTS = D_MODEL // MXU_DIM
OCT_K = SSM_LANES // OCTS

BF16 = jnp.bfloat16
F32 = jnp.float32


def _resident(shape):
    zeros = (0,) * len(shape)
    return pl.BlockSpec(shape, lambda *_: zeros, pipeline_mode=pl.Buffered(1))


def _params(semantics):
    return pltpu.CompilerParams(dimension_semantics=semantics, vmem_limit_bytes=VMEM_LIMIT)


def _rms(x, g):
    ms = jnp.mean(x * x, axis=-1, keepdims=True)
    return (x * lax.rsqrt(ms + EPS)) * g


def _dot(a, b):
    return jnp.dot(a, b, preferred_element_type=F32)


BF16_ROWS = 16


class _SideCast:
    def __init__(self, w, layer, n_blocks, step_of=lambda i: i):
        _, r, c = w.shape
        assert r % (n_blocks * BF16_ROWS) == 0

        def block(*idx):
            return jnp.minimum(step_of(*idx), n_blocks - 1)

        self.operand = w
        self.in_spec = pl.BlockSpec((None, r // n_blocks, c), lambda *idx: (layer, block(*idx), 0))
        self.out_spec = pl.BlockSpec((r // n_blocks, c), lambda *idx: (block(*idx), 0))
        self.out_shape = jax.ShapeDtypeStruct((r, c), BF16)


def _cast_blocks(src_refs, dst_refs):
    for src, dst in zip(src_refs, dst_refs):
        dst[...] = src[...].astype(BF16)


def _slice_count(rows, steps):
    return max(n for n in range(1, steps + 1) if rows % (n * BF16_ROWS) == 0)


def _zoh(lr, li, log_dt):
    dt = jnp.exp(log_dt)
    mag = jnp.exp(lr * dt)
    a_re = mag * jnp.cos(li * dt)
    a_im = mag * jnp.sin(li * dt)
    den = lr * lr + li * li
    nr = a_re - 1.0
    ni = a_im
    q_re = (nr * lr + ni * li) / den
    q_im = (ni * lr - nr * li) / den
    return a_re, a_im, q_re, q_im


def _ssm_tables_kernel(lr_ref, li_ref, ldt_ref, br_ref, bi_ref, cr_ref, ci_ref,
                       are_ref, aim_ref, wb_ref, wcr_ref, wci_ref):
    c, p = SSM_GROUP, SSM_STATE
    a_re, a_im, q_re, q_im = _zoh(lr_ref[...], li_ref[...], ldt_ref[...])
    are_ref[...] = a_re
    aim_ref[...] = a_im

    def per_channel(a):
        return jnp.broadcast_to(a[:, None, :], (SSM_GROUPS, c, p)).reshape(SSM_GROUPS * c, p)

    q_re, q_im = per_channel(q_re), per_channel(q_im)
    br = br_ref[...]
    bi = bi_ref[...]
    bb_re = (q_re * br - q_im * bi).astype(BF16)
    bb_im = (q_re * bi + q_im * br).astype(BF16)

    src = lax.broadcasted_iota(jnp.int32, (2 * p, 2 * MXU_DIM), 0)
    dst = lax.broadcasted_iota(jnp.int32, (2 * p, 2 * MXU_DIM), 1)
    spread = ((src % p == dst % p) & (src // p == (dst % MXU_DIM) // LANES)).astype(BF16)
    row = lax.broadcasted_iota(jnp.int32, (2 * QUAD_K, 2 * MXU_DIM), 0)
    col = lax.broadcasted_iota(jnp.int32, (2 * QUAD_K, 2 * MXU_DIM), 1)
    hit = (row % QUAD_K) // c == 4 * (row // QUAD_K) + 2 * (col // MXU_DIM) + (col % LANES) // p
    for q in range(QUADS):
        rows = slice(QUAD_K * q, QUAD_K * (q + 1))
        both = jnp.concatenate([bb_re[rows, :], bb_im[rows, :]], axis=1)
        tile = _dot(both, spread)
        wb_ref[q] = jnp.where(hit, jnp.concatenate([tile, tile], axis=0), 0.0).astype(BF16)

    groups = MXU_DIM // c
    eye = (lax.broadcasted_iota(jnp.int32, (MXU_DIM, MXU_DIM), 0)
           == lax.broadcasted_iota(jnp.int32, (MXU_DIM, MXU_DIM), 1)).astype(BF16)
    same_group = (lax.broadcasted_iota(jnp.int32, (OCT_K, MXU_DIM), 0) // p
                  == lax.broadcasted_iota(jnp.int32, (OCT_K, MXU_DIM), 1) // c)
    transpose_lhs = (((0,), (0,)), ((), ()))
    for o in range(OCTS):
        rows = slice(MXU_DIM * o, MXU_DIM * (o + 1))
        for src_ref, dst_ref, sign in ((cr_ref, wcr_ref, 1.0), (ci_ref, wci_ref, -1.0)):
            blk = (sign * src_ref[rows, :]).astype(BF16)
            by_state = lax.dot_general(blk, eye, transpose_lhs, preferred_element_type=F32)
            dst_ref[o] = jnp.where(same_group, jnp.concatenate([by_state] * groups, axis=0), 0.0).astype(BF16)


def _ssm_tables(lambda_re, lambda_im, log_dt, b_re, b_im, c_re, c_im):
    g, p, c = b_re.shape
    rows = (g * c, p)
    a_shape = jax.ShapeDtypeStruct((g, p), F32)
    wc_shape = jax.ShapeDtypeStruct((OCTS, OCT_K, MXU_DIM), BF16)
    a_re, a_im, w_b, w_cr, w_ci = pl.pallas_call(
        _ssm_tables_kernel,
        out_shape=(a_shape, a_shape, jax.ShapeDtypeStruct((QUADS, 2 * QUAD_K, 2 * MXU_DIM), BF16), wc_shape, wc_shape),
        compiler_params=pltpu.CompilerParams(vmem_limit_bytes=VMEM_LIMIT),
        name="ssm_tables",
    )(lambda_re, lambda_im, jnp.broadcast_to(log_dt[:, None], (g, p)),
      jnp.transpose(b_re, (0, 2, 1)).reshape(rows), jnp.transpose(b_im, (0, 2, 1)).reshape(rows),
      c_re.reshape(rows), c_im.reshape(rows))
    a_re_t = jnp.broadcast_to(a_re.reshape(1, SSM_LANES), (SUBLANES, SSM_LANES))
    a_im_t = jnp.broadcast_to(a_im.reshape(1, SSM_LANES), (SUBLANES, SSM_LANES))
    return (w_b, a_re_t, a_im_t, w_cr, w_ci)


def _norm_matmul_kernel(x_ref, g_ref, w_ref, *rest):
    n_cast = (len(rest) - 2) // 2
    o_ref, wb_ref = rest[n_cast], rest[-1]
    _cast_blocks(rest[:n_cast], rest[n_cast + 1:-1])

    @pl.when(pl.program_id(0) == 0)
    def _():
        wb_ref[...] = w_ref[...].astype(BF16)

    half = x_ref.shape[0] // 2
    for rows in (slice(0, half), slice(half, 2 * half)):
        hn = _rms(x_ref[rows, :], g_ref[...]).astype(BF16)
        o_ref[rows, :] = _dot(hn, wb_ref[...])


def _norm_matmul(x, g, w, layer, tm, casts=()):
    m, d = x.shape
    n = w.shape[2]
    casts = [_SideCast(cw, cl, m // tm) for cw, cl in casts]
    return pl.pallas_call(
        _norm_matmul_kernel,
        grid=(m // tm,),
        in_specs=[pl.BlockSpec((tm, d), lambda i: (i, 0)), _resident((1, d)),
                  pl.BlockSpec((None, d, n), lambda i: (layer, 0, 0), pipeline_mode=pl.Buffered(1))]
                 + [c.in_spec for c in casts],
        out_specs=[pl.BlockSpec((tm, n), lambda i: (i, 0))] + [c.out_spec for c in casts],
        out_shape=[jax.ShapeDtypeStruct((m, n), F32)] + [c.out_shape for c in casts],
        scratch_shapes=[pltpu.VMEM((d, n), BF16)],
        compiler_params=_params(("arbitrary",)),
        name="ssm_norm_in_proj",
    )(x, g.reshape(1, d), w, *[c.operand for c in casts])


SCAN_COLS = 1024


def _ssm_scan_kernel(u_ref, wb_ref, are_ref, aim_ref, wcr_ref, wci_ref, dsk_ref, h0r_ref, h0i_ref,
                     z_ref, sr_ref, si_ref, bur_ref, bui_ref, *, nb, tt):
    @pl.when(pl.program_id(1) == 0)
    def _():
        sr_ref[...] = h0r_ref[...]
        si_ref[...] = h0i_ref[...]

    u = u_ref[...]
    ub = u.astype(BF16)
    for q in range(QUADS):
        lhs = ub[:, QUAD_K * q:QUAD_K * (q + 1)]
        for h in range(2):
            res = _dot(lhs, wb_ref[q, QUAD_K * h:QUAD_K * (h + 1), :])
            for s in range(2):
                j = 4 * q + 2 * h + s
                bur_ref[:, LANES * j:LANES * (j + 1)] = res[:, MXU_DIM * s:MXU_DIM * s + LANES]
                bui_ref[:, LANES * j:LANES * (j + 1)] = res[:, MXU_DIM * s + LANES:MXU_DIM * (s + 1)]

    for cb in range(SSM_LANES // SCAN_COLS):
        cols = slice(cb * SCAN_COLS, (cb + 1) * SCAN_COLS)
        ar = are_ref[:, cols]
        ai = aim_ref[:, cols]
        for sg in range(nb // SUBLANES):
            seqs = slice(sg * SUBLANES, (sg + 1) * SUBLANES)
            sr = sr_ref[seqs, cols]
            si = si_ref[seqs, cols]
            for t in range(tt):
                rows = slice(t * nb + sg * SUBLANES, t * nb + (sg + 1) * SUBLANES)
                nr = ar * sr - ai * si + bur_ref[rows, cols]
                ni = ar * si + ai * sr + bui_ref[rows, cols]
                sr, si = nr, ni
                bur_ref[rows, cols] = sr
                bui_ref[rows, cols] = si
            sr_ref[seqs, cols] = sr
            si_ref[seqs, cols] = si

    for o in range(OCTS):
        kk = slice(o * OCT_K, (o + 1) * OCT_K)
        nn = slice(o * MXU_DIM, (o + 1) * MXU_DIM)
        y = _dot(bur_ref[:, kk].astype(BF16), wcr_ref[o]) + _dot(bui_ref[:, kk].astype(BF16), wci_ref[o])
        y = y + dsk_ref[:, nn] * u[:, nn]
        z_ref[:, nn] = jax.nn.gelu(y).astype(BF16)


def _ssm_scan(u, tables, d_skip, h0_re, h0_im, *, nb, tt, n_bt, n_tt):
    w_b, a_re_t, a_im_t, w_cr, w_ci = tables
    m, d = u.shape
    r = nb * tt
    assert m == r * n_bt * n_tt and nb % SUBLANES == 0
    rows = pl.BlockSpec((r, d), lambda b, t: (b * n_tt + t, 0))
    state = pl.BlockSpec((nb, SSM_LANES), lambda b, t: (b, 0))
    st_shape = jax.ShapeDtypeStruct((nb * n_bt, SSM_LANES), F32)
    return pl.pallas_call(
        functools.partial(_ssm_scan_kernel, nb=nb, tt=tt),
        grid=(n_bt, n_tt),
        in_specs=[rows, _resident(w_b.shape), _resident(a_re_t.shape), _resident(a_im_t.shape),
                  _resident(w_cr.shape), _resident(w_ci.shape), _resident((1, d)), state, state],
        out_specs=(rows, state, state),
        out_shape=(jax.ShapeDtypeStruct((m, d), BF16), st_shape, st_shape),
        scratch_shapes=[pltpu.VMEM((r, SSM_LANES), F32), pltpu.VMEM((r, SSM_LANES), F32)],
        compiler_params=_params(("parallel", "arbitrary")),
        name="ssm_scan",
    )(u, w_b, a_re_t, a_im_t, w_cr, w_ci, d_skip.reshape(1, d), h0_re, h0_im)


SCAN_SLABS = 8


def _ssm_scan_split_kernel(u_ref, wb_ref, are_ref, aim_ref, wcr_ref, wci_ref, dsk_ref, *rest, nb, tt):
    n_cast = (len(rest) - 7) // 2
    z_ref, sr_ref, si_ref = rest[n_cast:n_cast + 3]
    lhs_ref, bur_ref, bui_ref, y_ref = rest[-4:]
    _cast_blocks(rest[:n_cast], rest[n_cast + 3:-4])
    r = nb * tt
    i = pl.program_id(0)

    @pl.when(i == 0)
    def _():
        lhs_ref[...] = jnp.zeros_like(lhs_ref)
        sr_ref[...] = jnp.zeros_like(sr_ref)
        si_ref[...] = jnp.zeros_like(si_ref)

    def project_in(q):
        for b in range(nb):
            blk = u_ref[b, :, LANES * q:LANES * (q + 1)]
            for h in range(2):
                lhs_ref[q, h, pl.ds(2 * b + h, tt, stride=2 * nb), :] = blk
        lhs = jnp.concatenate([lhs_ref[q, 0], lhs_ref[q, 1]], axis=1).astype(BF16)
        res = _dot(lhs, wb_ref[q])
        for s in range(2):
            bur_ref[2 * q + s] = res[:, MXU_DIM * s:MXU_DIM * s + LANES]
            bui_ref[2 * q + s] = res[:, MXU_DIM * s + LANES:MXU_DIM * (s + 1)]

    def recur(slabs):
        ar = [are_ref[:, LANES * k:LANES * (k + 1)] for k in slabs]
        ai = [aim_ref[:, LANES * k:LANES * (k + 1)] for k in slabs]
        sr = [sr_ref[:, LANES * k:LANES * (k + 1)] for k in slabs]
        si = [si_ref[:, LANES * k:LANES * (k + 1)] for k in slabs]
        for t in range(tt):
            rows = slice(SUBLANES * t, SUBLANES * (t + 1))
            for n, k in enumerate(slabs):
                nr = ar[n] * sr[n] - ai[n] * si[n] + bur_ref[k, rows, :]
                ni = ar[n] * si[n] + ai[n] * sr[n] + bui_ref[k, rows, :]
                sr[n], si[n] = nr, ni
                bur_ref[k, rows, :] = nr
                bui_ref[k, rows, :] = ni
        for n, k in enumerate(slabs):
            sr_ref[:, LANES * k:LANES * (k + 1)] = sr[n]
            si_ref[:, LANES * k:LANES * (k + 1)] = si[n]

    def project_out(o):
        order = [(2 * (2 * o + ql) + s, h) for ql in range(2) for h in range(2) for s in range(2)]
        lre = jnp.concatenate([bur_ref[k, pl.ds(h, r, stride=2), :] for k, h in order], axis=1).astype(BF16)
        lim = jnp.concatenate([bui_ref[k, pl.ds(h, r, stride=2), :] for k, h in order], axis=1).astype(BF16)
        y = _dot(lre, wcr_ref[o]) + _dot(lim, wci_ref[o])
        for half in range(2):
            y_ref[half] = y[:, LANES * half:LANES * (half + 1)]
        for b in range(nb):
            for half in range(2):
                cols = slice(MXU_DIM * o + LANES * half, MXU_DIM * o + LANES * (half + 1))
                yb = y_ref[half, pl.ds(b, tt, stride=nb), :] + dsk_ref[:, cols] * u_ref[b, :, cols]
                z_ref[b, :, cols] = jax.nn.gelu(yb).astype(BF16)

    for q in range(QUADS):
        project_in(q)
    for k0 in range(0, HALF_SLABS, SCAN_SLABS):
        recur(range(k0, k0 + SCAN_SLABS))
    for o in range(OCTS):
        project_out(o)


def _ssm_scan_split(u, tables, d_skip, tt, casts):
    w_b, a_re_t, a_im_t, w_cr, w_ci = tables
    nb, seq, d = u.shape
    assert 2 * nb == SUBLANES and seq % tt == 0
    casts = [_SideCast(cw, layer, seq // tt) for cw, layer in casts]

    def split_lanes(a):
        halves = jnp.transpose(a[0].reshape(QUADS, 2, 2 * LANES), (1, 0, 2)).reshape(2, HALF_SLABS * LANES)
        return jnp.tile(halves, (nb, 1))

    rows = pl.BlockSpec((nb, tt, d), lambda i: (0, i, 0))
    st_shape = jax.ShapeDtypeStruct((SUBLANES, HALF_SLABS * LANES), F32)
    r = nb * tt
    state = pl.BlockSpec(st_shape.shape, lambda i: (0, 0))
    z, s_re, s_im, *w_cast = pl.pallas_call(
        functools.partial(_ssm_scan_split_kernel, nb=nb, tt=tt),
        grid=(seq // tt,),
        in_specs=[rows, _resident(w_b.shape), _resident(st_shape.shape), _resident(st_shape.shape),
                  _resident(w_cr.shape), _resident(w_ci.shape), _resident((1, d))] + [c.in_spec for c in casts],
        out_specs=[rows, state, state] + [c.out_spec for c in casts],
        out_shape=[jax.ShapeDtypeStruct((nb, seq, d), BF16), st_shape, st_shape] + [c.out_shape for c in casts],
        scratch_shapes=[pltpu.VMEM((QUADS, 2, 2 * r, LANES), F32),
                        pltpu.VMEM((HALF_SLABS, 2 * r, LANES), F32),
                        pltpu.VMEM((HALF_SLABS, 2 * r, LANES), F32),
                        pltpu.VMEM((2, r, LANES), F32)],
        compiler_params=pltpu.CompilerParams(dimension_semantics=("arbitrary",), vmem_limit_bytes=BIG_VMEM_LIMIT),
        name="ssm_scan_split",
    )(u, w_b, split_lanes(a_re_t), split_lanes(a_im_t), w_cr, w_ci, d_skip.reshape(1, d),
      *[c.operand for c in casts])

    def join(s):
        return jnp.transpose(s.reshape(nb, 2, QUADS, 2 * LANES), (0, 2, 1, 3)).reshape(nb, SSM_LANES)

    return (z, join(s_re), join(s_im), *w_cast)


SSM_OUT_COLS = 512


def _ssm_out_kernel(z_ref, x_ref, w_ref, *rest):
    o_ref = rest[len(rest) // 2]
    _cast_blocks(rest[:len(rest) // 2], rest[len(rest) // 2 + 1:])
    z = z_ref[...]
    for c in range(D_MODEL // SSM_OUT_COLS):
        cols = slice(c * SSM_OUT_COLS, (c + 1) * SSM_OUT_COLS)
        gate_cols = slice(D_MODEL + c * SSM_OUT_COLS, D_MODEL + (c + 1) * SSM_OUT_COLS)
        val = _dot(z, w_ref[:, cols])
        gate = _dot(z, w_ref[:, gate_cols])
        o_ref[:, cols] = x_ref[:, cols] + val * jax.nn.sigmoid(gate)


def _ssm_out(z, x, w, tm, casts=()):
    m, d = x.shape
    rows = pl.BlockSpec((tm, d), lambda i: (i, 0))
    casts = [_SideCast(cw, layer, m // tm) for cw, layer in casts]
    return pl.pallas_call(
        _ssm_out_kernel,
        grid=(m // tm,),
        in_specs=[rows, rows, _resident(w.shape)] + [c.in_spec for c in casts],
        out_specs=[rows] + [c.out_spec for c in casts],
        out_shape=[jax.ShapeDtypeStruct((m, d), F32)] + [c.out_shape for c in casts],
        compiler_params=_params(("parallel",)),
        name="ssm_out_glu",
    )(z, x, w, *[c.operand for c in casts])


def _ffn_kernel(x_ref, g_ref, wg_ref, wu_ref, wd_ref, gf_ref, *rest, final_norm):
    n_cast = (len(rest) - 2) // 2
    o_ref, hn_ref = rest[n_cast], rest[-1]
    _cast_blocks(rest[:n_cast], rest[n_cast + 1:-1])
    h = pl.program_id(1)

    @pl.when(h == 0)
    def _():
        x = x_ref[...]
        hn_ref[...] = _rms(x, g_ref[...]).astype(BF16)
        o_ref[...] = x

    hn = hn_ref[...]
    act = (jax.nn.silu(_dot(hn, wg_ref[...])) * _dot(hn, wu_ref[...])).astype(BF16)
    o_ref[...] += _dot(act, wd_ref[...])

    if final_norm:
        @pl.when(h == pl.num_programs(1) - 1)
        def _():
            o_ref[...] = _rms(o_ref[...], gf_ref[...])


def _ffn(x, g, w_gate_up, w_down, g_final, *, tm, th, final_norm, casts=()):
    m, d = x.shape
    n_h = FFN_HIDDEN // th
    steps = (m // tm) * n_h
    rows = pl.BlockSpec((tm, d), lambda i, h: (i, 0))
    casts = [_SideCast(cw, layer, _slice_count(cw.shape[1], steps), lambda i, h: i * n_h + h) for cw, layer in casts]
    return pl.pallas_call(
        functools.partial(_ffn_kernel, final_norm=final_norm),
        grid=(m // tm, n_h),
        in_specs=[rows,
                  _resident((1, d)),
                  pl.BlockSpec((d, th), lambda i, h: (0, h)),
                  pl.BlockSpec((d, th), lambda i, h: (0, n_h + h)),
                  pl.BlockSpec((th, d), lambda i, h: (h, 0)),
                  _resident((1, d))] + [c.in_spec for c in casts],
        out_specs=[rows] + [c.out_spec for c in casts],
        out_shape=[jax.ShapeDtypeStruct((m, d), F32)] + [c.out_shape for c in casts],
        scratch_shapes=[pltpu.VMEM((tm, d), BF16)],
        compiler_params=pltpu.CompilerParams(dimension_semantics=("arbitrary", "arbitrary"),
                                             vmem_limit_bytes=BIG_VMEM_LIMIT),
        name="ffn_swiglu",
    )(x, g.reshape(1, d), w_gate_up, w_gate_up, w_down, g_final.reshape(1, d), *[c.operand for c in casts])


def _gmlp_gate_inputs(x, g, win_ref, v_gain):
    hn = _rms(x, g).astype(BF16)
    v = jax.nn.gelu(_dot(hn, win_ref[:, GMLP_WIDTH:]))
    vc = v - jnp.mean(v, axis=-1, keepdims=True)
    vn = (vc * lax.rsqrt(jnp.mean(vc * vc, axis=-1, keepdims=True) + EPS)) * v_gain
    u = jax.nn.gelu(_dot(hn, win_ref[:, :GMLP_WIDTH]))
    return u, vn


def _gmlp_prompt_kernel(x_ref, g_ref, win_ref, vg_ref, ws_ref, bs_ref, wout_ref, *rest):
    n_cast = (len(rest) - 2) // 2
    o_ref, gate_ref = rest[n_cast], rest[-1]
    _cast_blocks(rest[:n_cast], rest[n_cast + 1:-1])
    x = x_ref[...]
    u, vn = _gmlp_gate_inputs(x, g_ref[...], win_ref, vg_ref[...])
    vb = vn.astype(BF16)
    q_idx = lax.broadcasted_iota(jnp.int32, (CHUNK, CHUNK), 0)
    k_idx = lax.broadcasted_iota(jnp.int32, (CHUNK, CHUNK), 1)
    causal = k_idx <= q_idx
    for h in range(GMLP_HEADS):
        cols = slice(h * GMLP_HEAD_DIM, (h + 1) * GMLP_HEAD_DIM)
        ws = jnp.where(causal, ws_ref[h], 0.0).astype(BF16)
        bias = bs_ref[:, h:h + 1]
        for c in range(0, x.shape[0] // CHUNK, 2):
            lo = slice(c * CHUNK, (c + 1) * CHUNK)
            hi = slice((c + 1) * CHUNK, (c + 2) * CHUNK)
            s = _dot(ws, jnp.concatenate([vb[lo, cols], vb[hi, cols]], axis=1)) + bias
            gate_ref[lo, cols] = (u[lo, cols] * s[:, :GMLP_HEAD_DIM]).astype(BF16)
            gate_ref[hi, cols] = (u[hi, cols] * s[:, GMLP_HEAD_DIM:]).astype(BF16)
    o_ref[...] = x + _dot(gate_ref[...], wout_ref[...])


def _gmlp_sample_kernel(x_ref, g_ref, win_ref, vg_ref, wq_ref, bq_ref, wout_ref, o_ref, v_ref, *, steps):
    x = x_ref[...]
    u, vn = _gmlp_gate_inputs(x, g_ref[...], win_ref, vg_ref[...])
    v_ref[...] = vn
    nseq = x.shape[0] // steps
    gates = []
    for q in range(steps):
        s = bq_ref[q:q + 1, :]
        for k in range(q + 1):
            s = s + wq_ref[q * steps + k:q * steps + k + 1, :] * vn[k * nseq:(k + 1) * nseq, :]
        gates.append(u[q * nseq:(q + 1) * nseq, :] * s)
    gate = jnp.concatenate(gates, axis=0).astype(BF16)
    o_ref[...] = x + _dot(gate, wout_ref[...])


def _gmlp_prompt(x, g, w_in, v_gain, w_spatial, b_spatial_t, w_out, tm, casts=()):
    m, d = x.shape
    rows = pl.BlockSpec((tm, d), lambda i: (i, 0))
    casts = [_SideCast(cw, layer, m // tm) for cw, layer in casts]
    return pl.pallas_call(
        _gmlp_prompt_kernel,
        grid=(m // tm,),
        in_specs=[rows, _resident((1, d)), _resident(w_in.shape), _resident((1, GMLP_WIDTH)),
                  _resident(w_spatial.shape), _resident(b_spatial_t.shape), _resident(w_out.shape)]
                 + [c.in_spec for c in casts],
        out_specs=[rows] + [c.out_spec for c in casts],
        out_shape=[jax.ShapeDtypeStruct((m, d), F32)] + [c.out_shape for c in casts],
        scratch_shapes=[pltpu.VMEM((tm, GMLP_WIDTH), BF16)],
        compiler_params=_params(("parallel",)),
        name="gmlp_prompt",
    )(x, g.reshape(1, d), w_in, v_gain.reshape(1, GMLP_WIDTH), w_spatial, b_spatial_t, w_out,
      *[c.operand for c in casts])


def _gmlp_sample(x, g, w_in, v_gain, w_q, b_q, w_out, tm, steps):
    m, d = x.shape
    rows = pl.BlockSpec((tm, d), lambda i: (i, 0))
    return pl.pallas_call(
        functools.partial(_gmlp_sample_kernel, steps=steps),
        grid=(m // tm,),
        in_specs=[rows, _resident((1, d)), _resident(w_in.shape), _resident((1, GMLP_WIDTH)),
                  _resident(w_q.shape), _resident(b_q.shape), _resident(w_out.shape)],
        out_specs=(rows, pl.BlockSpec((tm, GMLP_WIDTH), lambda i: (i, 0))),
        out_shape=(jax.ShapeDtypeStruct((m, d), F32), jax.ShapeDtypeStruct((m, GMLP_WIDTH), F32)),
        compiler_params=_params(("parallel",)),
        name="gmlp_sample",
    )(x, g.reshape(1, d), w_in, v_gain.reshape(1, GMLP_WIDTH), w_q, b_q, w_out)


ROW_TILE = 512
FFN_ROW_TILE = 1024
FFN_HIDDEN_TILE = 512
PROMPT_SCAN_STEPS = 64
SAMPLE_TILE_SEQS = 64


def kernel(x_prompt, x_sample, state_ssm_re, state_ssm_im, norm_mix, norm_ffn, norm_final, ssm_w_in, ssm_lambda_re, ssm_lambda_im, ssm_log_dt, ssm_b_re, ssm_b_im, ssm_c_re, ssm_c_im, ssm_d, ssm_w_out, gmlp_w_in, gmlp_v_gain, gmlp_w_spatial, gmlp_b_spatial, gmlp_w_out, ffn_w_gate_up, ffn_w_down):
    bsz, seq, d = x_prompt.shape
    dbsz, dseq, _ = x_sample.shape

    tables = _ssm_tables(ssm_lambda_re[0], ssm_lambda_im[0], ssm_log_dt[0], ssm_b_re[0], ssm_b_im[0],
                         ssm_c_re[0], ssm_c_im[0])

    def ffn(x_rows, layer, w_gu, w_dn, ffn_tm, casts=()):
        return _ffn(x_rows, norm_ffn[layer], w_gu, w_dn, norm_final, tm=ffn_tm, th=FFN_HIDDEN_TILE,
                    final_norm=layer == 1, casts=casts)

    xp = x_prompt.reshape(bsz * seq, d)
    u, w_out0 = _norm_matmul(xp, norm_mix[0], ssm_w_in, 0, ROW_TILE, casts=[(ssm_w_out, 0)])
    z, p_re, p_im, w_gu0, g_w_in = _ssm_scan_split(u.reshape(bsz, seq, d), tables, ssm_d[0], PROMPT_SCAN_STEPS,
                                                   casts=[(ffn_w_gate_up, 0), (gmlp_w_in, 0)])
    xp1, w_dn0, g_w_out = _ssm_out(z.reshape(bsz * seq, d), xp, w_out0, ROW_TILE,
                                   casts=[(ffn_w_down, 0), (gmlp_w_out, 0)])
    xp2, w_gu1, w_dn1 = ffn(xp1, 0, w_gu0, w_dn0, FFN_ROW_TILE, casts=[(ffn_w_gate_up, 1), (ffn_w_down, 1)])
    xp3, = _gmlp_prompt(xp2, norm_mix[1], g_w_in, gmlp_v_gain[0], gmlp_w_spatial[0],
                        jnp.transpose(gmlp_b_spatial[0]), g_w_out, ROW_TILE)
    y_prompt = ffn(xp3, 1, w_gu1, w_dn1, FFN_ROW_TILE)[0].reshape(bsz, seq, d)

    n_bt = dbsz // SAMPLE_TILE_SEQS
    tile_rows = SAMPLE_TILE_SEQS * dseq

    def to_rows(a):
        return jnp.transpose(a.reshape(n_bt, SAMPLE_TILE_SEQS, dseq, d), (0, 2, 1, 3)).reshape(dbsz * dseq, d)

    def from_rows(a):
        return jnp.transpose(a.reshape(n_bt, dseq, SAMPLE_TILE_SEQS, d), (0, 2, 1, 3)).reshape(dbsz, dseq, d)

    xs = to_rows(x_sample)
    h0_re = state_ssm_re[0].reshape(dbsz, SSM_LANES)
    h0_im = state_ssm_im[0].reshape(dbsz, SSM_LANES)
    us, = _norm_matmul(xs, norm_mix[0], ssm_w_in, 0, dbsz * dseq)
    zs, s_re, s_im = _ssm_scan(us, tables, ssm_d[0], h0_re, h0_im, nb=SAMPLE_TILE_SEQS, tt=dseq, n_bt=n_bt, n_tt=1)
    xs1, = _ssm_out(zs, xs, w_out0, tile_rows)
    xs2, = ffn(xs1, 0, w_gu0, w_dn0, dbsz * dseq)
    w_q = jnp.repeat(gmlp_w_spatial[0][:, :dseq, :dseq].reshape(GMLP_HEADS, dseq * dseq).T, GMLP_HEAD_DIM, axis=1)
    b_q = jnp.repeat(gmlp_b_spatial[0][:, :dseq].T, GMLP_HEAD_DIM, axis=1)
    xs3, v_rows = _gmlp_sample(xs2, norm_mix[1], g_w_in, gmlp_v_gain[0], w_q, b_q, g_w_out, tile_rows, dseq)
    y_sample = from_rows(ffn(xs3, 1, w_gu1, w_dn1, dbsz * dseq)[0])

    state_shape = (1, -1, SSM_GROUPS, SSM_STATE)
    return (y_prompt, y_sample,
            p_re.reshape(state_shape), p_im.reshape(state_shape),
            s_re.reshape(state_shape), s_im.reshape(state_shape),
            from_rows(v_rows)[None])
```

```python
import functools

import jax
import jax.numpy as jnp
from jax import lax
from jax.experimental import pallas as pl
from jax.experimental.pallas import tpu as pltpu

D_MODEL = 2048
SSM_GROUPS = 128
SSM_GROUP = 16
SSM_STATE = 64
SSM_LANES = SSM_GROUPS * SSM_STATE
GMLP_WIDTH = D_MODEL
GMLP_HEADS = 16
GMLP_HEAD_DIM = GMLP_WIDTH // GMLP_HEADS
CHUNK = 128
FFN_HIDDEN = 5632
EPS = 1e-6

LANES = 128
SUBLANES = 8
MXU_DIM = 256
VMEM_LIMIT = 56 * 1024 * 1024
BIG_VMEM_LIMIT = 60 * 1024 * 1024

PAIRS = SSM_GROUPS // 2
QUAD_K = 4 * 2 * SSM_GROUP
QUADS = PAIRS // 4
HALF_SLABS = PAIRS // 2
OCTS = D_MODEL // MXU_DIM
OCT_K = SSM_LANES // OCTS

BF16 = jnp.bfloat16
F32 = jnp.float32


def _resident(shape):
    zeros = (0,) * len(shape)
    return pl.BlockSpec(shape, lambda *_: zeros, pipeline_mode=pl.Buffered(1))


def _params(semantics):
    return pltpu.CompilerParams(dimension_semantics=semantics, vmem_limit_bytes=VMEM_LIMIT)


def _rms(x, g):
    ms = jnp.mean(x * x, axis=-1, keepdims=True)
    return (x * lax.rsqrt(ms + EPS)) * g


def _dot(a, b):
    return jnp.dot(a, b, preferred_element_type=F32)


BF16_ROWS = 16


class _SideCast:
    def __init__(self, w, layer, n_blocks, step_of=lambda i: i):
        _, r, c = w.shape
        assert r % (n_blocks * BF16_ROWS) == 0

        def block(*idx):
            return jnp.minimum(step_of(*idx), n_blocks - 1)

        self.operand = w
        self.in_spec = pl.BlockSpec((None, r // n_blocks, c), lambda *idx: (layer, block(*idx), 0))
        self.out_spec = pl.BlockSpec((r // n_blocks, c), lambda *idx: (block(*idx), 0))
        self.out_shape = jax.ShapeDtypeStruct((r, c), BF16)


def _cast_blocks(src_refs, dst_refs):
    for src, dst in zip(src_refs, dst_refs):
        dst[...] = src[...].astype(BF16)


def _slice_count(rows, steps):
    return max(n for n in range(1, steps + 1) if rows % (n * BF16_ROWS) == 0)


STREAM_COLS = 512


def _col_chunks(start, stop):
    return [slice(c, c + STREAM_COLS) for c in range(start, stop, STREAM_COLS)]


def _chunk_copy(w_hbm, w_vmem, sems, chunks, n):
    return pltpu.make_async_copy(w_hbm.at[:, chunks[n]], w_vmem.at[:, chunks[n]], sems.at[n])


def _first_step_or_not(body):
    first = pl.program_id(0) == 0
    pl.when(first)(functools.partial(body, True))
    pl.when(jnp.logical_not(first))(functools.partial(body, False))


def _zoh(lr, li, log_dt):
    dt = jnp.exp(log_dt)
    mag = jnp.exp(lr * dt)
    a_re = mag * jnp.cos(li * dt)
    a_im = mag * jnp.sin(li * dt)
    den = lr * lr + li * li
    nr = a_re - 1.0
    ni = a_im
    q_re = (nr * lr + ni * li) / den
    q_im = (ni * lr - nr * li) / den
    return a_re, a_im, q_re, q_im


def _ssm_tables_kernel(lr_ref, li_ref, ldt_ref, br_ref, bi_ref, cr_ref, ci_ref,
                       are_ref, aim_ref, wb_ref, wcr_ref, wci_ref):
    c, p = SSM_GROUP, SSM_STATE
    a_re, a_im, q_re, q_im = _zoh(lr_ref[...], li_ref[...], ldt_ref[...])
    are_ref[...] = a_re
    aim_ref[...] = a_im

    def per_channel(a):
        return jnp.broadcast_to(a[:, None, :], (SSM_GROUPS, c, p)).reshape(SSM_GROUPS * c, p)

    q_re, q_im = per_channel(q_re), per_channel(q_im)
    br = br_ref[...]
    bi = bi_ref[...]
    bb_re = (q_re * br - q_im * bi).astype(BF16)
    bb_im = (q_re * bi + q_im * br).astype(BF16)

    src = lax.broadcasted_iota(jnp.int32, (2 * p, 2 * MXU_DIM), 0)
    dst = lax.broadcasted_iota(jnp.int32, (2 * p, 2 * MXU_DIM), 1)
    spread = ((src % p == dst % p) & (src // p == (dst % MXU_DIM) // LANES)).astype(BF16)
    row = lax.broadcasted_iota(jnp.int32, (2 * QUAD_K, 2 * MXU_DIM), 0)
    col = lax.broadcasted_iota(jnp.int32, (2 * QUAD_K, 2 * MXU_DIM), 1)
    hit = (row % QUAD_K) // c == 4 * (row // QUAD_K) + 2 * (col // MXU_DIM) + (col % LANES) // p
    for q in range(QUADS):
        rows = slice(QUAD_K * q, QUAD_K * (q + 1))
        both = jnp.concatenate([bb_re[rows, :], bb_im[rows, :]], axis=1)
        tile = _dot(both, spread)
        wb_ref[q] = jnp.where(hit, jnp.concatenate([tile, tile], axis=0), 0.0).astype(BF16)

    groups = MXU_DIM // c
    eye = (lax.broadcasted_iota(jnp.int32, (MXU_DIM, MXU_DIM), 0)
           == lax.broadcasted_iota(jnp.int32, (MXU_DIM, MXU_DIM), 1)).astype(BF16)
    same_group = (lax.broadcasted_iota(jnp.int32, (OCT_K, MXU_DIM), 0) // p
                  == lax.broadcasted_iota(jnp.int32, (OCT_K, MXU_DIM), 1) // c)
    transpose_lhs = (((0,), (0,)), ((), ()))
    for o in range(OCTS):
        rows = slice(MXU_DIM * o, MXU_DIM * (o + 1))
        for src_ref, dst_ref, sign in ((cr_ref, wcr_ref, 1.0), (ci_ref, wci_ref, -1.0)):
            blk = (sign * src_ref[rows, :]).astype(BF16)
            by_state = lax.dot_general(blk, eye, transpose_lhs, preferred_element_type=F32)
            dst_ref[o] = jnp.where(same_group, jnp.concatenate([by_state] * groups, axis=0), 0.0).astype(BF16)


def _ssm_tables(lambda_re, lambda_im, log_dt, b_re, b_im, c_re, c_im):
    g, p, c = b_re.shape
    rows = (g * c, p)
    a_shape = jax.ShapeDtypeStruct((g, p), F32)
    wc_shape = jax.ShapeDtypeStruct((OCTS, OCT_K, MXU_DIM), BF16)
    a_re, a_im, w_b, w_cr, w_ci = pl.pallas_call(
        _ssm_tables_kernel,
        out_shape=(a_shape, a_shape, jax.ShapeDtypeStruct((QUADS, 2 * QUAD_K, 2 * MXU_DIM), BF16), wc_shape, wc_shape),
        compiler_params=pltpu.CompilerParams(vmem_limit_bytes=VMEM_LIMIT),
        name="ssm_tables",
    )(lambda_re, lambda_im, jnp.broadcast_to(log_dt[:, None], (g, p)),
      jnp.transpose(b_re, (0, 2, 1)).reshape(rows), jnp.transpose(b_im, (0, 2, 1)).reshape(rows),
      c_re.reshape(rows), c_im.reshape(rows))
    a_re_t = jnp.broadcast_to(a_re.reshape(1, SSM_LANES), (SUBLANES, SSM_LANES))
    a_im_t = jnp.broadcast_to(a_im.reshape(1, SSM_LANES), (SUBLANES, SSM_LANES))
    return (w_b, a_re_t, a_im_t, w_cr, w_ci)


def _norm_matmul_kernel(x_ref, g_ref, w_ref, *rest):
    n_cast = (len(rest) - 2) // 2
    o_ref, wb_ref = rest[n_cast], rest[-1]
    _cast_blocks(rest[:n_cast], rest[n_cast + 1:-1])

    @pl.when(pl.program_id(0) == 0)
    def _():
        wb_ref[...] = w_ref[...].astype(BF16)

    half = x_ref.shape[0] // 2
    for rows in (slice(0, half), slice(half, 2 * half)):
        hn = _rms(x_ref[rows, :], g_ref[...]).astype(BF16)
        o_ref[rows, :] = _dot(hn, wb_ref[...])


def _norm_matmul(x, g, w, layer, tm, casts=()):
    m, d = x.shape
    n = w.shape[2]
    casts = [_SideCast(cw, cl, m // tm) for cw, cl in casts]
    return pl.pallas_call(
        _norm_matmul_kernel,
        grid=(m // tm,),
        in_specs=[pl.BlockSpec((tm, d), lambda i: (i, 0)), _resident((1, d)),
                  pl.BlockSpec((None, d, n), lambda i: (layer, 0, 0), pipeline_mode=pl.Buffered(1))]
                 + [c.in_spec for c in casts],
        out_specs=[pl.BlockSpec((tm, n), lambda i: (i, 0))] + [c.out_spec for c in casts],
        out_shape=[jax.ShapeDtypeStruct((m, n), F32)] + [c.out_shape for c in casts],
        scratch_shapes=[pltpu.VMEM((d, n), BF16)],
        compiler_params=_params(("arbitrary",)),
        name="ssm_norm_in_proj",
    )(x, g.reshape(1, d), w, *[c.operand for c in casts])


SCAN_COLS = 1024


def _ssm_scan_kernel(u_ref, wb_ref, are_ref, aim_ref, wcr_ref, wci_ref, dsk_ref, h0r_ref, h0i_ref,
                     z_ref, sr_ref, si_ref, bur_ref, bui_ref, *, nb, tt):
    @pl.when(pl.program_id(1) == 0)
    def _():
        sr_ref[...] = h0r_ref[...]
        si_ref[...] = h0i_ref[...]

    u = u_ref[...]
    ub = u.astype(BF16)
    for q in range(QUADS):
        lhs = ub[:, QUAD_K * q:QUAD_K * (q + 1)]
        for h in range(2):
            res = _dot(lhs, wb_ref[q, QUAD_K * h:QUAD_K * (h + 1), :])
            for s in range(2):
                j = 4 * q + 2 * h + s
                bur_ref[:, LANES * j:LANES * (j + 1)] = res[:, MXU_DIM * s:MXU_DIM * s + LANES]
                bui_ref[:, LANES * j:LANES * (j + 1)] = res[:, MXU_DIM * s + LANES:MXU_DIM * (s + 1)]

    for cb in range(SSM_LANES // SCAN_COLS):
        cols = slice(cb * SCAN_COLS, (cb + 1) * SCAN_COLS)
        ar = are_ref[:, cols]
        ai = aim_ref[:, cols]
        for sg in range(nb // SUBLANES):
            seqs = slice(sg * SUBLANES, (sg + 1) * SUBLANES)
            sr = sr_ref[seqs, cols]
            si = si_ref[seqs, cols]
            for t in range(tt):
                rows = slice(t * nb + sg * SUBLANES, t * nb + (sg + 1) * SUBLANES)
                nr = ar * sr - ai * si + bur_ref[rows, cols]
                ni = ar * si + ai * sr + bui_ref[rows, cols]
                sr, si = nr, ni
                bur_ref[rows, cols] = sr
                bui_ref[rows, cols] = si
            sr_ref[seqs, cols] = sr
            si_ref[seqs, cols] = si

    for o in range(OCTS):
        kk = slice(o * OCT_K, (o + 1) * OCT_K)
        nn = slice(o * MXU_DIM, (o + 1) * MXU_DIM)
        y = _dot(bur_ref[:, kk].astype(BF16), wcr_ref[o]) + _dot(bui_ref[:, kk].astype(BF16), wci_ref[o])
        y = y + dsk_ref[:, nn] * u[:, nn]
        z_ref[:, nn] = jax.nn.gelu(y).astype(BF16)


def _ssm_scan(u, tables, d_skip, h0_re, h0_im, *, nb, tt, n_bt, n_tt):
    w_b, a_re_t, a_im_t, w_cr, w_ci = tables
    m, d = u.shape
    r = nb * tt
    assert m == r * n_bt * n_tt and nb % SUBLANES == 0
    rows = pl.BlockSpec((r, d), lambda b, t: (b * n_tt + t, 0))
    state = pl.BlockSpec((nb, SSM_LANES), lambda b, t: (b, 0))
    st_shape = jax.ShapeDtypeStruct((nb * n_bt, SSM_LANES), F32)
    return pl.pallas_call(
        functools.partial(_ssm_scan_kernel, nb=nb, tt=tt),
        grid=(n_bt, n_tt),
        in_specs=[rows, _resident(w_b.shape), _resident(a_re_t.shape), _resident(a_im_t.shape),
                  _resident(w_cr.shape), _resident(w_ci.shape), _resident((1, d)), state, state],
        out_specs=(rows, state, state),
        out_shape=(jax.ShapeDtypeStruct((m, d), BF16), st_shape, st_shape),
        scratch_shapes=[pltpu.VMEM((r, SSM_LANES), F32), pltpu.VMEM((r, SSM_LANES), F32)],
        compiler_params=_params(("parallel", "arbitrary")),
        name="ssm_scan",
    )(u, w_b, a_re_t, a_im_t, w_cr, w_ci, d_skip.reshape(1, d), h0_re, h0_im)


SCAN_SLABS = 8


def _ssm_scan_split_kernel(u_ref, wb_ref, are_ref, aim_ref, wcr_ref, wci_ref, dsk_ref, *rest, nb, tt):
    n_cast = (len(rest) - 7) // 2
    z_ref, sr_ref, si_ref = rest[n_cast:n_cast + 3]
    lhs_ref, bur_ref, bui_ref, y_ref = rest[-4:]
    _cast_blocks(rest[:n_cast], rest[n_cast + 3:-4])
    r = nb * tt
    i = pl.program_id(0)

    @pl.when(i == 0)
    def _():
        lhs_ref[...] = jnp.zeros_like(lhs_ref)
        sr_ref[...] = jnp.zeros_like(sr_ref)
        si_ref[...] = jnp.zeros_like(si_ref)

    def project_in(q):
        for b in range(nb):
            blk = u_ref[b, :, LANES * q:LANES * (q + 1)]
            for h in range(2):
                lhs_ref[q, h, pl.ds(2 * b + h, tt, stride=2 * nb), :] = blk
        lhs = jnp.concatenate([lhs_ref[q, 0], lhs_ref[q, 1]], axis=1).astype(BF16)
        res = _dot(lhs, wb_ref[q])
        for s in range(2):
            bur_ref[2 * q + s] = res[:, MXU_DIM * s:MXU_DIM * s + LANES]
            bui_ref[2 * q + s] = res[:, MXU_DIM * s + LANES:MXU_DIM * (s + 1)]

    def recur(slabs):
        ar = [are_ref[:, LANES * k:LANES * (k + 1)] for k in slabs]
        ai = [aim_ref[:, LANES * k:LANES * (k + 1)] for k in slabs]
        sr = [sr_ref[:, LANES * k:LANES * (k + 1)] for k in slabs]
        si = [si_ref[:, LANES * k:LANES * (k + 1)] for k in slabs]
        for t in range(tt):
            rows = slice(SUBLANES * t, SUBLANES * (t + 1))
            for n, k in enumerate(slabs):
                nr = ar[n] * sr[n] - ai[n] * si[n] + bur_ref[k, rows, :]
                ni = ar[n] * si[n] + ai[n] * sr[n] + bui_ref[k, rows, :]
                sr[n], si[n] = nr, ni
                bur_ref[k, rows, :] = nr
                bui_ref[k, rows, :] = ni
        for n, k in enumerate(slabs):
            sr_ref[:, LANES * k:LANES * (k + 1)] = sr[n]
            si_ref[:, LANES * k:LANES * (k + 1)] = si[n]

    def project_out(o):
        order = [(2 * (2 * o + ql) + s, h) for ql in range(2) for h in range(2) for s in range(2)]
        lre = jnp.concatenate([bur_ref[k, pl.ds(h, r, stride=2), :] for k, h in order], axis=1).astype(BF16)
        lim = jnp.concatenate([bui_ref[k, pl.ds(h, r, stride=2), :] for k, h in order], axis=1).astype(BF16)
        y = _dot(lre, wcr_ref[o]) + _dot(lim, wci_ref[o])
        for half in range(2):
            y_ref[half] = y[:, LANES * half:LANES * (half + 1)]
        for b in range(nb):
            for half in range(2):
                cols = slice(MXU_DIM * o + LANES * half, MXU_DIM * o + LANES * (half + 1))
                yb = y_ref[half, pl.ds(b, tt, stride=nb), :] + dsk_ref[:, cols] * u_ref[b, :, cols]
                z_ref[b, :, cols] = jax.nn.gelu(yb).astype(BF16)

    for q in range(QUADS):
        project_in(q)
    for k0 in range(0, HALF_SLABS, SCAN_SLABS):
        recur(range(k0, k0 + SCAN_SLABS))
    for o in range(OCTS):
        project_out(o)


def _ssm_scan_split(u, tables, d_skip, tt, casts):
    w_b, a_re_t, a_im_t, w_cr, w_ci = tables
    nb, seq, d = u.shape
    assert 2 * nb == SUBLANES and seq % tt == 0
    casts = [_SideCast(cw, layer, seq // tt) for cw, layer in casts]

    def split_lanes(a):
        halves = jnp.transpose(a[0].reshape(QUADS, 2, 2 * LANES), (1, 0, 2)).reshape(2, HALF_SLABS * LANES)
        return jnp.tile(halves, (nb, 1))

    rows = pl.BlockSpec((nb, tt, d), lambda i: (0, i, 0))
    st_shape = jax.ShapeDtypeStruct((SUBLANES, HALF_SLABS * LANES), F32)
    r = nb * tt
    state = pl.BlockSpec(st_shape.shape, lambda i: (0, 0))
    z, s_re, s_im, *w_cast = pl.pallas_call(
        functools.partial(_ssm_scan_split_kernel, nb=nb, tt=tt),
        grid=(seq // tt,),
        in_specs=[rows, _resident(w_b.shape), _resident(st_shape.shape), _resident(st_shape.shape),
                  _resident(w_cr.shape), _resident(w_ci.shape), _resident((1, d))] + [c.in_spec for c in casts],
        out_specs=[rows, state, state] + [c.out_spec for c in casts],
        out_shape=[jax.ShapeDtypeStruct((nb, seq, d), BF16), st_shape, st_shape] + [c.out_shape for c in casts],
        scratch_shapes=[pltpu.VMEM((QUADS, 2, 2 * r, LANES), F32),
                        pltpu.VMEM((HALF_SLABS, 2 * r, LANES), F32),
                        pltpu.VMEM((HALF_SLABS, 2 * r, LANES), F32),
                        pltpu.VMEM((2, r, LANES), F32)],
        compiler_params=pltpu.CompilerParams(dimension_semantics=("arbitrary",), vmem_limit_bytes=BIG_VMEM_LIMIT),
        name="ssm_scan_split",
    )(u, w_b, split_lanes(a_re_t), split_lanes(a_im_t), w_cr, w_ci, d_skip.reshape(1, d),
      *[c.operand for c in casts])

    def join(s):
        return jnp.transpose(s.reshape(nb, 2, QUADS, 2 * LANES), (0, 2, 1, 3)).reshape(nb, SSM_LANES)

    return (z, join(s_re), join(s_im), *w_cast)


def _ssm_out_kernel(z_ref, x_ref, w_hbm, *rest):
    n_cast = (len(rest) - 3) // 2
    o_ref, w_ref, sems = rest[n_cast], rest[-2], rest[-1]
    _cast_blocks(rest[:n_cast], rest[n_cast + 1:-2])
    chunks = [cols for pair in zip(_col_chunks(0, D_MODEL), _col_chunks(D_MODEL, 2 * D_MODEL)) for cols in pair]

    def compute(first):
        if first:
            for n in range(len(chunks)):
                _chunk_copy(w_hbm, w_ref, sems, chunks, n).start()
        z = z_ref[...]
        for c in range(len(chunks) // 2):
            if first:
                _chunk_copy(w_hbm, w_ref, sems, chunks, 2 * c).wait()
                _chunk_copy(w_hbm, w_ref, sems, chunks, 2 * c + 1).wait()
            cols = chunks[2 * c]
            val = _dot(z, w_ref[:, cols])
            gate = _dot(z, w_ref[:, chunks[2 * c + 1]])
            o_ref[:, cols] = x_ref[:, cols] + val * jax.nn.sigmoid(gate)

    _first_step_or_not(compute)


def _ssm_out(z, x, w, tm, casts=()):
    m, d = x.shape
    rows = pl.BlockSpec((tm, d), lambda i: (i, 0))
    casts = [_SideCast(cw, layer, m // tm) for cw, layer in casts]
    return pl.pallas_call(
        _ssm_out_kernel,
        grid=(m // tm,),
        in_specs=[rows, rows, pl.BlockSpec(memory_space=pl.ANY)] + [c.in_spec for c in casts],
        out_specs=[rows] + [c.out_spec for c in casts],
        out_shape=[jax.ShapeDtypeStruct((m, d), F32)] + [c.out_shape for c in casts],
        scratch_shapes=[pltpu.VMEM(w.shape, BF16), pltpu.SemaphoreType.DMA((w.shape[1] // STREAM_COLS,))],
        compiler_params=_params(("arbitrary",)),
        name="ssm_out_glu",
    )(z, x, w, *[c.operand for c in casts])


def _ffn_kernel(x_ref, g_ref, wg_ref, wu_ref, wd_ref, gf_ref, *rest, final_norm):
    n_cast = (len(rest) - 2) // 2
    o_ref, hn_ref = rest[n_cast], rest[-1]
    _cast_blocks(rest[:n_cast], rest[n_cast + 1:-1])
    h = pl.program_id(1)

    @pl.when(h == 0)
    def _():
        x = x_ref[...]
        hn_ref[...] = _rms(x, g_ref[...]).astype(BF16)
        o_ref[...] = x

    hn = hn_ref[...]
    act = (jax.nn.silu(_dot(hn, wg_ref[...])) * _dot(hn, wu_ref[...])).astype(BF16)
    o_ref[...] += _dot(act, wd_ref[...])

    if final_norm:
        @pl.when(h == pl.num_programs(1) - 1)
        def _():
            o_ref[...] = _rms(o_ref[...], gf_ref[...])


def _ffn(x, g, w_gate_up, w_down, g_final, *, tm, th, final_norm, casts=()):
    m, d = x.shape
    n_h = FFN_HIDDEN // th
    steps = (m // tm) * n_h
    rows = pl.BlockSpec((tm, d), lambda i, h: (i, 0))
    casts = [_SideCast(cw, layer, _slice_count(cw.shape[1], steps), lambda i, h: i * n_h + h) for cw, layer in casts]
    return pl.pallas_call(
        functools.partial(_ffn_kernel, final_norm=final_norm),
        grid=(m // tm, n_h),
        in_specs=[rows,
                  _resident((1, d)),
                  pl.BlockSpec((d, th), lambda i, h: (0, h)),
                  pl.BlockSpec((d, th), lambda i, h: (0, n_h + h)),
                  pl.BlockSpec((th, d), lambda i, h: (h, 0)),
                  _resident((1, d))] + [c.in_spec for c in casts],
        out_specs=[rows] + [c.out_spec for c in casts],
        out_shape=[jax.ShapeDtypeStruct((m, d), F32)] + [c.out_shape for c in casts],
        scratch_shapes=[pltpu.VMEM((tm, d), BF16)],
        compiler_params=pltpu.CompilerParams(dimension_semantics=("arbitrary", "arbitrary"),
                                             vmem_limit_bytes=BIG_VMEM_LIMIT),
        name="ffn_swiglu",
    )(x, g.reshape(1, d), w_gate_up, w_gate_up, w_down, g_final.reshape(1, d), *[c.operand for c in casts])


class _Streamed:
    def __init__(self, w_hbm, w_ref, sems, chunks):
        self.w_hbm, self.w_ref, self.sems, self.chunks = w_hbm, w_ref, sems, chunks

    def start(self):
        for n in range(len(self.chunks)):
            _chunk_copy(self.w_hbm, self.w_ref, self.sems, self.chunks, n).start()

    def dot(self, lhs, which, first):
        if not first:
            return _dot(lhs, self.w_ref[:, self.chunks[which[0]].start:self.chunks[which[-1]].stop])
        parts = []
        for n in which:
            _chunk_copy(self.w_hbm, self.w_ref, self.sems, self.chunks, n).wait()
            parts.append(_dot(lhs, self.w_ref[:, self.chunks[n]]))
        return jnp.concatenate(parts, axis=1)


GMLP_IN_CHUNKS = [slice(GMLP_WIDTH, 3 * GMLP_WIDTH // 2), slice(3 * GMLP_WIDTH // 2, 2 * GMLP_WIDTH),
                  slice(0, GMLP_WIDTH)]
GMLP_V_CHUNKS = (0, 1)
GMLP_U_CHUNKS = (2,)
GMLP_OUT_CHUNKS = [slice(0, D_MODEL)]


def _gmlp_weights(win_hbm, wout_hbm, win_ref, wout_ref, in_sems, out_sems):
    return (_Streamed(win_hbm, win_ref, in_sems, GMLP_IN_CHUNKS),
            _Streamed(wout_hbm, wout_ref, out_sems, GMLP_OUT_CHUNKS))


def _gmlp_weight_scratch(w_in, w_out):
    return [pltpu.VMEM(w_in.shape, BF16), pltpu.VMEM(w_out.shape, BF16),
            pltpu.SemaphoreType.DMA((len(GMLP_IN_CHUNKS),)), pltpu.SemaphoreType.DMA((len(GMLP_OUT_CHUNKS),))]


def _gmlp_gate_inputs(x, g, w_in, v_gain, first):
    hn = _rms(x, g).astype(BF16)
    v = jax.nn.gelu(w_in.dot(hn, GMLP_V_CHUNKS, first))
    vc = v - jnp.mean(v, axis=-1, keepdims=True)
    vn = (vc * lax.rsqrt(jnp.mean(vc * vc, axis=-1, keepdims=True) + EPS)) * v_gain
    u = jax.nn.gelu(w_in.dot(hn, GMLP_U_CHUNKS, first))
    return u, vn


def _gmlp_prompt_kernel(x_ref, g_ref, win_hbm, vg_ref, ws_ref, bs_ref, wout_hbm, *rest):
    n_cast = (len(rest) - 6) // 2
    o_ref, gate_ref = rest[n_cast], rest[-5]
    _cast_blocks(rest[:n_cast], rest[n_cast + 1:-5])
    w_in, w_out = _gmlp_weights(win_hbm, wout_hbm, *rest[-4:])

    def compute(first):
        if first:
            w_in.start()
            w_out.start()
        x = x_ref[...]
        u, vn = _gmlp_gate_inputs(x, g_ref[...], w_in, vg_ref[...], first)
        vb = vn.astype(BF16)
        q_idx = lax.broadcasted_iota(jnp.int32, (CHUNK, CHUNK), 0)
        k_idx = lax.broadcasted_iota(jnp.int32, (CHUNK, CHUNK), 1)
        causal = k_idx <= q_idx
        for h in range(GMLP_HEADS):
            cols = slice(h * GMLP_HEAD_DIM, (h + 1) * GMLP_HEAD_DIM)
            ws = jnp.where(causal, ws_ref[h], 0.0).astype(BF16)
            bias = bs_ref[:, h:h + 1]
            for c in range(0, x.shape[0] // CHUNK, 2):
                lo = slice(c * CHUNK, (c + 1) * CHUNK)
                hi = slice((c + 1) * CHUNK, (c + 2) * CHUNK)
                s = _dot(ws, jnp.concatenate([vb[lo, cols], vb[hi, cols]], axis=1)) + bias
                gate_ref[lo, cols] = (u[lo, cols] * s[:, :GMLP_HEAD_DIM]).astype(BF16)
                gate_ref[hi, cols] = (u[hi, cols] * s[:, GMLP_HEAD_DIM:]).astype(BF16)
        o_ref[...] = x + w_out.dot(gate_ref[...], range(len(w_out.chunks)), first)

    _first_step_or_not(compute)


def _gmlp_sample_kernel(x_ref, g_ref, win_hbm, vg_ref, wq_ref, bq_ref, wout_hbm, o_ref, v_ref,
                        win_ref, wout_ref, in_sems, out_sems, *, steps):
    w_in, w_out = _gmlp_weights(win_hbm, wout_hbm, win_ref, wout_ref, in_sems, out_sems)

    def compute(first):
        if first:
            w_in.start()
            w_out.start()
        x = x_ref[...]
        u, vn = _gmlp_gate_inputs(x, g_ref[...], w_in, vg_ref[...], first)
        v_ref[...] = vn
        nseq = x.shape[0] // steps
        gates = []
        for q in range(steps):
            s = bq_ref[q:q + 1, :]
            for k in range(q + 1):
                s = s + wq_ref[q * steps + k:q * steps + k + 1, :] * vn[k * nseq:(k + 1) * nseq, :]
            gates.append(u[q * nseq:(q + 1) * nseq, :] * s)
        gate = jnp.concatenate(gates, axis=0).astype(BF16)
        o_ref[...] = x + w_out.dot(gate, range(len(w_out.chunks)), first)

    _first_step_or_not(compute)


def _gmlp_prompt(x, g, w_in, v_gain, w_spatial, b_spatial_t, w_out, tm, casts=()):
    m, d = x.shape
    rows = pl.BlockSpec((tm, d), lambda i: (i, 0))
    casts = [_SideCast(cw, layer, m // tm) for cw, layer in casts]
    return pl.pallas_call(
        _gmlp_prompt_kernel,
        grid=(m // tm,),
        in_specs=[rows, _resident((1, d)), pl.BlockSpec(memory_space=pl.ANY), _resident((1, GMLP_WIDTH)),
                  _resident(w_spatial.shape), _resident(b_spatial_t.shape), pl.BlockSpec(memory_space=pl.ANY)]
                 + [c.in_spec for c in casts],
        out_specs=[rows] + [c.out_spec for c in casts],
        out_shape=[jax.ShapeDtypeStruct((m, d), F32)] + [c.out_shape for c in casts],
        scratch_shapes=[pltpu.VMEM((tm, GMLP_WIDTH), BF16)] + _gmlp_weight_scratch(w_in, w_out),
        compiler_params=_params(("arbitrary",)),
        name="gmlp_prompt",
    )(x, g.reshape(1, d), w_in, v_gain.reshape(1, GMLP_WIDTH), w_spatial, b_spatial_t, w_out,
      *[c.operand for c in casts])


def _gmlp_sample(x, g, w_in, v_gain, w_q, b_q, w_out, tm, steps):
    m, d = x.shape
    rows = pl.BlockSpec((tm, d), lambda i: (i, 0))
    return pl.pallas_call(
        functools.partial(_gmlp_sample_kernel, steps=steps),
        grid=(m // tm,),
        in_specs=[rows, _resident((1, d)), pl.BlockSpec(memory_space=pl.ANY), _resident((1, GMLP_WIDTH)),
                  _resident(w_q.shape), _resident(b_q.shape), pl.BlockSpec(memory_space=pl.ANY)],
        out_specs=(rows, pl.BlockSpec((tm, GMLP_WIDTH), lambda i: (i, 0))),
        out_shape=(jax.ShapeDtypeStruct((m, d), F32), jax.ShapeDtypeStruct((m, GMLP_WIDTH), F32)),
        scratch_shapes=_gmlp_weight_scratch(w_in, w_out),
        compiler_params=_params(("arbitrary",)),
        name="gmlp_sample",
    )(x, g.reshape(1, d), w_in, v_gain.reshape(1, GMLP_WIDTH), w_q, b_q, w_out)


ROW_TILE = 512
FFN_ROW_TILE = 1024
FFN_HIDDEN_TILE = 512
PROMPT_SCAN_STEPS = 64
SAMPLE_TILE_SEQS = 64


def kernel(x_prompt, x_sample, state_ssm_re, state_ssm_im, norm_mix, norm_ffn, norm_final, ssm_w_in, ssm_lambda_re, ssm_lambda_im, ssm_log_dt, ssm_b_re, ssm_b_im, ssm_c_re, ssm_c_im, ssm_d, ssm_w_out, gmlp_w_in, gmlp_v_gain, gmlp_w_spatial, gmlp_b_spatial, gmlp_w_out, ffn_w_gate_up, ffn_w_down):
    bsz, seq, d = x_prompt.shape
    dbsz, dseq, _ = x_sample.shape

    tables = _ssm_tables(ssm_lambda_re[0], ssm_lambda_im[0], ssm_log_dt[0], ssm_b_re[0], ssm_b_im[0],
                         ssm_c_re[0], ssm_c_im[0])

    def ffn(x_rows, layer, w_gu, w_dn, ffn_tm, casts=()):
        return _ffn(x_rows, norm_ffn[layer], w_gu, w_dn, norm_final, tm=ffn_tm, th=FFN_HIDDEN_TILE,
                    final_norm=layer == 1, casts=casts)

    xp = x_prompt.reshape(bsz * seq, d)
    u, w_out0 = _norm_matmul(xp, norm_mix[0], ssm_w_in, 0, ROW_TILE, casts=[(ssm_w_out, 0)])
    z, p_re, p_im, w_gu0, g_w_in = _ssm_scan_split(u.reshape(bsz, seq, d), tables, ssm_d[0], PROMPT_SCAN_STEPS,
                                                   casts=[(ffn_w_gate_up, 0), (gmlp_w_in, 0)])
    xp1, w_dn0, g_w_out = _ssm_out(z.reshape(bsz * seq, d), xp, w_out0, ROW_TILE,
                                   casts=[(ffn_w_down, 0), (gmlp_w_out, 0)])
    xp2, w_gu1, w_dn1 = ffn(xp1, 0, w_gu0, w_dn0, FFN_ROW_TILE, casts=[(ffn_w_gate_up, 1), (ffn_w_down, 1)])
    xp3, = _gmlp_prompt(xp2, norm_mix[1], g_w_in, gmlp_v_gain[0], gmlp_w_spatial[0],
                        jnp.transpose(gmlp_b_spatial[0]), g_w_out, ROW_TILE)
    y_prompt = ffn(xp3, 1, w_gu1, w_dn1, FFN_ROW_TILE)[0].reshape(bsz, seq, d)

    n_bt = dbsz // SAMPLE_TILE_SEQS
    tile_rows = SAMPLE_TILE_SEQS * dseq

    def to_rows(a):
        return jnp.transpose(a.reshape(n_bt, SAMPLE_TILE_SEQS, dseq, d), (0, 2, 1, 3)).reshape(dbsz * dseq, d)

    def from_rows(a):
        return jnp.transpose(a.reshape(n_bt, dseq, SAMPLE_TILE_SEQS, d), (0, 2, 1, 3)).reshape(dbsz, dseq, d)

    xs = to_rows(x_sample)
    h0_re = state_ssm_re[0].reshape(dbsz, SSM_LANES)
    h0_im = state_ssm_im[0].reshape(dbsz, SSM_LANES)
    us, = _norm_matmul(xs, norm_mix[0], ssm_w_in, 0, dbsz * dseq)
    zs, s_re, s_im = _ssm_scan(us, tables, ssm_d[0], h0_re, h0_im, nb=SAMPLE_TILE_SEQS, tt=dseq, n_bt=n_bt, n_tt=1)
    xs1, = _ssm_out(zs, xs, w_out0, tile_rows)
    xs2, = ffn(xs1, 0, w_gu0, w_dn0, dbsz * dseq)
    w_q = jnp.repeat(gmlp_w_spatial[0][:, :dseq, :dseq].reshape(GMLP_HEADS, dseq * dseq).T, GMLP_HEAD_DIM, axis=1)
    b_q = jnp.repeat(gmlp_b_spatial[0][:, :dseq].T, GMLP_HEAD_DIM, axis=1)
    xs3, v_rows = _gmlp_sample(xs2, norm_mix[1], g_w_in, gmlp_v_gain[0], w_q, b_q, g_w_out, tile_rows, dseq)
    y_sample = from_rows(ffn(xs3, 1, w_gu1, w_dn1, dbsz * dseq)[0])

    state_shape = (1, -1, SSM_GROUPS, SSM_STATE)
    return (y_prompt, y_sample,
            p_re.reshape(state_shape), p_im.reshape(state_shape),
            s_re.reshape(state_shape), s_im.reshape(state_shape),
            from_rows(v_rows)[None])
```

```python
import functools

import jax
import jax.numpy as jnp
from jax import lax
from jax.experimental import pallas as pl
from jax.experimental.pallas import tpu as pltpu

D_MODEL = 2048
SSM_GROUPS = 128
SSM_GROUP = 16
SSM_STATE = 64
SSM_LANES = SSM_GROUPS * SSM_STATE
GMLP_WIDTH = D_MODEL
GMLP_HEADS = 16
GMLP_HEAD_DIM = GMLP_WIDTH // GMLP_HEADS
CHUNK = 128
FFN_HIDDEN = 5632
EPS = 1e-6

LANES = 128
SUBLANES = 8
MXU_DIM = 256
VMEM_LIMIT = 56 * 1024 * 1024
BIG_VMEM_LIMIT = 60 * 1024 * 1024

PAIRS = SSM_GROUPS // 2
QUAD_K = 4 * 2 * SSM_GROUP
QUADS = PAIRS // 4
HALF_SLABS = PAIRS // 2
OCTS = D_MODEL // MXU_DIM
OCT_K = SSM_LANES // OCTS

BF16 = jnp.bfloat16
F32 = jnp.float32


def _resident(shape):
    zeros = (0,) * len(shape)
    return pl.BlockSpec(shape, lambda *_: zeros, pipeline_mode=pl.Buffered(1))


def _params(semantics):
    return pltpu.CompilerParams(dimension_semantics=semantics, vmem_limit_bytes=VMEM_LIMIT)


def _rms(x, g):
    ms = jnp.mean(x * x, axis=-1, keepdims=True)
    return (x * lax.rsqrt(ms + EPS)) * g


def _dot(a, b):
    return jnp.dot(a, b, preferred_element_type=F32)


BF16_ROWS = 16


class _SideCast:
    def __init__(self, w, layer, n_blocks, step_of=lambda i: i):
        _, r, c = w.shape
        assert r % (n_blocks * BF16_ROWS) == 0

        def block(*idx):
            return jnp.minimum(step_of(*idx), n_blocks - 1)

        self.operand = w
        self.in_spec = pl.BlockSpec((None, r // n_blocks, c), lambda *idx: (layer, block(*idx), 0))
        self.out_spec = pl.BlockSpec((r // n_blocks, c), lambda *idx: (block(*idx), 0))
        self.out_shape = jax.ShapeDtypeStruct((r, c), BF16)


def _cast_blocks(src_refs, dst_refs):
    for src, dst in zip(src_refs, dst_refs):
        dst[...] = src[...].astype(BF16)


def _slice_count(rows, steps):
    return max(n for n in range(1, steps + 1) if rows % (n * BF16_ROWS) == 0)


def _zoh(lr, li, log_dt):
    dt = jnp.exp(log_dt)
    mag = jnp.exp(lr * dt)
    a_re = mag * jnp.cos(li * dt)
    a_im = mag * jnp.sin(li * dt)
    den = lr * lr + li * li
    nr = a_re - 1.0
    ni = a_im
    q_re = (nr * lr + ni * li) / den
    q_im = (ni * lr - nr * li) / den
    return a_re, a_im, q_re, q_im


def _ssm_tables_kernel(lr_ref, li_ref, ldt_ref, br_ref, bi_ref, cr_ref, ci_ref,
                       are_ref, aim_ref, wb_ref, wcr_ref, wci_ref):
    c, p = SSM_GROUP, SSM_STATE
    a_re, a_im, q_re, q_im = _zoh(lr_ref[...], li_ref[...], ldt_ref[...])
    are_ref[...] = a_re
    aim_ref[...] = a_im

    def per_channel(a):
        return jnp.broadcast_to(a[:, None, :], (SSM_GROUPS, c, p)).reshape(SSM_GROUPS * c, p)

    q_re, q_im = per_channel(q_re), per_channel(q_im)
    br = br_ref[...]
    bi = bi_ref[...]
    bb_re = (q_re * br - q_im * bi).astype(BF16)
    bb_im = (q_re * bi + q_im * br).astype(BF16)

    src = lax.broadcasted_iota(jnp.int32, (2 * p, 2 * MXU_DIM), 0)
    dst = lax.broadcasted_iota(jnp.int32, (2 * p, 2 * MXU_DIM), 1)
    spread = ((src % p == dst % p) & (src // p == (dst % MXU_DIM) // LANES)).astype(BF16)
    row = lax.broadcasted_iota(jnp.int32, (2 * QUAD_K, 2 * MXU_DIM), 0)
    col = lax.broadcasted_iota(jnp.int32, (2 * QUAD_K, 2 * MXU_DIM), 1)
    hit = (row % QUAD_K) // c == 4 * (row // QUAD_K) + 2 * (col // MXU_DIM) + (col % LANES) // p
    for q in range(QUADS):
        rows = slice(QUAD_K * q, QUAD_K * (q + 1))
        both = jnp.concatenate([bb_re[rows, :], bb_im[rows, :]], axis=1)
        tile = _dot(both, spread)
        wb_ref[q] = jnp.where(hit, jnp.concatenate([tile, tile], axis=0), 0.0).astype(BF16)

    groups = MXU_DIM // c
    eye = (lax.broadcasted_iota(jnp.int32, (MXU_DIM, MXU_DIM), 0)
           == lax.broadcasted_iota(jnp.int32, (MXU_DIM, MXU_DIM), 1)).astype(BF16)
    same_group = (lax.broadcasted_iota(jnp.int32, (OCT_K, MXU_DIM), 0) // p
                  == lax.broadcasted_iota(jnp.int32, (OCT_K, MXU_DIM), 1) // c)
    transpose_lhs = (((0,), (0,)), ((), ()))
    for o in range(OCTS):
        rows = slice(MXU_DIM * o, MXU_DIM * (o + 1))
        for src_ref, dst_ref, sign in ((cr_ref, wcr_ref, 1.0), (ci_ref, wci_ref, -1.0)):
            blk = (sign * src_ref[rows, :]).astype(BF16)
            by_state = lax.dot_general(blk, eye, transpose_lhs, preferred_element_type=F32)
            dst_ref[o] = jnp.where(same_group, jnp.concatenate([by_state] * groups, axis=0), 0.0).astype(BF16)


def _ssm_tables(lambda_re, lambda_im, log_dt, b_re, b_im, c_re, c_im):
    g, p, c = b_re.shape
    rows = (g * c, p)
    a_shape = jax.ShapeDtypeStruct((g, p), F32)
    wc_shape = jax.ShapeDtypeStruct((OCTS, OCT_K, MXU_DIM), BF16)
    a_re, a_im, w_b, w_cr, w_ci = pl.pallas_call(
        _ssm_tables_kernel,
        out_shape=(a_shape, a_shape, jax.ShapeDtypeStruct((QUADS, 2 * QUAD_K, 2 * MXU_DIM), BF16), wc_shape, wc_shape),
        compiler_params=pltpu.CompilerParams(vmem_limit_bytes=VMEM_LIMIT),
        name="ssm_tables",
    )(lambda_re, lambda_im, jnp.broadcast_to(log_dt[:, None], (g, p)),
      jnp.transpose(b_re, (0, 2, 1)).reshape(rows), jnp.transpose(b_im, (0, 2, 1)).reshape(rows),
      c_re.reshape(rows), c_im.reshape(rows))
    a_re_t = jnp.broadcast_to(a_re.reshape(1, SSM_LANES), (SUBLANES, SSM_LANES))
    a_im_t = jnp.broadcast_to(a_im.reshape(1, SSM_LANES), (SUBLANES, SSM_LANES))
    return (w_b, a_re_t, a_im_t, w_cr, w_ci)


def _norm_matmul_kernel(x_ref, g_ref, w_ref, *rest):
    n_cast = (len(rest) - 2) // 2
    o_ref, wb_ref = rest[n_cast], rest[-1]
    _cast_blocks(rest[:n_cast], rest[n_cast + 1:-1])

    @pl.when(pl.program_id(0) == 0)
    def _():
        wb_ref[...] = w_ref[...].astype(BF16)

    half = x_ref.shape[0] // 2
    for rows in (slice(0, half), slice(half, 2 * half)):
        hn = _rms(x_ref[rows, :], g_ref[...]).astype(BF16)
        o_ref[rows, :] = _dot(hn, wb_ref[...])


def _norm_matmul(x, g, w, layer, tm, casts=()):
    m, d = x.shape
    n = w.shape[2]
    casts = [_SideCast(cw, cl, m // tm) for cw, cl in casts]
    return pl.pallas_call(
        _norm_matmul_kernel,
        grid=(m // tm,),
        in_specs=[pl.BlockSpec((tm, d), lambda i: (i, 0)), _resident((1, d)),
                  pl.BlockSpec((None, d, n), lambda i: (layer, 0, 0), pipeline_mode=pl.Buffered(1))]
                 + [c.in_spec for c in casts],
        out_specs=[pl.BlockSpec((tm, n), lambda i: (i, 0))] + [c.out_spec for c in casts],
        out_shape=[jax.ShapeDtypeStruct((m, n), F32)] + [c.out_shape for c in casts],
        scratch_shapes=[pltpu.VMEM((d, n), BF16)],
        compiler_params=_params(("arbitrary",)),
        name="ssm_norm_in_proj",
    )(x, g.reshape(1, d), w, *[c.operand for c in casts])


SAMPLE_SCAN_SLABS = 8


def _ssm_scan_kernel(u_ref, wb_ref, are_ref, aim_ref, wcr_ref, wci_ref, dsk_ref, h0r_ref, h0i_ref,
                     z_ref, sr_ref, si_ref, bur_ref, bui_ref, *, nb, tt):
    u = u_ref[...]
    ub = u.astype(BF16)
    for q in range(QUADS):
        lhs = ub[:, QUAD_K * q:QUAD_K * (q + 1)]
        for h in range(2):
            res = _dot(lhs, wb_ref[q, QUAD_K * h:QUAD_K * (h + 1), :])
            for s in range(2):
                j = 4 * q + 2 * h + s
                bur_ref[j] = res[:, MXU_DIM * s:MXU_DIM * s + LANES]
                bui_ref[j] = res[:, MXU_DIM * s + LANES:MXU_DIM * (s + 1)]

    for j0 in range(0, PAIRS, SAMPLE_SCAN_SLABS):
        slabs = range(j0, j0 + SAMPLE_SCAN_SLABS)
        ar = [are_ref[:, LANES * j:LANES * (j + 1)] for j in slabs]
        ai = [aim_ref[:, LANES * j:LANES * (j + 1)] for j in slabs]
        for sg in range(nb // SUBLANES):
            seqs = slice(sg * SUBLANES, (sg + 1) * SUBLANES)
            sr = [h0r_ref[seqs, LANES * j:LANES * (j + 1)] for j in slabs]
            si = [h0i_ref[seqs, LANES * j:LANES * (j + 1)] for j in slabs]
            for t in range(tt):
                rows = pl.ds(sg * SUBLANES * tt + t, SUBLANES, stride=tt)
                for n, j in enumerate(slabs):
                    nr = ar[n] * sr[n] - ai[n] * si[n] + bur_ref[j, rows, :]
                    ni = ar[n] * si[n] + ai[n] * sr[n] + bui_ref[j, rows, :]
                    sr[n], si[n] = nr, ni
                    bur_ref[j, rows, :] = nr
                    bui_ref[j, rows, :] = ni
            for n, j in enumerate(slabs):
                sr_ref[seqs, LANES * j:LANES * (j + 1)] = sr[n]
                si_ref[seqs, LANES * j:LANES * (j + 1)] = si[n]

    pairs_per_tile = PAIRS // OCTS
    for o in range(OCTS):
        slabs = range(pairs_per_tile * o, pairs_per_tile * (o + 1))
        nn = slice(o * MXU_DIM, (o + 1) * MXU_DIM)
        lre = jnp.concatenate([bur_ref[j] for j in slabs], axis=1).astype(BF16)
        lim = jnp.concatenate([bui_ref[j] for j in slabs], axis=1).astype(BF16)
        y = _dot(lre, wcr_ref[o]) + _dot(lim, wci_ref[o])
        y = y + dsk_ref[:, nn] * u[:, nn]
        z_ref[:, nn] = jax.nn.gelu(y).astype(BF16)


def _ssm_scan(u, tables, d_skip, h0_re, h0_im, *, nb, tt):
    w_b, a_re_t, a_im_t, w_cr, w_ci = tables
    m, d = u.shape
    r = nb * tt
    assert m % r == 0 and nb % SUBLANES == 0
    rows = pl.BlockSpec((r, d), lambda b: (b, 0))
    state = pl.BlockSpec((nb, SSM_LANES), lambda b: (b, 0))
    st_shape = jax.ShapeDtypeStruct((m // tt, SSM_LANES), F32)
    return pl.pallas_call(
        functools.partial(_ssm_scan_kernel, nb=nb, tt=tt),
        grid=(m // r,),
        in_specs=[rows, _resident(w_b.shape), _resident(a_re_t.shape), _resident(a_im_t.shape),
                  _resident(w_cr.shape), _resident(w_ci.shape), _resident((1, d)), state, state],
        out_specs=(rows, state, state),
        out_shape=(jax.ShapeDtypeStruct((m, d), BF16), st_shape, st_shape),
        scratch_shapes=[pltpu.VMEM((PAIRS, r, LANES), F32), pltpu.VMEM((PAIRS, r, LANES), F32)],
        compiler_params=_params(("parallel",)),
        name="ssm_scan",
    )(u, w_b, a_re_t, a_im_t, w_cr, w_ci, d_skip.reshape(1, d), h0_re, h0_im)


SCAN_SLABS = 8


def _ssm_scan_split_kernel(u_ref, wb_ref, are_ref, aim_ref, wcr_ref, wci_ref, dsk_ref, *rest, nb, tt):
    n_cast = (len(rest) - 7) // 2
    z_ref, sr_ref, si_ref = rest[n_cast:n_cast + 3]
    lhs_ref, bur_ref, bui_ref, y_ref = rest[-4:]
    _cast_blocks(rest[:n_cast], rest[n_cast + 3:-4])
    r = nb * tt
    i = pl.program_id(0)

    @pl.when(i == 0)
    def _():
        lhs_ref[...] = jnp.zeros_like(lhs_ref)
        sr_ref[...] = jnp.zeros_like(sr_ref)
        si_ref[...] = jnp.zeros_like(si_ref)

    def project_in(q):
        for b in range(nb):
            blk = u_ref[b, :, LANES * q:LANES * (q + 1)]
            for h in range(2):
                lhs_ref[q, h, pl.ds(2 * b + h, tt, stride=2 * nb), :] = blk
        lhs = jnp.concatenate([lhs_ref[q, 0], lhs_ref[q, 1]], axis=1).astype(BF16)
        res = _dot(lhs, wb_ref[q])
        for s in range(2):
            bur_ref[2 * q + s] = res[:, MXU_DIM * s:MXU_DIM * s + LANES]
            bui_ref[2 * q + s] = res[:, MXU_DIM * s + LANES:MXU_DIM * (s + 1)]

    def recur(slabs):
        ar = [are_ref[:, LANES * k:LANES * (k + 1)] for k in slabs]
        ai = [aim_ref[:, LANES * k:LANES * (k + 1)] for k in slabs]
        sr = [sr_ref[:, LANES * k:LANES * (k + 1)] for k in slabs]
        si = [si_ref[:, LANES * k:LANES * (k + 1)] for k in slabs]
        for t in range(tt):
            rows = slice(SUBLANES * t, SUBLANES * (t + 1))
            for n, k in enumerate(slabs):
                nr = ar[n] * sr[n] - ai[n] * si[n] + bur_ref[k, rows, :]
                ni = ar[n] * si[n] + ai[n] * sr[n] + bui_ref[k, rows, :]
                sr[n], si[n] = nr, ni
                bur_ref[k, rows, :] = nr
                bui_ref[k, rows, :] = ni
        for n, k in enumerate(slabs):
            sr_ref[:, LANES * k:LANES * (k + 1)] = sr[n]
            si_ref[:, LANES * k:LANES * (k + 1)] = si[n]

    def project_out(o):
        order = [(2 * (2 * o + ql) + s, h) for ql in range(2) for h in range(2) for s in range(2)]
        lre = jnp.concatenate([bur_ref[k, pl.ds(h, r, stride=2), :] for k, h in order], axis=1).astype(BF16)
        lim = jnp.concatenate([bui_ref[k, pl.ds(h, r, stride=2), :] for k, h in order], axis=1).astype(BF16)
        y = _dot(lre, wcr_ref[o]) + _dot(lim, wci_ref[o])
        for half in range(2):
            y_ref[half] = y[:, LANES * half:LANES * (half + 1)]
        for b in range(nb):
            for half in range(2):
                cols = slice(MXU_DIM * o + LANES * half, MXU_DIM * o + LANES * (half + 1))
                yb = y_ref[half, pl.ds(b, tt, stride=nb), :] + dsk_ref[:, cols] * u_ref[b, :, cols]
                z_ref[b, :, cols] = jax.nn.gelu(yb).astype(BF16)

    for q in range(QUADS):
        project_in(q)
    for k0 in range(0, HALF_SLABS, SCAN_SLABS):
        recur(range(k0, k0 + SCAN_SLABS))
    for o in range(OCTS):
        project_out(o)


def _ssm_scan_split(u, tables, d_skip, tt, casts):
    w_b, a_re_t, a_im_t, w_cr, w_ci = tables
    nb, seq, d = u.shape
    assert 2 * nb == SUBLANES and seq % tt == 0
    casts = [_SideCast(cw, layer, seq // tt) for cw, layer in casts]

    def split_lanes(a):
        halves = jnp.transpose(a[0].reshape(QUADS, 2, 2 * LANES), (1, 0, 2)).reshape(2, HALF_SLABS * LANES)
        return jnp.tile(halves, (nb, 1))

    rows = pl.BlockSpec((nb, tt, d), lambda i: (0, i, 0))
    st_shape = jax.ShapeDtypeStruct((SUBLANES, HALF_SLABS * LANES), F32)
    r = nb * tt
    state = pl.BlockSpec(st_shape.shape, lambda i: (0, 0))
    z, s_re, s_im, *w_cast = pl.pallas_call(
        functools.partial(_ssm_scan_split_kernel, nb=nb, tt=tt),
        grid=(seq // tt,),
        in_specs=[rows, _resident(w_b.shape), _resident(st_shape.shape), _resident(st_shape.shape),
                  _resident(w_cr.shape), _resident(w_ci.shape), _resident((1, d))] + [c.in_spec for c in casts],
        out_specs=[rows, state, state] + [c.out_spec for c in casts],
        out_shape=[jax.ShapeDtypeStruct((nb, seq, d), BF16), st_shape, st_shape] + [c.out_shape for c in casts],
        scratch_shapes=[pltpu.VMEM((QUADS, 2, 2 * r, LANES), F32),
                        pltpu.VMEM((HALF_SLABS, 2 * r, LANES), F32),
                        pltpu.VMEM((HALF_SLABS, 2 * r, LANES), F32),
                        pltpu.VMEM((2, r, LANES), F32)],
        compiler_params=pltpu.CompilerParams(dimension_semantics=("arbitrary",), vmem_limit_bytes=BIG_VMEM_LIMIT),
        name="ssm_scan_split",
    )(u, w_b, split_lanes(a_re_t), split_lanes(a_im_t), w_cr, w_ci, d_skip.reshape(1, d),
      *[c.operand for c in casts])

    def join(s):
        return jnp.transpose(s.reshape(nb, 2, QUADS, 2 * LANES), (0, 2, 1, 3)).reshape(nb, SSM_LANES)

    return (z, join(s_re), join(s_im), *w_cast)


SSM_OUT_COLS = 512


def _ssm_out_kernel(z_ref, x_ref, w_ref, *rest):
    o_ref = rest[len(rest) // 2]
    _cast_blocks(rest[:len(rest) // 2], rest[len(rest) // 2 + 1:])
    z = z_ref[...]
    for c in range(D_MODEL // SSM_OUT_COLS):
        cols = slice(c * SSM_OUT_COLS, (c + 1) * SSM_OUT_COLS)
        gate_cols = slice(D_MODEL + c * SSM_OUT_COLS, D_MODEL + (c + 1) * SSM_OUT_COLS)
        val = _dot(z, w_ref[:, cols])
        gate = _dot(z, w_ref[:, gate_cols])
        o_ref[:, cols] = x_ref[:, cols] + val * jax.nn.sigmoid(gate)


def _ssm_out(z, x, w, tm, casts=()):
    m, d = x.shape
    rows = pl.BlockSpec((tm, d), lambda i: (i, 0))
    casts = [_SideCast(cw, layer, m // tm) for cw, layer in casts]
    return pl.pallas_call(
        _ssm_out_kernel,
        grid=(m // tm,),
        in_specs=[rows, rows, _resident(w.shape)] + [c.in_spec for c in casts],
        out_specs=[rows] + [c.out_spec for c in casts],
        out_shape=[jax.ShapeDtypeStruct((m, d), F32)] + [c.out_shape for c in casts],
        compiler_params=_params(("parallel",)),
        name="ssm_out_glu",
    )(z, x, w, *[c.operand for c in casts])


def _ffn_kernel(x_ref, g_ref, wg_ref, wu_ref, wd_ref, gf_ref, *rest, final_norm):
    n_cast = (len(rest) - 2) // 2
    o_ref, hn_ref = rest[n_cast], rest[-1]
    _cast_blocks(rest[:n_cast], rest[n_cast + 1:-1])
    h = pl.program_id(1)

    @pl.when(h == 0)
    def _():
        x = x_ref[...]
        hn_ref[...] = _rms(x, g_ref[...]).astype(BF16)
        o_ref[...] = x

    hn = hn_ref[...]
    act = (jax.nn.silu(_dot(hn, wg_ref[...])) * _dot(hn, wu_ref[...])).astype(BF16)
    o_ref[...] += _dot(act, wd_ref[...])

    if final_norm:
        @pl.when(h == pl.num_programs(1) - 1)
        def _():
            o_ref[...] = _rms(o_ref[...], gf_ref[...])


def _ffn(x, g, w_gate_up, w_down, g_final, *, tm, th, final_norm, casts=()):
    m, d = x.shape
    n_h = FFN_HIDDEN // th
    steps = (m // tm) * n_h
    rows = pl.BlockSpec((tm, d), lambda i, h: (i, 0))
    casts = [_SideCast(cw, layer, _slice_count(cw.shape[1], steps), lambda i, h: i * n_h + h) for cw, layer in casts]
    return pl.pallas_call(
        functools.partial(_ffn_kernel, final_norm=final_norm),
        grid=(m // tm, n_h),
        in_specs=[rows,
                  _resident((1, d)),
                  pl.BlockSpec((d, th), lambda i, h: (0, h)),
                  pl.BlockSpec((d, th), lambda i, h: (0, n_h + h)),
                  pl.BlockSpec((th, d), lambda i, h: (h, 0)),
                  _resident((1, d))] + [c.in_spec for c in casts],
        out_specs=[rows] + [c.out_spec for c in casts],
        out_shape=[jax.ShapeDtypeStruct((m, d), F32)] + [c.out_shape for c in casts],
        scratch_shapes=[pltpu.VMEM((tm, d), BF16)],
        compiler_params=pltpu.CompilerParams(dimension_semantics=("arbitrary", "arbitrary"),
                                             vmem_limit_bytes=BIG_VMEM_LIMIT),
        name="ffn_swiglu",
    )(x, g.reshape(1, d), w_gate_up, w_gate_up, w_down, g_final.reshape(1, d), *[c.operand for c in casts])


def _gmlp_gate_inputs(x, g, win_ref, v_gain):
    hn = _rms(x, g).astype(BF16)
    v = jax.nn.gelu(_dot(hn, win_ref[:, GMLP_WIDTH:]))
    vc = v - jnp.mean(v, axis=-1, keepdims=True)
    vn = (vc * lax.rsqrt(jnp.mean(vc * vc, axis=-1, keepdims=True) + EPS)) * v_gain
    u = jax.nn.gelu(_dot(hn, win_ref[:, :GMLP_WIDTH]))
    return u, vn


def _gmlp_prompt_kernel(x_ref, g_ref, win_ref, vg_ref, ws_ref, bs_ref, wout_ref, *rest):
    n_cast = (len(rest) - 2) // 2
    o_ref, gate_ref = rest[n_cast], rest[-1]
    _cast_blocks(rest[:n_cast], rest[n_cast + 1:-1])
    x = x_ref[...]
    u, vn = _gmlp_gate_inputs(x, g_ref[...], win_ref, vg_ref[...])
    vb = vn.astype(BF16)
    q_idx = lax.broadcasted_iota(jnp.int32, (CHUNK, CHUNK), 0)
    k_idx = lax.broadcasted_iota(jnp.int32, (CHUNK, CHUNK), 1)
    causal = k_idx <= q_idx
    for h in range(GMLP_HEADS):
        cols = slice(h * GMLP_HEAD_DIM, (h + 1) * GMLP_HEAD_DIM)
        ws = jnp.where(causal, ws_ref[h], 0.0).astype(BF16)
        bias = bs_ref[:, h:h + 1]
        for c in range(0, x.shape[0] // CHUNK, 2):
            lo = slice(c * CHUNK, (c + 1) * CHUNK)
            hi = slice((c + 1) * CHUNK, (c + 2) * CHUNK)
            s = _dot(ws, jnp.concatenate([vb[lo, cols], vb[hi, cols]], axis=1)) + bias
            gate_ref[lo, cols] = (u[lo, cols] * s[:, :GMLP_HEAD_DIM]).astype(BF16)
            gate_ref[hi, cols] = (u[hi, cols] * s[:, GMLP_HEAD_DIM:]).astype(BF16)
    o_ref[...] = x + _dot(gate_ref[...], wout_ref[...])


def _gmlp_sample_kernel(x_ref, g_ref, win_ref, vg_ref, wq_ref, bq_ref, wout_ref, o_ref, v_ref,
                        us_ref, vs_ref, gs_ref, *, steps):
    x = x_ref[...]
    u, vn = _gmlp_gate_inputs(x, g_ref[...], win_ref, vg_ref[...])
    v_ref[...] = vn
    nseq = x.shape[0] // steps
    n_slabs = GMLP_WIDTH // LANES
    for j in range(n_slabs):
        lanes = slice(LANES * j, LANES * (j + 1))
        us_ref[j] = u[:, lanes]
        vs_ref[j] = vn[:, lanes]
    for j in range(n_slabs):
        lanes = slice(LANES * j, LANES * (j + 1))
        v_at = [vs_ref[j, pl.ds(k, nseq, stride=steps), :] for k in range(steps)]
        for q in range(steps):
            s = bq_ref[q:q + 1, lanes]
            for k in range(q + 1):
                s = s + wq_ref[q * steps + k:q * steps + k + 1, lanes] * v_at[k]
            gs_ref[j, pl.ds(q, nseq, stride=steps), :] = us_ref[j, pl.ds(q, nseq, stride=steps), :] * s
    gate = jnp.concatenate([gs_ref[j] for j in range(n_slabs)], axis=1).astype(BF16)
    o_ref[...] = x + _dot(gate, wout_ref[...])


def _gmlp_prompt(x, g, w_in, v_gain, w_spatial, b_spatial_t, w_out, tm, casts=()):
    m, d = x.shape
    rows = pl.BlockSpec((tm, d), lambda i: (i, 0))
    casts = [_SideCast(cw, layer, m // tm) for cw, layer in casts]
    return pl.pallas_call(
        _gmlp_prompt_kernel,
        grid=(m // tm,),
        in_specs=[rows, _resident((1, d)), _resident(w_in.shape), _resident((1, GMLP_WIDTH)),
                  _resident(w_spatial.shape), _resident(b_spatial_t.shape), _resident(w_out.shape)]
                 + [c.in_spec for c in casts],
        out_specs=[rows] + [c.out_spec for c in casts],
        out_shape=[jax.ShapeDtypeStruct((m, d), F32)] + [c.out_shape for c in casts],
        scratch_shapes=[pltpu.VMEM((tm, GMLP_WIDTH), BF16)],
        compiler_params=_params(("parallel",)),
        name="gmlp_prompt",
    )(x, g.reshape(1, d), w_in, v_gain.reshape(1, GMLP_WIDTH), w_spatial, b_spatial_t, w_out,
      *[c.operand for c in casts])


def _gmlp_sample(x, g, w_in, v_gain, w_q, b_q, w_out, tm, steps):
    m, d = x.shape
    rows = pl.BlockSpec((tm, d), lambda i: (i, 0))
    return pl.pallas_call(
        functools.partial(_gmlp_sample_kernel, steps=steps),
        grid=(m // tm,),
        in_specs=[rows, _resident((1, d)), _resident(w_in.shape), _resident((1, GMLP_WIDTH)),
                  _resident(w_q.shape), _resident(b_q.shape), _resident(w_out.shape)],
        out_specs=(rows, pl.BlockSpec((tm, GMLP_WIDTH), lambda i: (i, 0))),
        out_shape=(jax.ShapeDtypeStruct((m, d), F32), jax.ShapeDtypeStruct((m, GMLP_WIDTH), F32)),
        scratch_shapes=[pltpu.VMEM((GMLP_WIDTH // LANES, tm, LANES), F32)] * 3,
        compiler_params=_params(("parallel",)),
        name="gmlp_sample",
    )(x, g.reshape(1, d), w_in, v_gain.reshape(1, GMLP_WIDTH), w_q, b_q, w_out)


ROW_TILE = 512
FFN_ROW_TILE = 1024
FFN_HIDDEN_TILE = 512
PROMPT_SCAN_STEPS = 64
SAMPLE_TILE_SEQS = 64


def kernel(x_prompt, x_sample, state_ssm_re, state_ssm_im, norm_mix, norm_ffn, norm_final, ssm_w_in, ssm_lambda_re, ssm_lambda_im, ssm_log_dt, ssm_b_re, ssm_b_im, ssm_c_re, ssm_c_im, ssm_d, ssm_w_out, gmlp_w_in, gmlp_v_gain, gmlp_w_spatial, gmlp_b_spatial, gmlp_w_out, ffn_w_gate_up, ffn_w_down):
    bsz, seq, d = x_prompt.shape
    dbsz, dseq, _ = x_sample.shape

    tables = _ssm_tables(ssm_lambda_re[0], ssm_lambda_im[0], ssm_log_dt[0], ssm_b_re[0], ssm_b_im[0],
                         ssm_c_re[0], ssm_c_im[0])

    def ffn(x_rows, layer, w_gu, w_dn, ffn_tm, casts=()):
        return _ffn(x_rows, norm_ffn[layer], w_gu, w_dn, norm_final, tm=ffn_tm, th=FFN_HIDDEN_TILE,
                    final_norm=layer == 1, casts=casts)

    xp = x_prompt.reshape(bsz * seq, d)
    u, w_out0 = _norm_matmul(xp, norm_mix[0], ssm_w_in, 0, ROW_TILE, casts=[(ssm_w_out, 0)])
    z, p_re, p_im, w_gu0, g_w_in = _ssm_scan_split(u.reshape(bsz, seq, d), tables, ssm_d[0], PROMPT_SCAN_STEPS,
                                                   casts=[(ffn_w_gate_up, 0), (gmlp_w_in, 0)])
    xp1, w_dn0, g_w_out = _ssm_out(z.reshape(bsz * seq, d), xp, w_out0, ROW_TILE,
                                   casts=[(ffn_w_down, 0), (gmlp_w_out, 0)])
    xp2, w_gu1, w_dn1 = ffn(xp1, 0, w_gu0, w_dn0, FFN_ROW_TILE, casts=[(ffn_w_gate_up, 1), (ffn_w_down, 1)])
    xp3, = _gmlp_prompt(xp2, norm_mix[1], g_w_in, gmlp_v_gain[0], gmlp_w_spatial[0],
                        jnp.transpose(gmlp_b_spatial[0]), g_w_out, ROW_TILE)
    y_prompt = ffn(xp3, 1, w_gu1, w_dn1, FFN_ROW_TILE)[0].reshape(bsz, seq, d)

    tile_rows = SAMPLE_TILE_SEQS * dseq
    xs = x_sample.reshape(dbsz * dseq, d)
    h0_re = state_ssm_re[0].reshape(dbsz, SSM_LANES)
    h0_im = state_ssm_im[0].reshape(dbsz, SSM_LANES)
    us, = _norm_matmul(xs, norm_mix[0], ssm_w_in, 0, dbsz * dseq)
    zs, s_re, s_im = _ssm_scan(us, tables, ssm_d[0], h0_re, h0_im, nb=SAMPLE_TILE_SEQS, tt=dseq)
    xs1, = _ssm_out(zs, xs, w_out0, tile_rows)
    xs2, = ffn(xs1, 0, w_gu0, w_dn0, dbsz * dseq)
    w_q = jnp.repeat(gmlp_w_spatial[0][:, :dseq, :dseq].reshape(GMLP_HEADS, dseq * dseq).T, GMLP_HEAD_DIM, axis=1)
    b_q = jnp.repeat(gmlp_b_spatial[0][:, :dseq].T, GMLP_HEAD_DIM, axis=1)
    xs3, v_rows = _gmlp_sample(xs2, norm_mix[1], g_w_in, gmlp_v_gain[0], w_q, b_q, g_w_out, tile_rows, dseq)
    y_sample = ffn(xs3, 1, w_gu1, w_dn1, dbsz * dseq)[0].reshape(dbsz, dseq, d)

    state_shape = (1, -1, SSM_GROUPS, SSM_STATE)
    return (y_prompt, y_sample,
            p_re.reshape(state_shape), p_im.reshape(state_shape),
            s_re.reshape(state_shape), s_im.reshape(state_shape),
            v_rows.reshape(1, dbsz, dseq, GMLP_WIDTH))
```

```python
import functools

import jax
import jax.numpy as jnp
from jax import lax
from jax.experimental import pallas as pl
from jax.experimental.pallas import tpu as pltpu

D_MODEL = 2048
SSM_GROUPS = 128
SSM_GROUP = 16
SSM_STATE = 64
SSM_LANES = SSM_GROUPS * SSM_STATE
GMLP_WIDTH = D_MODEL
GMLP_HEADS = 16
GMLP_HEAD_DIM = GMLP_WIDTH // GMLP_HEADS
CHUNK = 128
FFN_HIDDEN = 5632
EPS = 1e-6

LANES = 128
SUBLANES = 8
MXU_DIM = 256
VMEM_LIMIT = 56 * 1024 * 1024
BIG_VMEM_LIMIT = 60 * 1024 * 1024

PAIRS = SSM_GROUPS // 2
QUAD_K = 4 * 2 * SSM_GROUP
QUADS = PAIRS // 4
HALF_SLABS = PAIRS // 2
OCTS = D_MODEL // MXU_DIM
OCT_K = SSM_LANES // OCTS

BF16 = jnp.bfloat16
F32 = jnp.float32


def _resident(shape):
    zeros = (0,) * len(shape)
    return pl.BlockSpec(shape, lambda *_: zeros, pipeline_mode=pl.Buffered(1))


def _params(semantics, fusible=None):
    return pltpu.CompilerParams(dimension_semantics=semantics, vmem_limit_bytes=VMEM_LIMIT,
                                allow_input_fusion=fusible)


def _rms(x, g):
    ms = jnp.mean(x * x, axis=-1, keepdims=True)
    return (x * lax.rsqrt(ms + EPS)) * g


def _dot(a, b):
    return jnp.dot(a, b, preferred_element_type=F32)


BF16_ROWS = 16


class _SideCast:
    def __init__(self, w, layer, n_blocks, step_of=lambda i: i):
        _, r, c = w.shape
        assert r % (n_blocks * BF16_ROWS) == 0

        def block(*idx):
            return jnp.minimum(step_of(*idx), n_blocks - 1)

        self.operand = w
        self.in_spec = pl.BlockSpec((None, r // n_blocks, c), lambda *idx: (layer, block(*idx), 0))
        self.out_spec = pl.BlockSpec((r // n_blocks, c), lambda *idx: (block(*idx), 0))
        self.out_shape = jax.ShapeDtypeStruct((r, c), BF16)


def _cast_blocks(src_refs, dst_refs):
    for src, dst in zip(src_refs, dst_refs):
        dst[...] = src[...].astype(BF16)


def _slice_count(rows, steps):
    return max(n for n in range(1, steps + 1) if rows % (n * BF16_ROWS) == 0)


def _zoh(lr, li, log_dt):
    dt = jnp.exp(log_dt)
    mag = jnp.exp(lr * dt)
    a_re = mag * jnp.cos(li * dt)
    a_im = mag * jnp.sin(li * dt)
    den = lr * lr + li * li
    nr = a_re - 1.0
    ni = a_im
    q_re = (nr * lr + ni * li) / den
    q_im = (ni * lr - nr * li) / den
    return a_re, a_im, q_re, q_im


def _ssm_tables_kernel(lr_ref, li_ref, ldt_ref, br_ref, bi_ref, cr_ref, ci_ref,
                       are_ref, aim_ref, wb_ref, wcr_ref, wci_ref):
    c, p = SSM_GROUP, SSM_STATE
    a_re, a_im, q_re, q_im = _zoh(lr_ref[...], li_ref[...], ldt_ref[...])
    are_ref[...] = a_re
    aim_ref[...] = a_im

    def per_channel(a):
        return jnp.broadcast_to(a[:, None, :], (SSM_GROUPS, c, p)).reshape(SSM_GROUPS * c, p)

    q_re, q_im = per_channel(q_re), per_channel(q_im)
    br = br_ref[...]
    bi = bi_ref[...]
    bb_re = (q_re * br - q_im * bi).astype(BF16)
    bb_im = (q_re * bi + q_im * br).astype(BF16)

    src = lax.broadcasted_iota(jnp.int32, (2 * p, 2 * MXU_DIM), 0)
    dst = lax.broadcasted_iota(jnp.int32, (2 * p, 2 * MXU_DIM), 1)
    spread = ((src % p == dst % p) & (src // p == (dst % MXU_DIM) // LANES)).astype(BF16)
    row = lax.broadcasted_iota(jnp.int32, (2 * QUAD_K, 2 * MXU_DIM), 0)
    col = lax.broadcasted_iota(jnp.int32, (2 * QUAD_K, 2 * MXU_DIM), 1)
    hit = (row % QUAD_K) // c == 4 * (row // QUAD_K) + 2 * (col // MXU_DIM) + (col % LANES) // p
    for q in range(QUADS):
        rows = slice(QUAD_K * q, QUAD_K * (q + 1))
        both = jnp.concatenate([bb_re[rows, :], bb_im[rows, :]], axis=1)
        tile = _dot(both, spread)
        wb_ref[q] = jnp.where(hit, jnp.concatenate([tile, tile], axis=0), 0.0).astype(BF16)

    groups = MXU_DIM // c
    eye = (lax.broadcasted_iota(jnp.int32, (MXU_DIM, MXU_DIM), 0)
           == lax.broadcasted_iota(jnp.int32, (MXU_DIM, MXU_DIM), 1)).astype(BF16)
    same_group = (lax.broadcasted_iota(jnp.int32, (OCT_K, MXU_DIM), 0) // p
                  == lax.broadcasted_iota(jnp.int32, (OCT_K, MXU_DIM), 1) // c)
    transpose_lhs = (((0,), (0,)), ((), ()))
    for o in range(OCTS):
        rows = slice(MXU_DIM * o, MXU_DIM * (o + 1))
        for src_ref, dst_ref, sign in ((cr_ref, wcr_ref, 1.0), (ci_ref, wci_ref, -1.0)):
            blk = (sign * src_ref[rows, :]).astype(BF16)
            by_state = lax.dot_general(blk, eye, transpose_lhs, preferred_element_type=F32)
            dst_ref[o] = jnp.where(same_group, jnp.concatenate([by_state] * groups, axis=0), 0.0).astype(BF16)


def _ssm_tables(lambda_re, lambda_im, log_dt, b_re, b_im, c_re, c_im):
    g, p, c = b_re.shape
    rows = (g * c, p)
    a_shape = jax.ShapeDtypeStruct((g, p), F32)
    wc_shape = jax.ShapeDtypeStruct((OCTS, OCT_K, MXU_DIM), BF16)
    a_re, a_im, w_b, w_cr, w_ci = pl.pallas_call(
        _ssm_tables_kernel,
        out_shape=(a_shape, a_shape, jax.ShapeDtypeStruct((QUADS, 2 * QUAD_K, 2 * MXU_DIM), BF16), wc_shape, wc_shape),
        compiler_params=pltpu.CompilerParams(vmem_limit_bytes=VMEM_LIMIT, allow_input_fusion=[True] * 7),
        name="ssm_tables",
    )(lambda_re, lambda_im, jnp.broadcast_to(log_dt[:, None], (g, p)),
      jnp.transpose(b_re, (0, 2, 1)).reshape(rows), jnp.transpose(b_im, (0, 2, 1)).reshape(rows),
      c_re.reshape(rows), c_im.reshape(rows))
    a_re_t = jnp.broadcast_to(a_re.reshape(1, SSM_LANES), (SUBLANES, SSM_LANES))
    a_im_t = jnp.broadcast_to(a_im.reshape(1, SSM_LANES), (SUBLANES, SSM_LANES))
    return (w_b, a_re_t, a_im_t, w_cr, w_ci)


def _norm_matmul_kernel(x_ref, g_ref, w_ref, *rest):
    n_cast = (len(rest) - 2) // 2
    o_ref, wb_ref = rest[n_cast], rest[-1]
    _cast_blocks(rest[:n_cast], rest[n_cast + 1:-1])

    @pl.when(pl.program_id(0) == 0)
    def _():
        wb_ref[...] = w_ref[...].astype(BF16)

    half = x_ref.shape[0] // 2
    for rows in (slice(0, half), slice(half, 2 * half)):
        hn = _rms(x_ref[rows, :], g_ref[...]).astype(BF16)
        o_ref[rows, :] = _dot(hn, wb_ref[...])


def _norm_matmul(x, g, w, layer, tm, casts=()):
    m, d = x.shape
    n = w.shape[2]
    casts = [_SideCast(cw, cl, m // tm) for cw, cl in casts]
    return pl.pallas_call(
        _norm_matmul_kernel,
        grid=(m // tm,),
        in_specs=[pl.BlockSpec((tm, d), lambda i: (i, 0)), _resident((1, d)),
                  pl.BlockSpec((None, d, n), lambda i: (layer, 0, 0), pipeline_mode=pl.Buffered(1))]
                 + [c.in_spec for c in casts],
        out_specs=[pl.BlockSpec((tm, n), lambda i: (i, 0))] + [c.out_spec for c in casts],
        out_shape=[jax.ShapeDtypeStruct((m, n), F32)] + [c.out_shape for c in casts],
        scratch_shapes=[pltpu.VMEM((d, n), BF16)],
        compiler_params=_params(("arbitrary",), [True, True, False] + [False] * len(casts)),
        name="ssm_norm_in_proj",
    )(x, g.reshape(1, d), w, *[c.operand for c in casts])


SCAN_COLS = 1024


def _ssm_scan_kernel(u_ref, wb_ref, are_ref, aim_ref, wcr_ref, wci_ref, dsk_ref, h0r_ref, h0i_ref,
                     z_ref, sr_ref, si_ref, bur_ref, bui_ref, *, nb, tt):
    @pl.when(pl.program_id(1) == 0)
    def _():
        sr_ref[...] = h0r_ref[...]
        si_ref[...] = h0i_ref[...]

    u = u_ref[...]
    ub = u.astype(BF16)
    for q in range(QUADS):
        lhs = ub[:, QUAD_K * q:QUAD_K * (q + 1)]
        for h in range(2):
            res = _dot(lhs, wb_ref[q, QUAD_K * h:QUAD_K * (h + 1), :])
            for s in range(2):
                j = 4 * q + 2 * h + s
                bur_ref[:, LANES * j:LANES * (j + 1)] = res[:, MXU_DIM * s:MXU_DIM * s + LANES]
                bui_ref[:, LANES * j:LANES * (j + 1)] = res[:, MXU_DIM * s + LANES:MXU_DIM * (s + 1)]

    for cb in range(SSM_LANES // SCAN_COLS):
        cols = slice(cb * SCAN_COLS, (cb + 1) * SCAN_COLS)
        ar = are_ref[:, cols]
        ai = aim_ref[:, cols]
        for sg in range(nb // SUBLANES):
            seqs = slice(sg * SUBLANES, (sg + 1) * SUBLANES)
            sr = sr_ref[seqs, cols]
            si = si_ref[seqs, cols]
            for t in range(tt):
                rows = slice(t * nb + sg * SUBLANES, t * nb + (sg + 1) * SUBLANES)
                nr = ar * sr - ai * si + bur_ref[rows, cols]
                ni = ar * si + ai * sr + bui_ref[rows, cols]
                sr, si = nr, ni
                bur_ref[rows, cols] = sr
                bui_ref[rows, cols] = si
            sr_ref[seqs, cols] = sr
            si_ref[seqs, cols] = si

    for o in range(OCTS):
        kk = slice(o * OCT_K, (o + 1) * OCT_K)
        nn = slice(o * MXU_DIM, (o + 1) * MXU_DIM)
        y = _dot(bur_ref[:, kk].astype(BF16), wcr_ref[o]) + _dot(bui_ref[:, kk].astype(BF16), wci_ref[o])
        y = y + dsk_ref[:, nn] * u[:, nn]
        z_ref[:, nn] = jax.nn.gelu(y).astype(BF16)


def _ssm_scan(u, tables, d_skip, h0_re, h0_im, *, nb, tt, n_bt, n_tt):
    w_b, a_re_t, a_im_t, w_cr, w_ci = tables
    m, d = u.shape
    r = nb * tt
    assert m == r * n_bt * n_tt and nb % SUBLANES == 0
    rows = pl.BlockSpec((r, d), lambda b, t: (b * n_tt + t, 0))
    state = pl.BlockSpec((nb, SSM_LANES), lambda b, t: (b, 0))
    st_shape = jax.ShapeDtypeStruct((nb * n_bt, SSM_LANES), F32)
    return pl.pallas_call(
        functools.partial(_ssm_scan_kernel, nb=nb, tt=tt),
        grid=(n_bt, n_tt),
        in_specs=[rows, _resident(w_b.shape), _resident(a_re_t.shape), _resident(a_im_t.shape),
                  _resident(w_cr.shape), _resident(w_ci.shape), _resident((1, d)), state, state],
        out_specs=(rows, state, state),
        out_shape=(jax.ShapeDtypeStruct((m, d), BF16), st_shape, st_shape),
        scratch_shapes=[pltpu.VMEM((r, SSM_LANES), F32), pltpu.VMEM((r, SSM_LANES), F32)],
        compiler_params=_params(("parallel", "arbitrary"), [False, False, True, True, False, False, True, True, True]),
        name="ssm_scan",
    )(u, w_b, a_re_t, a_im_t, w_cr, w_ci, d_skip.reshape(1, d), h0_re, h0_im)


SCAN_SLABS = 8


def _ssm_scan_split_kernel(u_ref, wb_ref, are_ref, aim_ref, wcr_ref, wci_ref, dsk_ref, *rest, nb, tt):
    n_cast = (len(rest) - 7) // 2
    z_ref, sr_ref, si_ref = rest[n_cast:n_cast + 3]
    lhs_ref, bur_ref, bui_ref, y_ref = rest[-4:]
    _cast_blocks(rest[:n_cast], rest[n_cast + 3:-4])
    r = nb * tt
    i = pl.program_id(0)

    @pl.when(i == 0)
    def _():
        lhs_ref[...] = jnp.zeros_like(lhs_ref)
        sr_ref[...] = jnp.zeros_like(sr_ref)
        si_ref[...] = jnp.zeros_like(si_ref)

    def project_in(q):
        for b in range(nb):
            blk = u_ref[b, :, LANES * q:LANES * (q + 1)]
            for h in range(2):
                lhs_ref[q, h, pl.ds(2 * b + h, tt, stride=2 * nb), :] = blk
        lhs = jnp.concatenate([lhs_ref[q, 0], lhs_ref[q, 1]], axis=1).astype(BF16)
        res = _dot(lhs, wb_ref[q])
        for s in range(2):
            bur_ref[2 * q + s] = res[:, MXU_DIM * s:MXU_DIM * s + LANES]
            bui_ref[2 * q + s] = res[:, MXU_DIM * s + LANES:MXU_DIM * (s + 1)]

    def recur(slabs):
        ar = [are_ref[:, LANES * k:LANES * (k + 1)] for k in slabs]
        ai = [aim_ref[:, LANES * k:LANES * (k + 1)] for k in slabs]
        sr = [sr_ref[:, LANES * k:LANES * (k + 1)] for k in slabs]
        si = [si_ref[:, LANES * k:LANES * (k + 1)] for k in slabs]
        for t in range(tt):
            rows = slice(SUBLANES * t, SUBLANES * (t + 1))
            for n, k in enumerate(slabs):
                nr = ar[n] * sr[n] - ai[n] * si[n] + bur_ref[k, rows, :]
                ni = ar[n] * si[n] + ai[n] * sr[n] + bui_ref[k, rows, :]
                sr[n], si[n] = nr, ni
                bur_ref[k, rows, :] = nr
                bui_ref[k, rows, :] = ni
        for n, k in enumerate(slabs):
            sr_ref[:, LANES * k:LANES * (k + 1)] = sr[n]
            si_ref[:, LANES * k:LANES * (k + 1)] = si[n]

    def project_out(o):
        order = [(2 * (2 * o + ql) + s, h) for ql in range(2) for h in range(2) for s in range(2)]
        lre = jnp.concatenate([bur_ref[k, pl.ds(h, r, stride=2), :] for k, h in order], axis=1).astype(BF16)
        lim = jnp.concatenate([bui_ref[k, pl.ds(h, r, stride=2), :] for k, h in order], axis=1).astype(BF16)
        y = _dot(lre, wcr_ref[o]) + _dot(lim, wci_ref[o])
        for half in range(2):
            y_ref[half] = y[:, LANES * half:LANES * (half + 1)]
        for b in range(nb):
            for half in range(2):
                cols = slice(MXU_DIM * o + LANES * half, MXU_DIM * o + LANES * (half + 1))
                yb = y_ref[half, pl.ds(b, tt, stride=nb), :] + dsk_ref[:, cols] * u_ref[b, :, cols]
                z_ref[b, :, cols] = jax.nn.gelu(yb).astype(BF16)

    for q in range(QUADS):
        project_in(q)
    for k0 in range(0, HALF_SLABS, SCAN_SLABS):
        recur(range(k0, k0 + SCAN_SLABS))
    for o in range(OCTS):
        project_out(o)


def _ssm_scan_split(u, tables, d_skip, tt, casts):
    w_b, a_re_t, a_im_t, w_cr, w_ci = tables
    nb, seq, d = u.shape
    assert 2 * nb == SUBLANES and seq % tt == 0
    casts = [_SideCast(cw, layer, seq // tt) for cw, layer in casts]

    def split_lanes(a):
        halves = jnp.transpose(a[0].reshape(QUADS, 2, 2 * LANES), (1, 0, 2)).reshape(2, HALF_SLABS * LANES)
        return jnp.tile(halves, (nb, 1))

    rows = pl.BlockSpec((nb, tt, d), lambda i: (0, i, 0))
    st_shape = jax.ShapeDtypeStruct((SUBLANES, HALF_SLABS * LANES), F32)
    r = nb * tt
    state = pl.BlockSpec(st_shape.shape, lambda i: (0, 0))
    z, s_re, s_im, *w_cast = pl.pallas_call(
        functools.partial(_ssm_scan_split_kernel, nb=nb, tt=tt),
        grid=(seq // tt,),
        in_specs=[rows, _resident(w_b.shape), _resident(st_shape.shape), _resident(st_shape.shape),
                  _resident(w_cr.shape), _resident(w_ci.shape), _resident((1, d))] + [c.in_spec for c in casts],
        out_specs=[rows, state, state] + [c.out_spec for c in casts],
        out_shape=[jax.ShapeDtypeStruct((nb, seq, d), BF16), st_shape, st_shape] + [c.out_shape for c in casts],
        scratch_shapes=[pltpu.VMEM((QUADS, 2, 2 * r, LANES), F32),
                        pltpu.VMEM((HALF_SLABS, 2 * r, LANES), F32),
                        pltpu.VMEM((HALF_SLABS, 2 * r, LANES), F32),
                        pltpu.VMEM((2, r, LANES), F32)],
        compiler_params=pltpu.CompilerParams(
            dimension_semantics=("arbitrary",), vmem_limit_bytes=BIG_VMEM_LIMIT,
            allow_input_fusion=[False, False, True, True, False, False, True] + [False] * len(casts)),
        name="ssm_scan_split",
    )(u, w_b, split_lanes(a_re_t), split_lanes(a_im_t), w_cr, w_ci, d_skip.reshape(1, d),
      *[c.operand for c in casts])

    def join(s):
        return jnp.transpose(s.reshape(nb, 2, QUADS, 2 * LANES), (0, 2, 1, 3)).reshape(nb, SSM_LANES)

    return (z, join(s_re), join(s_im), *w_cast)


SSM_OUT_COLS = 512


def _ssm_out_kernel(z_ref, x_ref, w_ref, *rest):
    o_ref = rest[len(rest) // 2]
    _cast_blocks(rest[:len(rest) // 2], rest[len(rest) // 2 + 1:])
    z = z_ref[...]
    for c in range(D_MODEL // SSM_OUT_COLS):
        cols = slice(c * SSM_OUT_COLS, (c + 1) * SSM_OUT_COLS)
        gate_cols = slice(D_MODEL + c * SSM_OUT_COLS, D_MODEL + (c + 1) * SSM_OUT_COLS)
        val = _dot(z, w_ref[:, cols])
        gate = _dot(z, w_ref[:, gate_cols])
        o_ref[:, cols] = x_ref[:, cols] + val * jax.nn.sigmoid(gate)


def _ssm_out(z, x, w, tm, casts=()):
    m, d = x.shape
    rows = pl.BlockSpec((tm, d), lambda i: (i, 0))
    casts = [_SideCast(cw, layer, m // tm) for cw, layer in casts]
    return pl.pallas_call(
        _ssm_out_kernel,
        grid=(m // tm,),
        in_specs=[rows, rows, _resident(w.shape)] + [c.in_spec for c in casts],
        out_specs=[rows] + [c.out_spec for c in casts],
        out_shape=[jax.ShapeDtypeStruct((m, d), F32)] + [c.out_shape for c in casts],
        compiler_params=_params(("parallel",), [False, True, False] + [False] * len(casts)),
        name="ssm_out_glu",
    )(z, x, w, *[c.operand for c in casts])


def _ffn_kernel(x_ref, g_ref, wg_ref, wu_ref, wd_ref, gf_ref, *rest, final_norm):
    n_cast = (len(rest) - 2) // 2
    o_ref, hn_ref = rest[n_cast], rest[-1]
    _cast_blocks(rest[:n_cast], rest[n_cast + 1:-1])
    h = pl.program_id(1)

    @pl.when(h == 0)
    def _():
        x = x_ref[...]
        hn_ref[...] = _rms(x, g_ref[...]).astype(BF16)
        o_ref[...] = x

    hn = hn_ref[...]
    act = (jax.nn.silu(_dot(hn, wg_ref[...])) * _dot(hn, wu_ref[...])).astype(BF16)
    o_ref[...] += _dot(act, wd_ref[...])

    if final_norm:
        @pl.when(h == pl.num_programs(1) - 1)
        def _():
            o_ref[...] = _rms(o_ref[...], gf_ref[...])


def _ffn(x, g, w_gate_up, w_down, g_final, *, tm, th, final_norm, casts=()):
    m, d = x.shape
    n_h = FFN_HIDDEN // th
    steps = (m // tm) * n_h
    rows = pl.BlockSpec((tm, d), lambda i, h: (i, 0))
    casts = [_SideCast(cw, layer, _slice_count(cw.shape[1], steps), lambda i, h: i * n_h + h) for cw, layer in casts]
    return pl.pallas_call(
        functools.partial(_ffn_kernel, final_norm=final_norm),
        grid=(m // tm, n_h),
        in_specs=[rows,
                  _resident((1, d)),
                  pl.BlockSpec((d, th), lambda i, h: (0, h)),
                  pl.BlockSpec((d, th), lambda i, h: (0, n_h + h)),
                  pl.BlockSpec((th, d), lambda i, h: (h, 0)),
                  _resident((1, d))] + [c.in_spec for c in casts],
        out_specs=[rows] + [c.out_spec for c in casts],
        out_shape=[jax.ShapeDtypeStruct((m, d), F32)] + [c.out_shape for c in casts],
        scratch_shapes=[pltpu.VMEM((tm, d), BF16)],
        compiler_params=pltpu.CompilerParams(dimension_semantics=("parallel", "arbitrary"),
                                             vmem_limit_bytes=BIG_VMEM_LIMIT,
                                             allow_input_fusion=[False, True, False, False, False, True]
                                             + [False] * len(casts)),
        name="ffn_swiglu",
    )(x, g.reshape(1, d), w_gate_up, w_gate_up, w_down, g_final.reshape(1, d), *[c.operand for c in casts])


def _gmlp_gate_inputs(x, g, win_ref, v_gain):
    hn = _rms(x, g).astype(BF16)
    v = jax.nn.gelu(_dot(hn, win_ref[:, GMLP_WIDTH:]))
    vc = v - jnp.mean(v, axis=-1, keepdims=True)
    vn = (vc * lax.rsqrt(jnp.mean(vc * vc, axis=-1, keepdims=True) + EPS)) * v_gain
    u = jax.nn.gelu(_dot(hn, win_ref[:, :GMLP_WIDTH]))
    return u, vn


def _gmlp_prompt_kernel(x_ref, g_ref, win_ref, vg_ref, ws_ref, bs_ref, wout_ref, *rest):
    n_cast = (len(rest) - 2) // 2
    o_ref, gate_ref = rest[n_cast], rest[-1]
    _cast_blocks(rest[:n_cast], rest[n_cast + 1:-1])
    x = x_ref[...]
    u, vn = _gmlp_gate_inputs(x, g_ref[...], win_ref, vg_ref[...])
    vb = vn.astype(BF16)
    q_idx = lax.broadcasted_iota(jnp.int32, (CHUNK, CHUNK), 0)
    k_idx = lax.broadcasted_iota(jnp.int32, (CHUNK, CHUNK), 1)
    causal = k_idx <= q_idx
    for h in range(GMLP_HEADS):
        cols = slice(h * GMLP_HEAD_DIM, (h + 1) * GMLP_HEAD_DIM)
        ws = jnp.where(causal, ws_ref[h], 0.0).astype(BF16)
        bias = bs_ref[:, h:h + 1]
        for c in range(0, x.shape[0] // CHUNK, 2):
            lo = slice(c * CHUNK, (c + 1) * CHUNK)
            hi = slice((c + 1) * CHUNK, (c + 2) * CHUNK)
            s = _dot(ws, jnp.concatenate([vb[lo, cols], vb[hi, cols]], axis=1)) + bias
            gate_ref[lo, cols] = (u[lo, cols] * s[:, :GMLP_HEAD_DIM]).astype(BF16)
            gate_ref[hi, cols] = (u[hi, cols] * s[:, GMLP_HEAD_DIM:]).astype(BF16)
    o_ref[...] = x + _dot(gate_ref[...], wout_ref[...])


def _gmlp_sample_kernel(x_ref, g_ref, win_ref, vg_ref, wq_ref, bq_ref, wout_ref, o_ref, v_ref, *, steps):
    x = x_ref[...]
    u, vn = _gmlp_gate_inputs(x, g_ref[...], win_ref, vg_ref[...])
    v_ref[...] = vn
    nseq = x.shape[0] // steps
    gates = []
    for q in range(steps):
        s = bq_ref[q:q + 1, :]
        for k in range(q + 1):
            s = s + wq_ref[q * steps + k:q * steps + k + 1, :] * vn[k * nseq:(k + 1) * nseq, :]
        gates.append(u[q * nseq:(q + 1) * nseq, :] * s)
    gate = jnp.concatenate(gates, axis=0).astype(BF16)
    o_ref[...] = x + _dot(gate, wout_ref[...])


def _gmlp_prompt(x, g, w_in, v_gain, w_spatial, b_spatial_t, w_out, tm, casts=()):
    m, d = x.shape
    rows = pl.BlockSpec((tm, d), lambda i: (i, 0))
    casts = [_SideCast(cw, layer, m // tm) for cw, layer in casts]
    return pl.pallas_call(
        _gmlp_prompt_kernel,
        grid=(m // tm,),
        in_specs=[rows, _resident((1, d)), _resident(w_in.shape), _resident((1, GMLP_WIDTH)),
                  _resident(w_spatial.shape), _resident(b_spatial_t.shape), _resident(w_out.shape)]
                 + [c.in_spec for c in casts],
        out_specs=[rows] + [c.out_spec for c in casts],
        out_shape=[jax.ShapeDtypeStruct((m, d), F32)] + [c.out_shape for c in casts],
        scratch_shapes=[pltpu.VMEM((tm, GMLP_WIDTH), BF16)],
        compiler_params=_params(("parallel",), [False, True, False, True, False, True, False] + [False] * len(casts)),
        name="gmlp_prompt",
    )(x, g.reshape(1, d), w_in, v_gain.reshape(1, GMLP_WIDTH), w_spatial, b_spatial_t, w_out,
      *[c.operand for c in casts])


def _gmlp_sample(x, g, w_in, v_gain, w_q, b_q, w_out, tm, steps):
    m, d = x.shape
    rows = pl.BlockSpec((tm, d), lambda i: (i, 0))
    return pl.pallas_call(
        functools.partial(_gmlp_sample_kernel, steps=steps),
        grid=(m // tm,),
        in_specs=[rows, _resident((1, d)), _resident(w_in.shape), _resident((1, GMLP_WIDTH)),
                  _resident(w_q.shape), _resident(b_q.shape), _resident(w_out.shape)],
        out_specs=(rows, pl.BlockSpec((tm, GMLP_WIDTH), lambda i: (i, 0))),
        out_shape=(jax.ShapeDtypeStruct((m, d), F32), jax.ShapeDtypeStruct((m, GMLP_WIDTH), F32)),
        compiler_params=_params(("parallel",), [False, True, False, True, True, True, False]),
        name="gmlp_sample",
    )(x, g.reshape(1, d), w_in, v_gain.reshape(1, GMLP_WIDTH), w_q, b_q, w_out)


ROW_TILE = 512
FFN_ROW_TILE = 1024
FFN_HIDDEN_TILE = 512
PROMPT_SCAN_STEPS = 64
SAMPLE_TILE_SEQS = 64


def kernel(x_prompt, x_sample, state_ssm_re, state_ssm_im, norm_mix, norm_ffn, norm_final, ssm_w_in, ssm_lambda_re, ssm_lambda_im, ssm_log_dt, ssm_b_re, ssm_b_im, ssm_c_re, ssm_c_im, ssm_d, ssm_w_out, gmlp_w_in, gmlp_v_gain, gmlp_w_spatial, gmlp_b_spatial, gmlp_w_out, ffn_w_gate_up, ffn_w_down):
    bsz, seq, d = x_prompt.shape
    dbsz, dseq, _ = x_sample.shape

    tables = _ssm_tables(ssm_lambda_re[0], ssm_lambda_im[0], ssm_log_dt[0], ssm_b_re[0], ssm_b_im[0],
                         ssm_c_re[0], ssm_c_im[0])

    def ffn(x_rows, layer, w_gu, w_dn, ffn_tm, casts=()):
        return _ffn(x_rows, norm_ffn[layer], w_gu, w_dn, norm_final, tm=ffn_tm, th=FFN_HIDDEN_TILE,
                    final_norm=layer == 1, casts=casts)

    xp = x_prompt.reshape(bsz * seq, d)
    u, w_out0 = _norm_matmul(xp, norm_mix[0], ssm_w_in, 0, ROW_TILE, casts=[(ssm_w_out, 0)])
    z, p_re, p_im, w_gu0, g_w_in = _ssm_scan_split(u.reshape(bsz, seq, d), tables, ssm_d[0], PROMPT_SCAN_STEPS,
                                                   casts=[(ffn_w_gate_up, 0), (gmlp_w_in, 0)])
    xp1, w_dn0, g_w_out = _ssm_out(z.reshape(bsz * seq, d), xp, w_out0, ROW_TILE,
                                   casts=[(ffn_w_down, 0), (gmlp_w_out, 0)])
    xp2, w_gu1, w_dn1 = ffn(xp1, 0, w_gu0, w_dn0, FFN_ROW_TILE, casts=[(ffn_w_gate_up, 1), (ffn_w_down, 1)])
    xp3, = _gmlp_prompt(xp2, norm_mix[1], g_w_in, gmlp_v_gain[0], gmlp_w_spatial[0],
                        jnp.transpose(gmlp_b_spatial[0]), g_w_out, ROW_TILE)
    y_prompt = ffn(xp3, 1, w_gu1, w_dn1, FFN_ROW_TILE)[0].reshape(bsz, seq, d)

    n_bt = dbsz // SAMPLE_TILE_SEQS
    tile_rows = SAMPLE_TILE_SEQS * dseq

    def to_rows(a):
        return jnp.transpose(a.reshape(n_bt, SAMPLE_TILE_SEQS, dseq, d), (0, 2, 1, 3)).reshape(dbsz * dseq, d)

    def from_rows(a):
        return jnp.transpose(a.reshape(n_bt, dseq, SAMPLE_TILE_SEQS, d), (0, 2, 1, 3)).reshape(dbsz, dseq, d)

    xs = to_rows(x_sample)
    h0_re = state_ssm_re[0].reshape(dbsz, SSM_LANES)
    h0_im = state_ssm_im[0].reshape(dbsz, SSM_LANES)
    us, = _norm_matmul(xs, norm_mix[0], ssm_w_in, 0, dbsz * dseq)
    zs, s_re, s_im = _ssm_scan(us, tables, ssm_d[0], h0_re, h0_im, nb=SAMPLE_TILE_SEQS, tt=dseq, n_bt=n_bt, n_tt=1)
    xs1, = _ssm_out(zs, xs, w_out0, tile_rows)
    xs2, = ffn(xs1, 0, w_gu0, w_dn0, dbsz * dseq)
    w_q = jnp.repeat(gmlp_w_spatial[0][:, :dseq, :dseq].reshape(GMLP_HEADS, dseq * dseq).T, GMLP_HEAD_DIM, axis=1)
    b_q = jnp.repeat(gmlp_b_spatial[0][:, :dseq].T, GMLP_HEAD_DIM, axis=1)
    xs3, v_rows = _gmlp_sample(xs2, norm_mix[1], g_w_in, gmlp_v_gain[0], w_q, b_q, g_w_out, tile_rows, dseq)
    y_sample = from_rows(ffn(xs3, 1, w_gu1, w_dn1, dbsz * dseq)[0])

    state_shape = (1, -1, SSM_GROUPS, SSM_STATE)
    return (y_prompt, y_sample,
            p_re.reshape(state_shape), p_im.reshape(state_shape),
            s_re.reshape(state_shape), s_im.reshape(state_shape),
            from_rows(v_rows)[None])
```

```python
import functools

import jax
import jax.numpy as jnp
from jax import lax
from jax.experimental import pallas as pl
from jax.experimental.pallas import tpu as pltpu

D_MODEL = 2048
SSM_GROUPS = 128
SSM_GROUP = 16
SSM_STATE = 64
SSM_LANES = SSM_GROUPS * SSM_STATE
GMLP_WIDTH = D_MODEL
GMLP_HEADS = 16
GMLP_HEAD_DIM = GMLP_WIDTH // GMLP_HEADS
CHUNK = 128
FFN_HIDDEN = 5632
EPS = 1e-6

LANES = 128
SUBLANES = 8
MXU_DIM = 256
VMEM_LIMIT = 56 * 1024 * 1024
BIG_VMEM_LIMIT = 60 * 1024 * 1024

PAIRS = SSM_GROUPS // 2
QUAD_K = 4 * 2 * SSM_GROUP
QUADS = PAIRS // 4
HALF_SLABS = PAIRS // 2
OCTS = D_MODEL // MXU_DIM
OCT_K = SSM_LANES // OCTS

BF16 = jnp.bfloat16
F32 = jnp.float32


def _resident(shape):
    zeros = (0,) * len(shape)
    return pl.BlockSpec(shape, lambda *_: zeros, pipeline_mode=pl.Buffered(1))


def _params(semantics):
    return pltpu.CompilerParams(dimension_semantics=semantics, vmem_limit_bytes=VMEM_LIMIT)


def _rms(x, g):
    ms = jnp.mean(x * x, axis=-1, keepdims=True)
    return (x * lax.rsqrt(ms + EPS)) * g


def _dot(a, b):
    return jnp.dot(a, b, preferred_element_type=F32)


BF16_ROWS = 16


class _SideCast:
    def __init__(self, w, layer, n_blocks, step_of=lambda i: i):
        _, r, c = w.shape
        assert r % (n_blocks * BF16_ROWS) == 0

        def block(*idx):
            return jnp.minimum(step_of(*idx), n_blocks - 1)

        self.operand = w
        self.in_spec = pl.BlockSpec((None, r // n_blocks, c), lambda *idx: (layer, block(*idx), 0))
        self.out_spec = pl.BlockSpec((r // n_blocks, c), lambda *idx: (block(*idx), 0))
        self.out_shape = jax.ShapeDtypeStruct((r, c), BF16)


def _cast_blocks(src_refs, dst_refs):
    for src, dst in zip(src_refs, dst_refs):
        dst[...] = src[...].astype(BF16)


def _slice_count(rows, steps):
    return max(n for n in range(1, steps + 1) if rows % (n * BF16_ROWS) == 0)


def _zoh(lr, li, log_dt):
    dt = jnp.exp(log_dt)
    mag = jnp.exp(lr * dt)
    a_re = mag * jnp.cos(li * dt)
    a_im = mag * jnp.sin(li * dt)
    den = lr * lr + li * li
    nr = a_re - 1.0
    ni = a_im
    q_re = (nr * lr + ni * li) / den
    q_im = (ni * lr - nr * li) / den
    return a_re, a_im, q_re, q_im


def _ssm_tables_kernel(lr_ref, li_ref, ldt_ref, br_ref, bi_ref, cr_ref, ci_ref,
                       are_ref, aim_ref, wb_ref, wcr_ref, wci_ref):
    c, p = SSM_GROUP, SSM_STATE
    a_re, a_im, q_re, q_im = _zoh(lr_ref[...], li_ref[...], ldt_ref[...])
    are_ref[...] = a_re
    aim_ref[...] = a_im

    def per_channel(a):
        return jnp.broadcast_to(a[:, None, :], (SSM_GROUPS, c, p)).reshape(SSM_GROUPS * c, p)

    q_re, q_im = per_channel(q_re), per_channel(q_im)
    br = br_ref[...]
    bi = bi_ref[...]
    bb_re = (q_re * br - q_im * bi).astype(BF16)
    bb_im = (q_re * bi + q_im * br).astype(BF16)

    src = lax.broadcasted_iota(jnp.int32, (2 * p, 2 * MXU_DIM), 0)
    dst = lax.broadcasted_iota(jnp.int32, (2 * p, 2 * MXU_DIM), 1)
    spread = ((src % p == dst % p) & (src // p == (dst % MXU_DIM) // LANES)).astype(BF16)
    row = lax.broadcasted_iota(jnp.int32, (2 * QUAD_K, 2 * MXU_DIM), 0)
    col = lax.broadcasted_iota(jnp.int32, (2 * QUAD_K, 2 * MXU_DIM), 1)
    hit = (row % QUAD_K) // c == 4 * (row // QUAD_K) + 2 * (col // MXU_DIM) + (col % LANES) // p
    for q in range(QUADS):
        rows = slice(QUAD_K * q, QUAD_K * (q + 1))
        both = jnp.concatenate([bb_re[rows, :], bb_im[rows, :]], axis=1)
        tile = _dot(both, spread)
        wb_ref[q] = jnp.where(hit, jnp.concatenate([tile, tile], axis=0), 0.0).astype(BF16)

    groups = MXU_DIM // c
    eye = (lax.broadcasted_iota(jnp.int32, (MXU_DIM, MXU_DIM), 0)
           == lax.broadcasted_iota(jnp.int32, (MXU_DIM, MXU_DIM), 1)).astype(BF16)
    same_group = (lax.broadcasted_iota(jnp.int32, (OCT_K, MXU_DIM), 0) // p
                  == lax.broadcasted_iota(jnp.int32, (OCT_K, MXU_DIM), 1) // c)
    transpose_lhs = (((0,), (0,)), ((), ()))
    for o in range(OCTS):
        rows = slice(MXU_DIM * o, MXU_DIM * (o + 1))
        for src_ref, dst_ref, sign in ((cr_ref, wcr_ref, 1.0), (ci_ref, wci_ref, -1.0)):
            blk = (sign * src_ref[rows, :]).astype(BF16)
            by_state = lax.dot_general(blk, eye, transpose_lhs, preferred_element_type=F32)
            dst_ref[o] = jnp.where(same_group, jnp.concatenate([by_state] * groups, axis=0), 0.0).astype(BF16)


def _ssm_tables(lambda_re, lambda_im, log_dt, b_re, b_im, c_re, c_im):
    g, p, c = b_re.shape
    rows = (g * c, p)
    a_shape = jax.ShapeDtypeStruct((g, p), F32)
    wc_shape = jax.ShapeDtypeStruct((OCTS, OCT_K, MXU_DIM), BF16)
    a_re, a_im, w_b, w_cr, w_ci = pl.pallas_call(
        _ssm_tables_kernel,
        out_shape=(a_shape, a_shape, jax.ShapeDtypeStruct((QUADS, 2 * QUAD_K, 2 * MXU_DIM), BF16), wc_shape, wc_shape),
        compiler_params=pltpu.CompilerParams(vmem_limit_bytes=VMEM_LIMIT),
        name="ssm_tables",
    )(lambda_re, lambda_im, jnp.broadcast_to(log_dt[:, None], (g, p)),
      jnp.transpose(b_re, (0, 2, 1)).reshape(rows), jnp.transpose(b_im, (0, 2, 1)).reshape(rows),
      c_re.reshape(rows), c_im.reshape(rows))
    a_re_t = jnp.broadcast_to(a_re.reshape(1, SSM_LANES), (SUBLANES, SSM_LANES))
    a_im_t = jnp.broadcast_to(a_im.reshape(1, SSM_LANES), (SUBLANES, SSM_LANES))
    return (w_b, a_re_t, a_im_t, w_cr, w_ci)


def _norm_matmul_kernel(x_ref, g_ref, w_ref, *rest):
    n_cast = (len(rest) - 2) // 2
    o_ref, wb_ref = rest[n_cast], rest[-1]
    _cast_blocks(rest[:n_cast], rest[n_cast + 1:-1])

    @pl.when(pl.program_id(0) == 0)
    def _():
        wb_ref[...] = w_ref[...].astype(BF16)

    half = x_ref.shape[0] // 2
    for rows in (slice(0, half), slice(half, 2 * half)):
        hn = _rms(x_ref[rows, :], g_ref[...]).astype(BF16)
        o_ref[rows, :] = _dot(hn, wb_ref[...])


def _norm_matmul(x, g, w, layer, tm, casts=()):
    m, d = x.shape
    n = w.shape[2]
    casts = [_SideCast(cw, cl, m // tm) for cw, cl in casts]
    return pl.pallas_call(
        _norm_matmul_kernel,
        grid=(m // tm,),
        in_specs=[pl.BlockSpec((tm, d), lambda i: (i, 0)), _resident((1, d)),
                  pl.BlockSpec((None, d, n), lambda i: (layer, 0, 0), pipeline_mode=pl.Buffered(1))]
                 + [c.in_spec for c in casts],
        out_specs=[pl.BlockSpec((tm, n), lambda i: (i, 0))] + [c.out_spec for c in casts],
        out_shape=[jax.ShapeDtypeStruct((m, n), F32)] + [c.out_shape for c in casts],
        scratch_shapes=[pltpu.VMEM((d, n), BF16)],
        compiler_params=_params(("arbitrary",)),
        name="ssm_norm_in_proj",
    )(x, g.reshape(1, d), w, *[c.operand for c in casts])


SCAN_COLS = 1024


def _ssm_scan_kernel(u_ref, wb_ref, are_ref, aim_ref, wcr_ref, wci_ref, dsk_ref, h0r_ref, h0i_ref,
                     z_ref, sr_ref, si_ref, bur_ref, bui_ref, *, nb, tt):
    @pl.when(pl.program_id(1) == 0)
    def _():
        sr_ref[...] = h0r_ref[...]
        si_ref[...] = h0i_ref[...]

    u = u_ref[...]
    ub = u.astype(BF16)
    for q in range(QUADS):
        lhs = ub[:, QUAD_K * q:QUAD_K * (q + 1)]
        for h in range(2):
            res = _dot(lhs, wb_ref[q, QUAD_K * h:QUAD_K * (h + 1), :])
            for s in range(2):
                j = 4 * q + 2 * h + s
                bur_ref[:, LANES * j:LANES * (j + 1)] = res[:, MXU_DIM * s:MXU_DIM * s + LANES]
                bui_ref[:, LANES * j:LANES * (j + 1)] = res[:, MXU_DIM * s + LANES:MXU_DIM * (s + 1)]

    for cb in range(SSM_LANES // SCAN_COLS):
        cols = slice(cb * SCAN_COLS, (cb + 1) * SCAN_COLS)
        ar = are_ref[:, cols]
        ai = aim_ref[:, cols]
        for sg in range(nb // SUBLANES):
            seqs = slice(sg * SUBLANES, (sg + 1) * SUBLANES)
            sr = sr_ref[seqs, cols]
            si = si_ref[seqs, cols]
            for t in range(tt):
                rows = slice(t * nb + sg * SUBLANES, t * nb + (sg + 1) * SUBLANES)
                nr = ar * sr - ai * si + bur_ref[rows, cols]
                ni = ar * si + ai * sr + bui_ref[rows, cols]
                sr, si = nr, ni
                bur_ref[rows, cols] = sr
                bui_ref[rows, cols] = si
            sr_ref[seqs, cols] = sr
            si_ref[seqs, cols] = si

    for o in range(OCTS):
        kk = slice(o * OCT_K, (o + 1) * OCT_K)
        nn = slice(o * MXU_DIM, (o + 1) * MXU_DIM)
        y = _dot(bur_ref[:, kk].astype(BF16), wcr_ref[o]) + _dot(bui_ref[:, kk].astype(BF16), wci_ref[o])
        y = y + dsk_ref[:, nn] * u[:, nn]
        z_ref[:, nn] = jax.nn.gelu(y).astype(BF16)


def _ssm_scan(u, tables, d_skip, h0_re, h0_im, *, nb, tt, n_bt, n_tt):
    w_b, a_re_t, a_im_t, w_cr, w_ci = tables
    m, d = u.shape
    r = nb * tt
    assert m == r * n_bt * n_tt and nb % SUBLANES == 0
    rows = pl.BlockSpec((r, d), lambda b, t: (b * n_tt + t, 0))
    state = pl.BlockSpec((nb, SSM_LANES), lambda b, t: (b, 0))
    st_shape = jax.ShapeDtypeStruct((nb * n_bt, SSM_LANES), F32)
    return pl.pallas_call(
        functools.partial(_ssm_scan_kernel, nb=nb, tt=tt),
        grid=(n_bt, n_tt),
        in_specs=[rows, _resident(w_b.shape), _resident(a_re_t.shape), _resident(a_im_t.shape),
                  _resident(w_cr.shape), _resident(w_ci.shape), _resident((1, d)), state, state],
        out_specs=(rows, state, state),
        out_shape=(jax.ShapeDtypeStruct((m, d), BF16), st_shape, st_shape),
        scratch_shapes=[pltpu.VMEM((r, SSM_LANES), F32), pltpu.VMEM((r, SSM_LANES), F32)],
        compiler_params=_params(("parallel", "arbitrary")),
        name="ssm_scan",
    )(u, w_b, a_re_t, a_im_t, w_cr, w_ci, d_skip.reshape(1, d), h0_re, h0_im)


SCAN_SLABS = 8


def _ssm_scan_split_kernel(u_hbm, u_ref, wb_ref, are_ref, aim_ref, wcr_ref, wci_ref, dsk_ref, *rest, nb, tt):
    n_cast = (len(rest) - 8) // 2
    z_ref, sr_ref, si_ref = rest[n_cast:n_cast + 3]
    lhs_ref, bur_ref, bui_ref, y_ref, sems = rest[-5:]
    _cast_blocks(rest[:n_cast], rest[n_cast + 3:-5])
    r = nb * tt
    i = pl.program_id(0)

    def lhs_copies(step):
        return [pltpu.make_async_copy(u_hbm.at[b, pl.ds(step * tt, tt), :],
                                      lhs_ref.at[h, :, 2 * b + h, :],
                                      sems.at[2 * b + h])
                for b in range(nb) for h in range(2)]

    @pl.when(i == 0)
    def _():
        lhs_ref[...] = jnp.zeros_like(lhs_ref)
        sr_ref[...] = jnp.zeros_like(sr_ref)
        si_ref[...] = jnp.zeros_like(si_ref)
        for copy in lhs_copies(0):
            copy.start()

    for copy in lhs_copies(i):
        copy.wait()

    def project_in(q):
        lanes = slice(LANES * q, LANES * (q + 1))
        lhs = jnp.concatenate([lhs_ref[0, :, :, lanes].reshape(2 * r, LANES),
                               lhs_ref[1, :, :, lanes].reshape(2 * r, LANES)], axis=1).astype(BF16)
        res = _dot(lhs, wb_ref[q])
        for s in range(2):
            bur_ref[2 * q + s] = res[:, MXU_DIM * s:MXU_DIM * s + LANES]
            bui_ref[2 * q + s] = res[:, MXU_DIM * s + LANES:MXU_DIM * (s + 1)]

    def recur(slabs):
        ar = [are_ref[:, LANES * k:LANES * (k + 1)] for k in slabs]
        ai = [aim_ref[:, LANES * k:LANES * (k + 1)] for k in slabs]
        sr = [sr_ref[:, LANES * k:LANES * (k + 1)] for k in slabs]
        si = [si_ref[:, LANES * k:LANES * (k + 1)] for k in slabs]
        for t in range(tt):
            rows = slice(SUBLANES * t, SUBLANES * (t + 1))
            for n, k in enumerate(slabs):
                nr = ar[n] * sr[n] - ai[n] * si[n] + bur_ref[k, rows, :]
                ni = ar[n] * si[n] + ai[n] * sr[n] + bui_ref[k, rows, :]
                sr[n], si[n] = nr, ni
                bur_ref[k, rows, :] = nr
                bui_ref[k, rows, :] = ni
        for n, k in enumerate(slabs):
            sr_ref[:, LANES * k:LANES * (k + 1)] = sr[n]
            si_ref[:, LANES * k:LANES * (k + 1)] = si[n]

    def project_out(o):
        order = [(2 * (2 * o + ql) + s, h) for ql in range(2) for h in range(2) for s in range(2)]
        lre = jnp.concatenate([bur_ref[k, pl.ds(h, r, stride=2), :] for k, h in order], axis=1).astype(BF16)
        lim = jnp.concatenate([bui_ref[k, pl.ds(h, r, stride=2), :] for k, h in order], axis=1).astype(BF16)
        y = _dot(lre, wcr_ref[o]) + _dot(lim, wci_ref[o])
        for half in range(2):
            y_ref[half] = y[:, LANES * half:LANES * (half + 1)]
        for b in range(nb):
            for half in range(2):
                cols = slice(MXU_DIM * o + LANES * half, MXU_DIM * o + LANES * (half + 1))
                yb = y_ref[half, pl.ds(b, tt, stride=nb), :] + dsk_ref[:, cols] * u_ref[b, :, cols]
                z_ref[b, :, cols] = jax.nn.gelu(yb).astype(BF16)

    for q in range(QUADS):
        project_in(q)
    for k0 in range(0, HALF_SLABS, SCAN_SLABS):
        recur(range(k0, k0 + SCAN_SLABS))

    @pl.when(i + 1 < pl.num_programs(0))
    def _():
        for copy in lhs_copies(i + 1):
            copy.start()

    for o in range(OCTS):
        project_out(o)


def _ssm_scan_split(u, tables, d_skip, tt, casts):
    w_b, a_re_t, a_im_t, w_cr, w_ci = tables
    nb, seq, d = u.shape
    assert 2 * nb == SUBLANES and seq % tt == 0
    casts = [_SideCast(cw, layer, seq // tt) for cw, layer in casts]

    def split_lanes(a):
        halves = jnp.transpose(a[0].reshape(QUADS, 2, 2 * LANES), (1, 0, 2)).reshape(2, HALF_SLABS * LANES)
        return jnp.tile(halves, (nb, 1))

    rows = pl.BlockSpec((nb, tt, d), lambda i: (0, i, 0))
    st_shape = jax.ShapeDtypeStruct((SUBLANES, HALF_SLABS * LANES), F32)
    r = nb * tt
    state = pl.BlockSpec(st_shape.shape, lambda i: (0, 0))
    z, s_re, s_im, *w_cast = pl.pallas_call(
        functools.partial(_ssm_scan_split_kernel, nb=nb, tt=tt),
        grid=(seq // tt,),
        in_specs=[pl.BlockSpec(memory_space=pl.ANY), rows, _resident(w_b.shape), _resident(st_shape.shape),
                  _resident(st_shape.shape), _resident(w_cr.shape), _resident(w_ci.shape), _resident((1, d))]
                 + [c.in_spec for c in casts],
        out_specs=[rows, state, state] + [c.out_spec for c in casts],
        out_shape=[jax.ShapeDtypeStruct((nb, seq, d), BF16), st_shape, st_shape] + [c.out_shape for c in casts],
        scratch_shapes=[pltpu.VMEM((2, tt, 2 * nb, d), F32),
                        pltpu.VMEM((HALF_SLABS, 2 * r, LANES), F32),
                        pltpu.VMEM((HALF_SLABS, 2 * r, LANES), F32),
                        pltpu.VMEM((2, r, LANES), F32),
                        pltpu.SemaphoreType.DMA((2 * nb,))],
        compiler_params=pltpu.CompilerParams(dimension_semantics=("arbitrary",), vmem_limit_bytes=BIG_VMEM_LIMIT),
        name="ssm_scan_split",
    )(u, u, w_b, split_lanes(a_re_t), split_lanes(a_im_t), w_cr, w_ci, d_skip.reshape(1, d),
      *[c.operand for c in casts])

    def join(s):
        return jnp.transpose(s.reshape(nb, 2, QUADS, 2 * LANES), (0, 2, 1, 3)).reshape(nb, SSM_LANES)

    return (z, join(s_re), join(s_im), *w_cast)


SSM_OUT_COLS = 512


def _ssm_out_kernel(z_ref, x_ref, w_ref, *rest):
    o_ref = rest[len(rest) // 2]
    _cast_blocks(rest[:len(rest) // 2], rest[len(rest) // 2 + 1:])
    z = z_ref[...]
    for c in range(D_MODEL // SSM_OUT_COLS):
        cols = slice(c * SSM_OUT_COLS, (c + 1) * SSM_OUT_COLS)
        gate_cols = slice(D_MODEL + c * SSM_OUT_COLS, D_MODEL + (c + 1) * SSM_OUT_COLS)
        val = _dot(z, w_ref[:, cols])
        gate = _dot(z, w_ref[:, gate_cols])
        o_ref[:, cols] = x_ref[:, cols] + val * jax.nn.sigmoid(gate)


def _ssm_out(z, x, w, tm, casts=()):
    m, d = x.shape
    rows = pl.BlockSpec((tm, d), lambda i: (i, 0))
    casts = [_SideCast(cw, layer, m // tm) for cw, layer in casts]
    return pl.pallas_call(
        _ssm_out_kernel,
        grid=(m // tm,),
        in_specs=[rows, rows, _resident(w.shape)] + [c.in_spec for c in casts],
        out_specs=[rows] + [c.out_spec for c in casts],
        out_shape=[jax.ShapeDtypeStruct((m, d), F32)] + [c.out_shape for c in casts],
        compiler_params=_params(("parallel",)),
        name="ssm_out_glu",
    )(z, x, w, *[c.operand for c in casts])


def _ffn_kernel(x_ref, g_ref, wg_ref, wu_ref, wd_ref, gf_ref, *rest, final_norm):
    n_cast = (len(rest) - 2) // 2
    o_ref, hn_ref = rest[n_cast], rest[-1]
    _cast_blocks(rest[:n_cast], rest[n_cast + 1:-1])
    h = pl.program_id(1)

    @pl.when(h == 0)
    def _():
        x = x_ref[...]
        hn_ref[...] = _rms(x, g_ref[...]).astype(BF16)
        o_ref[...] = x

    hn = hn_ref[...]
    act = (jax.nn.silu(_dot(hn, wg_ref[...])) * _dot(hn, wu_ref[...])).astype(BF16)
    o_ref[...] += _dot(act, wd_ref[...])

    if final_norm:
        @pl.when(h == pl.num_programs(1) - 1)
        def _():
            o_ref[...] = _rms(o_ref[...], gf_ref[...])


def _ffn(x, g, w_gate_up, w_down, g_final, *, tm, th, final_norm, casts=()):
    m, d = x.shape
    n_h = FFN_HIDDEN // th
    steps = (m // tm) * n_h
    rows = pl.BlockSpec((tm, d), lambda i, h: (i, 0))
    casts = [_SideCast(cw, layer, _slice_count(cw.shape[1], steps), lambda i, h: i * n_h + h) for cw, layer in casts]
    return pl.pallas_call(
        functools.partial(_ffn_kernel, final_norm=final_norm),
        grid=(m // tm, n_h),
        in_specs=[rows,
                  _resident((1, d)),
                  pl.BlockSpec((d, th), lambda i, h: (0, h)),
                  pl.BlockSpec((d, th), lambda i, h: (0, n_h + h)),
                  pl.BlockSpec((th, d), lambda i, h: (h, 0)),
                  _resident((1, d))] + [c.in_spec for c in casts],
        out_specs=[rows] + [c.out_spec for c in casts],
        out_shape=[jax.ShapeDtypeStruct((m, d), F32)] + [c.out_shape for c in casts],
        scratch_shapes=[pltpu.VMEM((tm, d), BF16)],
        compiler_params=pltpu.CompilerParams(dimension_semantics=("arbitrary", "arbitrary"),
                                             vmem_limit_bytes=BIG_VMEM_LIMIT),
        name="ffn_swiglu",
    )(x, g.reshape(1, d), w_gate_up, w_gate_up, w_down, g_final.reshape(1, d), *[c.operand for c in casts])


def _gmlp_gate_inputs(x, g, win_ref, v_gain):
    hn = _rms(x, g).astype(BF16)
    v = jax.nn.gelu(_dot(hn, win_ref[:, GMLP_WIDTH:]))
    vc = v - jnp.mean(v, axis=-1, keepdims=True)
    vn = (vc * lax.rsqrt(jnp.mean(vc * vc, axis=-1, keepdims=True) + EPS)) * v_gain
    u = jax.nn.gelu(_dot(hn, win_ref[:, :GMLP_WIDTH]))
    return u, vn


def _gmlp_prompt_kernel(x_ref, g_ref, win_ref, vg_ref, ws_ref, bs_ref, wout_ref, *rest):
    n_cast = (len(rest) - 2) // 2
    o_ref, gate_ref = rest[n_cast], rest[-1]
    _cast_blocks(rest[:n_cast], rest[n_cast + 1:-1])
    x = x_ref[...]
    u, vn = _gmlp_gate_inputs(x, g_ref[...], win_ref, vg_ref[...])
    vb = vn.astype(BF16)
    q_idx = lax.broadcasted_iota(jnp.int32, (CHUNK, CHUNK), 0)
    k_idx = lax.broadcasted_iota(jnp.int32, (CHUNK, CHUNK), 1)
    causal = k_idx <= q_idx
    for h in range(GMLP_HEADS):
        cols = slice(h * GMLP_HEAD_DIM, (h + 1) * GMLP_HEAD_DIM)
        ws = jnp.where(causal, ws_ref[h], 0.0).astype(BF16)
        bias = bs_ref[:, h:h + 1]
        for c in range(0, x.shape[0] // CHUNK, 2):
            lo = slice(c * CHUNK, (c + 1) * CHUNK)
            hi = slice((c + 1) * CHUNK, (c + 2) * CHUNK)
            s = _dot(ws, jnp.concatenate([vb[lo, cols], vb[hi, cols]], axis=1)) + bias
            gate_ref[lo, cols] = (u[lo, cols] * s[:, :GMLP_HEAD_DIM]).astype(BF16)
            gate_ref[hi, cols] = (u[hi, cols] * s[:, GMLP_HEAD_DIM:]).astype(BF16)
    o_ref[...] = x + _dot(gate_ref[...], wout_ref[...])


def _gmlp_sample_kernel(x_ref, g_ref, win_ref, vg_ref, wq_ref, bq_ref, wout_ref, o_ref, v_ref, *, steps):
    x = x_ref[...]
    u, vn = _gmlp_gate_inputs(x, g_ref[...], win_ref, vg_ref[...])
    v_ref[...] = vn
    nseq = x.shape[0] // steps
    gates = []
    for q in range(steps):
        s = bq_ref[q:q + 1, :]
        for k in range(q + 1):
            s = s + wq_ref[q * steps + k:q * steps + k + 1, :] * vn[k * nseq:(k + 1) * nseq, :]
        gates.append(u[q * nseq:(q + 1) * nseq, :] * s)
    gate = jnp.concatenate(gates, axis=0).astype(BF16)
    o_ref[...] = x + _dot(gate, wout_ref[...])


def _gmlp_prompt(x, g, w_in, v_gain, w_spatial, b_spatial_t, w_out, tm, casts=()):
    m, d = x.shape
    rows = pl.BlockSpec((tm, d), lambda i: (i, 0))
    casts = [_SideCast(cw, layer, m // tm) for cw, layer in casts]
    return pl.pallas_call(
        _gmlp_prompt_kernel,
        grid=(m // tm,),
        in_specs=[rows, _resident((1, d)), _resident(w_in.shape), _resident((1, GMLP_WIDTH)),
                  _resident(w_spatial.shape), _resident(b_spatial_t.shape), _resident(w_out.shape)]
                 + [c.in_spec for c in casts],
        out_specs=[rows] + [c.out_spec for c in casts],
        out_shape=[jax.ShapeDtypeStruct((m, d), F32)] + [c.out_shape for c in casts],
        scratch_shapes=[pltpu.VMEM((tm, GMLP_WIDTH), BF16)],
        compiler_params=_params(("parallel",)),
        name="gmlp_prompt",
    )(x, g.reshape(1, d), w_in, v_gain.reshape(1, GMLP_WIDTH), w_spatial, b_spatial_t, w_out,
      *[c.operand for c in casts])


def _gmlp_sample(x, g, w_in, v_gain, w_q, b_q, w_out, tm, steps):
    m, d = x.shape
    rows = pl.BlockSpec((tm, d), lambda i: (i, 0))
    return pl.pallas_call(
        functools.partial(_gmlp_sample_kernel, steps=steps),
        grid=(m // tm,),
        in_specs=[rows, _resident((1, d)), _resident(w_in.shape), _resident((1, GMLP_WIDTH)),
                  _resident(w_q.shape), _resident(b_q.shape), _resident(w_out.shape)],
        out_specs=(rows, pl.BlockSpec((tm, GMLP_WIDTH), lambda i: (i, 0))),
        out_shape=(jax.ShapeDtypeStruct((m, d), F32), jax.ShapeDtypeStruct((m, GMLP_WIDTH), F32)),
        compiler_params=_params(("parallel",)),
        name="gmlp_sample",
    )(x, g.reshape(1, d), w_in, v_gain.reshape(1, GMLP_WIDTH), w_q, b_q, w_out)


ROW_TILE = 512
FFN_ROW_TILE = 1024
FFN_HIDDEN_TILE = 512
PROMPT_SCAN_STEPS = 64
SAMPLE_TILE_SEQS = 64


def kernel(x_prompt, x_sample, state_ssm_re, state_ssm_im, norm_mix, norm_ffn, norm_final, ssm_w_in, ssm_lambda_re, ssm_lambda_im, ssm_log_dt, ssm_b_re, ssm_b_im, ssm_c_re, ssm_c_im, ssm_d, ssm_w_out, gmlp_w_in, gmlp_v_gain, gmlp_w_spatial, gmlp_b_spatial, gmlp_w_out, ffn_w_gate_up, ffn_w_down):
    bsz, seq, d = x_prompt.shape
    dbsz, dseq, _ = x_sample.shape

    tables = _ssm_tables(ssm_lambda_re[0], ssm_lambda_im[0], ssm_log_dt[0], ssm_b_re[0], ssm_b_im[0],
                         ssm_c_re[0], ssm_c_im[0])

    def ffn(x_rows, layer, w_gu, w_dn, ffn_tm, casts=()):
        return _ffn(x_rows, norm_ffn[layer], w_gu, w_dn, norm_final, tm=ffn_tm, th=FFN_HIDDEN_TILE,
                    final_norm=layer == 1, casts=casts)

    xp = x_prompt.reshape(bsz * seq, d)
    u, w_out0 = _norm_matmul(xp, norm_mix[0], ssm_w_in, 0, ROW_TILE, casts=[(ssm_w_out, 0)])
    z, p_re, p_im, w_gu0, g_w_in = _ssm_scan_split(u.reshape(bsz, seq, d), tables, ssm_d[0], PROMPT_SCAN_STEPS,
                                                   casts=[(ffn_w_gate_up, 0), (gmlp_w_in, 0)])
    xp1, w_dn0, g_w_out = _ssm_out(z.reshape(bsz * seq, d), xp, w_out0, ROW_TILE,
                                   casts=[(ffn_w_down, 0), (gmlp_w_out, 0)])
    xp2, w_gu1, w_dn1 = ffn(xp1, 0, w_gu0, w_dn0, FFN_ROW_TILE, casts=[(ffn_w_gate_up, 1), (ffn_w_down, 1)])
    xp3, = _gmlp_prompt(xp2, norm_mix[1], g_w_in, gmlp_v_gain[0], gmlp_w_spatial[0],
                        jnp.transpose(gmlp_b_spatial[0]), g_w_out, ROW_TILE)
    y_prompt = ffn(xp3, 1, w_gu1, w_dn1, FFN_ROW_TILE)[0].reshape(bsz, seq, d)

    n_bt = dbsz // SAMPLE_TILE_SEQS
    tile_rows = SAMPLE_TILE_SEQS * dseq

    def to_rows(a):
        return jnp.transpose(a.reshape(n_bt, SAMPLE_TILE_SEQS, dseq, d), (0, 2, 1, 3)).reshape(dbsz * dseq, d)

    def from_rows(a):
        return jnp.transpose(a.reshape(n_bt, dseq, SAMPLE_TILE_SEQS, d), (0, 2, 1, 3)).reshape(dbsz, dseq, d)

    xs = to_rows(x_sample)
    h0_re = state_ssm_re[0].reshape(dbsz, SSM_LANES)
    h0_im = state_ssm_im[0].reshape(dbsz, SSM_LANES)
    us, = _norm_matmul(xs, norm_mix[0], ssm_w_in, 0, dbsz * dseq)
    zs, s_re, s_im = _ssm_scan(us, tables, ssm_d[0], h0_re, h0_im, nb=SAMPLE_TILE_SEQS, tt=dseq, n_bt=n_bt, n_tt=1)
    xs1, = _ssm_out(zs, xs, w_out0, tile_rows)
    xs2, = ffn(xs1, 0, w_gu0, w_dn0, dbsz * dseq)
    w_q = jnp.repeat(gmlp_w_spatial[0][:, :dseq, :dseq].reshape(GMLP_HEADS, dseq * dseq).T, GMLP_HEAD_DIM, axis=1)
    b_q = jnp.repeat(gmlp_b_spatial[0][:, :dseq].T, GMLP_HEAD_DIM, axis=1)
    xs3, v_rows = _gmlp_sample(xs2, norm_mix[1], g_w_in, gmlp_v_gain[0], w_q, b_q, g_w_out, tile_rows, dseq)
    y_sample = from_rows(ffn(xs3, 1, w_gu1, w_dn1, dbsz * dseq)[0])

    state_shape = (1, -1, SSM_GROUPS, SSM_STATE)
    return (y_prompt, y_sample,
            p_re.reshape(state_shape), p_im.reshape(state_shape),
            s_re.reshape(state_shape), s_im.reshape(state_shape),
            from_rows(v_rows)[None])
```

```python
import functools

import jax
import jax.numpy as jnp
from jax import lax
from jax.experimental import pallas as pl
from jax.experimental.pallas import tpu as pltpu

D_MODEL = 2048
SSM_GROUPS = 128
SSM_GROUP = 16
SSM_STATE = 64
SSM_LANES = SSM_GROUPS * SSM_STATE
GMLP_WIDTH = D_MODEL
GMLP_HEADS = 16
GMLP_HEAD_DIM = GMLP_WIDTH // GMLP_HEADS
CHUNK = 128
FFN_HIDDEN = 5632
EPS = 1e-6

LANES = 128
SUBLANES = 8
MXU_DIM = 256
VMEM_LIMIT = 56 * 1024 * 1024
BIG_VMEM_LIMIT = 60 * 1024 * 1024

PAIRS = SSM_GROUPS // 2
QUAD_K = 4 * 2 * SSM_GROUP
QUADS = PAIRS // 4
HALF_SLABS = PAIRS // 2
OCTS = D_MODEL // MXU_DIM
OCT_K = SSM_LANES // OCTS

BF16 = jnp.bfloat16
F32 = jnp.float32


def _resident(shape):
    zeros = (0,) * len(shape)
    return pl.BlockSpec(shape, lambda *_: zeros, pipeline_mode=pl.Buffered(1))


def _params(semantics):
    return pltpu.CompilerParams(dimension_semantics=semantics, vmem_limit_bytes=VMEM_LIMIT)


def _rms(x, g):
    ms = jnp.mean(x * x, axis=-1, keepdims=True)
    return (x * lax.rsqrt(ms + EPS)) * g


def _dot(a, b):
    return jnp.dot(a, b, preferred_element_type=F32)


BF16_ROWS = 16


class _SideCast:
    def __init__(self, w, layer, n_blocks, step_of=lambda i: i):
        _, r, c = w.shape
        assert r % (n_blocks * BF16_ROWS) == 0

        def block(*idx):
            return jnp.minimum(step_of(*idx), n_blocks - 1)

        self.operand = w
        self.in_spec = pl.BlockSpec((None, r // n_blocks, c), lambda *idx: (layer, block(*idx), 0))
        self.out_spec = pl.BlockSpec((r // n_blocks, c), lambda *idx: (block(*idx), 0))
        self.out_shape = jax.ShapeDtypeStruct((r, c), BF16)


def _cast_blocks(src_refs, dst_refs):
    for src, dst in zip(src_refs, dst_refs):
        dst[...] = src[...].astype(BF16)


def _slice_count(rows, steps):
    return max(n for n in range(1, steps + 1) if rows % (n * BF16_ROWS) == 0)


def _zoh(lr, li, log_dt):
    dt = jnp.exp(log_dt)
    mag = jnp.exp(lr * dt)
    a_re = mag * jnp.cos(li * dt)
    a_im = mag * jnp.sin(li * dt)
    den = lr * lr + li * li
    nr = a_re - 1.0
    ni = a_im
    q_re = (nr * lr + ni * li) / den
    q_im = (ni * lr - nr * li) / den
    return a_re, a_im, q_re, q_im


def _ssm_tables_kernel(lr_ref, li_ref, ldt_ref, br_ref, bi_ref, cr_ref, ci_ref,
                       are_ref, aim_ref, wb_ref, wcr_ref, wci_ref):
    c, p = SSM_GROUP, SSM_STATE
    a_re, a_im, q_re, q_im = _zoh(lr_ref[...], li_ref[...], ldt_ref[...])
    are_ref[...] = a_re
    aim_ref[...] = a_im

    def per_channel(a):
        return jnp.broadcast_to(a[:, None, :], (SSM_GROUPS, c, p)).reshape(SSM_GROUPS * c, p)

    q_re, q_im = per_channel(q_re), per_channel(q_im)
    br = br_ref[...]
    bi = bi_ref[...]
    bb_re = (q_re * br - q_im * bi).astype(BF16)
    bb_im = (q_re * bi + q_im * br).astype(BF16)

    src = lax.broadcasted_iota(jnp.int32, (2 * p, 2 * MXU_DIM), 0)
    dst = lax.broadcasted_iota(jnp.int32, (2 * p, 2 * MXU_DIM), 1)
    spread = ((src % p == dst % p) & (src // p == (dst % MXU_DIM) // LANES)).astype(BF16)
    row = lax.broadcasted_iota(jnp.int32, (2 * QUAD_K, 2 * MXU_DIM), 0)
    col = lax.broadcasted_iota(jnp.int32, (2 * QUAD_K, 2 * MXU_DIM), 1)
    hit = (row % QUAD_K) // c == 4 * (row // QUAD_K) + 2 * (col // MXU_DIM) + (col % LANES) // p
    for q in range(QUADS):
        rows = slice(QUAD_K * q, QUAD_K * (q + 1))
        both = jnp.concatenate([bb_re[rows, :], bb_im[rows, :]], axis=1)
        tile = _dot(both, spread)
        wb_ref[q] = jnp.where(hit, jnp.concatenate([tile, tile], axis=0), 0.0).astype(BF16)

    groups = MXU_DIM // c
    eye = (lax.broadcasted_iota(jnp.int32, (MXU_DIM, MXU_DIM), 0)
           == lax.broadcasted_iota(jnp.int32, (MXU_DIM, MXU_DIM), 1)).astype(BF16)
    same_group = (lax.broadcasted_iota(jnp.int32, (OCT_K, MXU_DIM), 0) // p
                  == lax.broadcasted_iota(jnp.int32, (OCT_K, MXU_DIM), 1) // c)
    transpose_lhs = (((0,), (0,)), ((), ()))
    for o in range(OCTS):
        rows = slice(MXU_DIM * o, MXU_DIM * (o + 1))
        for src_ref, dst_ref, sign in ((cr_ref, wcr_ref, 1.0), (ci_ref, wci_ref, -1.0)):
            blk = (sign * src_ref[rows, :]).astype(BF16)
            by_state = lax.dot_general(blk, eye, transpose_lhs, preferred_element_type=F32)
            dst_ref[o] = jnp.where(same_group, jnp.concatenate([by_state] * groups, axis=0), 0.0).astype(BF16)


def _ssm_tables(lambda_re, lambda_im, log_dt, b_re, b_im, c_re, c_im):
    g, p, c = b_re.shape
    rows = (g * c, p)
    a_shape = jax.ShapeDtypeStruct((g, p), F32)
    wc_shape = jax.ShapeDtypeStruct((OCTS, OCT_K, MXU_DIM), BF16)
    a_re, a_im, w_b, w_cr, w_ci = pl.pallas_call(
        _ssm_tables_kernel,
        out_shape=(a_shape, a_shape, jax.ShapeDtypeStruct((QUADS, 2 * QUAD_K, 2 * MXU_DIM), BF16), wc_shape, wc_shape),
        compiler_params=pltpu.CompilerParams(vmem_limit_bytes=VMEM_LIMIT),
        name="ssm_tables",
    )(lambda_re, lambda_im, jnp.broadcast_to(log_dt[:, None], (g, p)),
      jnp.transpose(b_re, (0, 2, 1)).reshape(rows), jnp.transpose(b_im, (0, 2, 1)).reshape(rows),
      c_re.reshape(rows), c_im.reshape(rows))
    a_re_t = jnp.broadcast_to(a_re.reshape(1, SSM_LANES), (SUBLANES, SSM_LANES))
    a_im_t = jnp.broadcast_to(a_im.reshape(1, SSM_LANES), (SUBLANES, SSM_LANES))
    return (w_b, a_re_t, a_im_t, w_cr, w_ci)


def _norm_matmul_kernel(x_ref, g_ref, w_ref, *rest):
    n_cast = (len(rest) - 3) // 2
    o_ref, wc_ref, wb_ref = rest[n_cast], rest[n_cast + 1], rest[-1]
    _cast_blocks(rest[:n_cast], rest[n_cast + 2:-1])

    @pl.when(pl.program_id(0) == 0)
    def _():
        wb_ref[...] = w_ref[...].astype(BF16)

    slice_rows = wc_ref.shape[0]
    wc_ref[...] = wb_ref[pl.ds(pl.multiple_of(pl.program_id(0) * slice_rows, slice_rows), slice_rows), :]

    half = x_ref.shape[0] // 2
    for rows in (slice(0, half), slice(half, 2 * half)):
        hn = _rms(x_ref[rows, :], g_ref[...]).astype(BF16)
        o_ref[rows, :] = _dot(hn, wb_ref[...])


def _norm_matmul(x, g, w, layer, tm, casts=()):
    m, d = x.shape
    n = w.shape[2]
    casts = [_SideCast(cw, cl, m // tm) for cw, cl in casts]
    return pl.pallas_call(
        _norm_matmul_kernel,
        grid=(m // tm,),
        in_specs=[pl.BlockSpec((tm, d), lambda i: (i, 0)), _resident((1, d)),
                  pl.BlockSpec((None, d, n), lambda i: (layer, 0, 0), pipeline_mode=pl.Buffered(1))]
                 + [c.in_spec for c in casts],
        out_specs=[pl.BlockSpec((tm, n), lambda i: (i, 0)), pl.BlockSpec((d // (m // tm), n), lambda i: (i, 0))]
                  + [c.out_spec for c in casts],
        out_shape=[jax.ShapeDtypeStruct((m, n), F32), jax.ShapeDtypeStruct((d, n), BF16)]
                  + [c.out_shape for c in casts],
        scratch_shapes=[pltpu.VMEM((d, n), BF16)],
        compiler_params=_params(("arbitrary",)),
        name="ssm_norm_in_proj",
    )(x, g.reshape(1, d), w, *[c.operand for c in casts])


def _norm_matmul_bf16_kernel(x_ref, g_ref, w_ref, o_ref):
    half = x_ref.shape[0] // 2
    for rows in (slice(0, half), slice(half, 2 * half)):
        hn = _rms(x_ref[rows, :], g_ref[...]).astype(BF16)
        o_ref[rows, :] = _dot(hn, w_ref[...])


def _norm_matmul_bf16(x, g, w, tm):
    m, d = x.shape
    n = w.shape[1]
    return pl.pallas_call(
        _norm_matmul_bf16_kernel,
        grid=(m // tm,),
        in_specs=[pl.BlockSpec((tm, d), lambda i: (i, 0)), _resident((1, d)), _resident((d, n))],
        out_specs=pl.BlockSpec((tm, n), lambda i: (i, 0)),
        out_shape=jax.ShapeDtypeStruct((m, n), F32),
        compiler_params=_params(("parallel",)),
        name="ssm_norm_in_proj",
    )(x, g.reshape(1, d), w)


SCAN_COLS = 1024


def _ssm_scan_kernel(u_ref, wb_ref, are_ref, aim_ref, wcr_ref, wci_ref, dsk_ref, h0r_ref, h0i_ref,
                     z_ref, sr_ref, si_ref, bur_ref, bui_ref, *, nb, tt):
    @pl.when(pl.program_id(1) == 0)
    def _():
        sr_ref[...] = h0r_ref[...]
        si_ref[...] = h0i_ref[...]

    u = u_ref[...]
    ub = u.astype(BF16)
    for q in range(QUADS):
        lhs = ub[:, QUAD_K * q:QUAD_K * (q + 1)]
        for h in range(2):
            res = _dot(lhs, wb_ref[q, QUAD_K * h:QUAD_K * (h + 1), :])
            for s in range(2):
                j = 4 * q + 2 * h + s
                bur_ref[:, LANES * j:LANES * (j + 1)] = res[:, MXU_DIM * s:MXU_DIM * s + LANES]
                bui_ref[:, LANES * j:LANES * (j + 1)] = res[:, MXU_DIM * s + LANES:MXU_DIM * (s + 1)]

    for cb in range(SSM_LANES // SCAN_COLS):
        cols = slice(cb * SCAN_COLS, (cb + 1) * SCAN_COLS)
        ar = are_ref[:, cols]
        ai = aim_ref[:, cols]
        for sg in range(nb // SUBLANES):
            seqs = slice(sg * SUBLANES, (sg + 1) * SUBLANES)
            sr = sr_ref[seqs, cols]
            si = si_ref[seqs, cols]
            for t in range(tt):
                rows = slice(t * nb + sg * SUBLANES, t * nb + (sg + 1) * SUBLANES)
                nr = ar * sr - ai * si + bur_ref[rows, cols]
                ni = ar * si + ai * sr + bui_ref[rows, cols]
                sr, si = nr, ni
                bur_ref[rows, cols] = sr
                bui_ref[rows, cols] = si
            sr_ref[seqs, cols] = sr
            si_ref[seqs, cols] = si

    for o in range(OCTS):
        kk = slice(o * OCT_K, (o + 1) * OCT_K)
        nn = slice(o * MXU_DIM, (o + 1) * MXU_DIM)
        y = _dot(bur_ref[:, kk].astype(BF16), wcr_ref[o]) + _dot(bui_ref[:, kk].astype(BF16), wci_ref[o])
        y = y + dsk_ref[:, nn] * u[:, nn]
        z_ref[:, nn] = jax.nn.gelu(y).astype(BF16)


def _ssm_scan(u, tables, d_skip, h0_re, h0_im, *, nb, tt, n_bt, n_tt):
    w_b, a_re_t, a_im_t, w_cr, w_ci = tables
    m, d = u.shape
    r = nb * tt
    assert m == r * n_bt * n_tt and nb % SUBLANES == 0
    rows = pl.BlockSpec((r, d), lambda b, t: (b * n_tt + t, 0))
    state = pl.BlockSpec((nb, SSM_LANES), lambda b, t: (b, 0))
    st_shape = jax.ShapeDtypeStruct((nb * n_bt, SSM_LANES), F32)
    return pl.pallas_call(
        functools.partial(_ssm_scan_kernel, nb=nb, tt=tt),
        grid=(n_bt, n_tt),
        in_specs=[rows, _resident(w_b.shape), _resident(a_re_t.shape), _resident(a_im_t.shape),
                  _resident(w_cr.shape), _resident(w_ci.shape), _resident((1, d)), state, state],
        out_specs=(rows, state, state),
        out_shape=(jax.ShapeDtypeStruct((m, d), BF16), st_shape, st_shape),
        scratch_shapes=[pltpu.VMEM((r, SSM_LANES), F32), pltpu.VMEM((r, SSM_LANES), F32)],
        compiler_params=_params(("parallel", "arbitrary")),
        name="ssm_scan",
    )(u, w_b, a_re_t, a_im_t, w_cr, w_ci, d_skip.reshape(1, d), h0_re, h0_im)


SCAN_SLABS = 8


def _ssm_scan_split_kernel(u_ref, wb_ref, are_ref, aim_ref, wcr_ref, wci_ref, dsk_ref, *rest, nb, tt):
    n_cast = (len(rest) - 7) // 2
    z_ref, sr_ref, si_ref = rest[n_cast:n_cast + 3]
    lhs_ref, bur_ref, bui_ref, y_ref = rest[-4:]
    _cast_blocks(rest[:n_cast], rest[n_cast + 3:-4])
    r = nb * tt
    i = pl.program_id(0)

    @pl.when(i == 0)
    def _():
        lhs_ref[...] = jnp.zeros_like(lhs_ref)
        sr_ref[...] = jnp.zeros_like(sr_ref)
        si_ref[...] = jnp.zeros_like(si_ref)

    def project_in(q):
        for b in range(nb):
            blk = u_ref[b, :, LANES * q:LANES * (q + 1)]
            for h in range(2):
                lhs_ref[q, h, pl.ds(2 * b + h, tt, stride=2 * nb), :] = blk
        lhs = jnp.concatenate([lhs_ref[q, 0], lhs_ref[q, 1]], axis=1).astype(BF16)
        res = _dot(lhs, wb_ref[q])
        for s in range(2):
            bur_ref[2 * q + s] = res[:, MXU_DIM * s:MXU_DIM * s + LANES]
            bui_ref[2 * q + s] = res[:, MXU_DIM * s + LANES:MXU_DIM * (s + 1)]

    def recur(slabs):
        ar = [are_ref[:, LANES * k:LANES * (k + 1)] for k in slabs]
        ai = [aim_ref[:, LANES * k:LANES * (k + 1)] for k in slabs]
        sr = [sr_ref[:, LANES * k:LANES * (k + 1)] for k in slabs]
        si = [si_ref[:, LANES * k:LANES * (k + 1)] for k in slabs]
        for t in range(tt):
            rows = slice(SUBLANES * t, SUBLANES * (t + 1))
            for n, k in enumerate(slabs):
                nr = ar[n] * sr[n] - ai[n] * si[n] + bur_ref[k, rows, :]
                ni = ar[n] * si[n] + ai[n] * sr[n] + bui_ref[k, rows, :]
                sr[n], si[n] = nr, ni
                bur_ref[k, rows, :] = nr
                bui_ref[k, rows, :] = ni
        for n, k in enumerate(slabs):
            sr_ref[:, LANES * k:LANES * (k + 1)] = sr[n]
            si_ref[:, LANES * k:LANES * (k + 1)] = si[n]

    def project_out(o):
        order = [(2 * (2 * o + ql) + s, h) for ql in range(2) for h in range(2) for s in range(2)]
        lre = jnp.concatenate([bur_ref[k, pl.ds(h, r, stride=2), :] for k, h in order], axis=1).astype(BF16)
        lim = jnp.concatenate([bui_ref[k, pl.ds(h, r, stride=2), :] for k, h in order], axis=1).astype(BF16)
        y = _dot(lre, wcr_ref[o]) + _dot(lim, wci_ref[o])
        for half in range(2):
            y_ref[half] = y[:, LANES * half:LANES * (half + 1)]
        for b in range(nb):
            for half in range(2):
                cols = slice(MXU_DIM * o + LANES * half, MXU_DIM * o + LANES * (half + 1))
                yb = y_ref[half, pl.ds(b, tt, stride=nb), :] + dsk_ref[:, cols] * u_ref[b, :, cols]
                z_ref[b, :, cols] = jax.nn.gelu(yb).astype(BF16)

    for q in range(QUADS):
        project_in(q)
    for k0 in range(0, HALF_SLABS, SCAN_SLABS):
        recur(range(k0, k0 + SCAN_SLABS))
    for o in range(OCTS):
        project_out(o)


def _ssm_scan_split(u, tables, d_skip, tt, casts):
    w_b, a_re_t, a_im_t, w_cr, w_ci = tables
    nb, seq, d = u.shape
    assert 2 * nb == SUBLANES and seq % tt == 0
    casts = [_SideCast(cw, layer, seq // tt) for cw, layer in casts]

    def split_lanes(a):
        halves = jnp.transpose(a[0].reshape(QUADS, 2, 2 * LANES), (1, 0, 2)).reshape(2, HALF_SLABS * LANES)
        return jnp.tile(halves, (nb, 1))

    rows = pl.BlockSpec((nb, tt, d), lambda i: (0, i, 0))
    st_shape = jax.ShapeDtypeStruct((SUBLANES, HALF_SLABS * LANES), F32)
    r = nb * tt
    state = pl.BlockSpec(st_shape.shape, lambda i: (0, 0))
    z, s_re, s_im, *w_cast = pl.pallas_call(
        functools.partial(_ssm_scan_split_kernel, nb=nb, tt=tt),
        grid=(seq // tt,),
        in_specs=[rows, _resident(w_b.shape), _resident(st_shape.shape), _resident(st_shape.shape),
                  _resident(w_cr.shape), _resident(w_ci.shape), _resident((1, d))] + [c.in_spec for c in casts],
        out_specs=[rows, state, state] + [c.out_spec for c in casts],
        out_shape=[jax.ShapeDtypeStruct((nb, seq, d), BF16), st_shape, st_shape] + [c.out_shape for c in casts],
        scratch_shapes=[pltpu.VMEM((QUADS, 2, 2 * r, LANES), F32),
                        pltpu.VMEM((HALF_SLABS, 2 * r, LANES), F32),
                        pltpu.VMEM((HALF_SLABS, 2 * r, LANES), F32),
                        pltpu.VMEM((2, r, LANES), F32)],
        compiler_params=pltpu.CompilerParams(dimension_semantics=("arbitrary",), vmem_limit_bytes=BIG_VMEM_LIMIT),
        name="ssm_scan_split",
    )(u, w_b, split_lanes(a_re_t), split_lanes(a_im_t), w_cr, w_ci, d_skip.reshape(1, d),
      *[c.operand for c in casts])

    def join(s):
        return jnp.transpose(s.reshape(nb, 2, QUADS, 2 * LANES), (0, 2, 1, 3)).reshape(nb, SSM_LANES)

    return (z, join(s_re), join(s_im), *w_cast)


SSM_OUT_COLS = 512


def _ssm_out_kernel(z_ref, x_ref, w_ref, *rest):
    o_ref = rest[len(rest) // 2]
    _cast_blocks(rest[:len(rest) // 2], rest[len(rest) // 2 + 1:])
    z = z_ref[...]
    for c in range(D_MODEL // SSM_OUT_COLS):
        cols = slice(c * SSM_OUT_COLS, (c + 1) * SSM_OUT_COLS)
        gate_cols = slice(D_MODEL + c * SSM_OUT_COLS, D_MODEL + (c + 1) * SSM_OUT_COLS)
        val = _dot(z, w_ref[:, cols])
        gate = _dot(z, w_ref[:, gate_cols])
        o_ref[:, cols] = x_ref[:, cols] + val * jax.nn.sigmoid(gate)


def _ssm_out(z, x, w, tm, casts=()):
    m, d = x.shape
    rows = pl.BlockSpec((tm, d), lambda i: (i, 0))
    casts = [_SideCast(cw, layer, m // tm) for cw, layer in casts]
    return pl.pallas_call(
        _ssm_out_kernel,
        grid=(m // tm,),
        in_specs=[rows, rows, _resident(w.shape)] + [c.in_spec for c in casts],
        out_specs=[rows] + [c.out_spec for c in casts],
        out_shape=[jax.ShapeDtypeStruct((m, d), F32)] + [c.out_shape for c in casts],
        compiler_params=_params(("parallel",)),
        name="ssm_out_glu",
    )(z, x, w, *[c.operand for c in casts])


FFN_FINAL_ROW_BLOCKS = 2


def _ffn_kernel(x_ref, g_ref, wg_ref, wu_ref, wd_ref, gf_ref, *rest, final_norm):
    n_cast = (len(rest) - 2) // 2
    o_ref, hn_ref = rest[n_cast], rest[-1]
    _cast_blocks(rest[:n_cast], rest[n_cast + 1:-1])
    h = pl.program_id(1)

    def hidden_tile(hn):
        act = (jax.nn.silu(_dot(hn, wg_ref[...])) * _dot(hn, wu_ref[...])).astype(BF16)
        return _dot(act, wd_ref[...])

    @pl.when(h == 0)
    def _():
        hn = _rms(x_ref[...], g_ref[...]).astype(BF16)
        hn_ref[...] = hn
        o_ref[...] = x_ref[...] + hidden_tile(hn)

    last = pl.num_programs(1) - 1

    @pl.when((h != 0) & (h != last) if final_norm else h != 0)
    def _():
        o_ref[...] += hidden_tile(hn_ref[...])

    if final_norm:
        @pl.when(h == last)
        def _():
            hn = hn_ref[...]
            act = (jax.nn.silu(_dot(hn, wg_ref[...])) * _dot(hn, wu_ref[...])).astype(BF16)
            rb = o_ref.shape[0] // FFN_FINAL_ROW_BLOCKS
            for r0 in range(0, o_ref.shape[0], rb):
                rows = slice(r0, r0 + rb)
                o_ref[rows, :] = _rms(o_ref[rows, :] + _dot(act[rows, :], wd_ref[...]), gf_ref[...])


def _ffn(x, g, w_gate_up, w_down, g_final, *, tm, th, final_norm, casts=()):
    m, d = x.shape
    n_h = FFN_HIDDEN // th
    steps = (m // tm) * n_h
    rows = pl.BlockSpec((tm, d), lambda i, h: (i, 0))
    casts = [_SideCast(cw, layer, _slice_count(cw.shape[1], steps), lambda i, h: i * n_h + h) for cw, layer in casts]
    return pl.pallas_call(
        functools.partial(_ffn_kernel, final_norm=final_norm),
        grid=(m // tm, n_h),
        in_specs=[rows,
                  _resident((1, d)),
                  pl.BlockSpec((d, th), lambda i, h: (0, h)),
                  pl.BlockSpec((d, th), lambda i, h: (0, n_h + h)),
                  pl.BlockSpec((th, d), lambda i, h: (h, 0)),
                  _resident((1, d))] + [c.in_spec for c in casts],
        out_specs=[rows] + [c.out_spec for c in casts],
        out_shape=[jax.ShapeDtypeStruct((m, d), F32)] + [c.out_shape for c in casts],
        scratch_shapes=[pltpu.VMEM((tm, d), BF16)],
        compiler_params=pltpu.CompilerParams(dimension_semantics=("arbitrary", "arbitrary"),
                                             vmem_limit_bytes=BIG_VMEM_LIMIT),
        name="ffn_swiglu",
    )(x, g.reshape(1, d), w_gate_up, w_gate_up, w_down, g_final.reshape(1, d), *[c.operand for c in casts])


def _gmlp_gate_inputs(x, g, win_ref, v_gain):
    hn = _rms(x, g).astype(BF16)
    v = jax.nn.gelu(_dot(hn, win_ref[:, GMLP_WIDTH:]))
    vc = v - jnp.mean(v, axis=-1, keepdims=True)
    vn = (vc * lax.rsqrt(jnp.mean(vc * vc, axis=-1, keepdims=True) + EPS)) * v_gain
    u = jax.nn.gelu(_dot(hn, win_ref[:, :GMLP_WIDTH]))
    return u, vn


def _gmlp_prompt_kernel(x_ref, g_ref, win_ref, vg_ref, ws_ref, bs_ref, wout_ref, *rest):
    n_cast = (len(rest) - 2) // 2
    o_ref, gate_ref = rest[n_cast], rest[-1]
    _cast_blocks(rest[:n_cast], rest[n_cast + 1:-1])
    x = x_ref[...]
    u, vn = _gmlp_gate_inputs(x, g_ref[...], win_ref, vg_ref[...])
    vb = vn.astype(BF16)
    q_idx = lax.broadcasted_iota(jnp.int32, (CHUNK, CHUNK), 0)
    k_idx = lax.broadcasted_iota(jnp.int32, (CHUNK, CHUNK), 1)
    causal = k_idx <= q_idx
    for h in range(GMLP_HEADS):
        cols = slice(h * GMLP_HEAD_DIM, (h + 1) * GMLP_HEAD_DIM)
        ws = jnp.where(causal, ws_ref[h], 0.0).astype(BF16)
        bias = bs_ref[:, h:h + 1]
        for c in range(0, x.shape[0] // CHUNK, 2):
            lo = slice(c * CHUNK, (c + 1) * CHUNK)
            hi = slice((c + 1) * CHUNK, (c + 2) * CHUNK)
            s = _dot(ws, jnp.concatenate([vb[lo, cols], vb[hi, cols]], axis=1)) + bias
            gate_ref[lo, cols] = (u[lo, cols] * s[:, :GMLP_HEAD_DIM]).astype(BF16)
            gate_ref[hi, cols] = (u[hi, cols] * s[:, GMLP_HEAD_DIM:]).astype(BF16)
    o_ref[...] = x + _dot(gate_ref[...], wout_ref[...])


def _gmlp_sample_kernel(x_ref, g_ref, win_ref, vg_ref, wq_ref, bq_ref, wout_ref, o_ref, v_ref, *, steps):
    x = x_ref[...]
    u, vn = _gmlp_gate_inputs(x, g_ref[...], win_ref, vg_ref[...])
    v_ref[...] = vn
    nseq = x.shape[0] // steps
    gates = []
    for q in range(steps):
        s = bq_ref[q:q + 1, :]
        for k in range(q + 1):
            s = s + wq_ref[q * steps + k:q * steps + k + 1, :] * vn[k * nseq:(k + 1) * nseq, :]
        gates.append(u[q * nseq:(q + 1) * nseq, :] * s)
    gate = jnp.concatenate(gates, axis=0).astype(BF16)
    o_ref[...] = x + _dot(gate, wout_ref[...])


def _gmlp_prompt(x, g, w_in, v_gain, w_spatial, b_spatial_t, w_out, tm, casts=()):
    m, d = x.shape
    rows = pl.BlockSpec((tm, d), lambda i: (i, 0))
    casts = [_SideCast(cw, layer, m // tm) for cw, layer in casts]
    return pl.pallas_call(
        _gmlp_prompt_kernel,
        grid=(m // tm,),
        in_specs=[rows, _resident((1, d)), _resident(w_in.shape), _resident((1, GMLP_WIDTH)),
                  _resident(w_spatial.shape), _resident(b_spatial_t.shape), _resident(w_out.shape)]
                 + [c.in_spec for c in casts],
        out_specs=[rows] + [c.out_spec for c in casts],
        out_shape=[jax.ShapeDtypeStruct((m, d), F32)] + [c.out_shape for c in casts],
        scratch_shapes=[pltpu.VMEM((tm, GMLP_WIDTH), BF16)],
        compiler_params=pltpu.CompilerParams(dimension_semantics=("parallel",),
                                             vmem_limit_bytes=BIG_VMEM_LIMIT if casts else VMEM_LIMIT),
        name="gmlp_prompt",
    )(x, g.reshape(1, d), w_in, v_gain.reshape(1, GMLP_WIDTH), w_spatial, b_spatial_t, w_out,
      *[c.operand for c in casts])


def _gmlp_sample(x, g, w_in, v_gain, w_q, b_q, w_out, tm, steps):
    m, d = x.shape
    rows = pl.BlockSpec((tm, d), lambda i: (i, 0))
    return pl.pallas_call(
        functools.partial(_gmlp_sample_kernel, steps=steps),
        grid=(m // tm,),
        in_specs=[rows, _resident((1, d)), _resident(w_in.shape), _resident((1, GMLP_WIDTH)),
                  _resident(w_q.shape), _resident(b_q.shape), _resident(w_out.shape)],
        out_specs=(rows, pl.BlockSpec((tm, GMLP_WIDTH), lambda i: (i, 0))),
        out_shape=(jax.ShapeDtypeStruct((m, d), F32), jax.ShapeDtypeStruct((m, GMLP_WIDTH), F32)),
        compiler_params=_params(("parallel",)),
        name="gmlp_sample",
    )(x, g.reshape(1, d), w_in, v_gain.reshape(1, GMLP_WIDTH), w_q, b_q, w_out)


ROW_TILE = 512
FFN_ROW_TILE = 1024
FFN_HIDDEN_TILE = 512
PROMPT_SCAN_STEPS = 64
SAMPLE_TILE_SEQS = 64


def kernel(x_prompt, x_sample, state_ssm_re, state_ssm_im, norm_mix, norm_ffn, norm_final, ssm_w_in, ssm_lambda_re, ssm_lambda_im, ssm_log_dt, ssm_b_re, ssm_b_im, ssm_c_re, ssm_c_im, ssm_d, ssm_w_out, gmlp_w_in, gmlp_v_gain, gmlp_w_spatial, gmlp_b_spatial, gmlp_w_out, ffn_w_gate_up, ffn_w_down):
    bsz, seq, d = x_prompt.shape
    dbsz, dseq, _ = x_sample.shape

    tables = _ssm_tables(ssm_lambda_re[0], ssm_lambda_im[0], ssm_log_dt[0], ssm_b_re[0], ssm_b_im[0],
                         ssm_c_re[0], ssm_c_im[0])

    def ffn(x_rows, layer, w_gu, w_dn, ffn_tm, casts=()):
        return _ffn(x_rows, norm_ffn[layer], w_gu, w_dn, norm_final, tm=ffn_tm, th=FFN_HIDDEN_TILE,
                    final_norm=layer == 1, casts=casts)

    xp = x_prompt.reshape(bsz * seq, d)
    u, w_in0, w_out0 = _norm_matmul(xp, norm_mix[0], ssm_w_in, 0, ROW_TILE, casts=[(ssm_w_out, 0)])
    z, p_re, p_im, w_gu0, g_w_in = _ssm_scan_split(u.reshape(bsz, seq, d), tables, ssm_d[0], PROMPT_SCAN_STEPS,
                                                   casts=[(ffn_w_gate_up, 0), (gmlp_w_in, 0)])
    xp1, w_dn0, g_w_out = _ssm_out(z.reshape(bsz * seq, d), xp, w_out0, ROW_TILE,
                                   casts=[(ffn_w_down, 0), (gmlp_w_out, 0)])
    xp2, w_gu1, w_dn1 = ffn(xp1, 0, w_gu0, w_dn0, FFN_ROW_TILE, casts=[(ffn_w_gate_up, 1), (ffn_w_down, 1)])
    xp3, = _gmlp_prompt(xp2, norm_mix[1], g_w_in, gmlp_v_gain[0], gmlp_w_spatial[0],
                        jnp.transpose(gmlp_b_spatial[0]), g_w_out, ROW_TILE)
    y_prompt = ffn(xp3, 1, w_gu1, w_dn1, FFN_ROW_TILE)[0].reshape(bsz, seq, d)

    n_bt = dbsz // SAMPLE_TILE_SEQS
    tile_rows = SAMPLE_TILE_SEQS * dseq

    def to_rows(a):
        return jnp.transpose(a.reshape(n_bt, SAMPLE_TILE_SEQS, dseq, d), (0, 2, 1, 3)).reshape(dbsz * dseq, d)

    def from_rows(a):
        return jnp.transpose(a.reshape(n_bt, dseq, SAMPLE_TILE_SEQS, d), (0, 2, 1, 3)).reshape(dbsz, dseq, d)

    xs = to_rows(x_sample)
    h0_re = state_ssm_re[0].reshape(dbsz, SSM_LANES)
    h0_im = state_ssm_im[0].reshape(dbsz, SSM_LANES)
    us = _norm_matmul_bf16(xs, norm_mix[0], w_in0, dbsz * dseq)
    zs, s_re, s_im = _ssm_scan(us, tables, ssm_d[0], h0_re, h0_im, nb=SAMPLE_TILE_SEQS, tt=dseq, n_bt=n_bt, n_tt=1)
    xs1, = _ssm_out(zs, xs, w_out0, tile_rows)
    xs2, = ffn(xs1, 0, w_gu0, w_dn0, dbsz * dseq)
    w_q = jnp.repeat(gmlp_w_spatial[0][:, :dseq, :dseq].reshape(GMLP_HEADS, dseq * dseq).T, GMLP_HEAD_DIM, axis=1)
    b_q = jnp.repeat(gmlp_b_spatial[0][:, :dseq].T, GMLP_HEAD_DIM, axis=1)
    xs3, v_rows = _gmlp_sample(xs2, norm_mix[1], g_w_in, gmlp_v_gain[0], w_q, b_q, g_w_out, tile_rows, dseq)
    y_sample = from_rows(ffn(xs3, 1, w_gu1, w_dn1, dbsz * dseq)[0])

    state_shape = (1, -1, SSM_GROUPS, SSM_STATE)
    return (y_prompt, y_sample,
            p_re.reshape(state_shape), p_im.reshape(state_shape),
            s_re.reshape(state_shape), s_im.reshape(state_shape),
            from_rows(v_rows)[None])
```
